```python
import math
import jax, jax.numpy as jnp
from jax import lax
import numpy as np

D_MODEL = 2048
BATCH = 1
SEQ = 8192
DEPTH = 1

MIX_WIDTH = D_MODEL
HGRN_WIDTH = MIX_WIDTH // 2
HGRN_DK = 128
HGRN_HEADS = HGRN_WIDTH // HGRN_DK
HGRN_DV = HGRN_WIDTH // HGRN_HEADS
FOURIER_WIDTH = MIX_WIDTH - HGRN_WIDTH
FOURIER_GROUPS = 4
FOURIER_GDIM = FOURIER_WIDTH // FOURIER_GROUPS
CHUNK = 64
D_FF = int(math.ceil((8 * D_MODEL / 3) / 256) * 256)
PROJ_WIDTH = 5 * HGRN_WIDTH + FOURIER_WIDTH
DEEPNORM_ALPHA = (2.0 * DEPTH) ** 0.25
DEEPNORM_BETA = (8.0 * DEPTH) ** -0.25
LN_EPS = 1e-5
RMS_EPS = 1e-6

kernel_name = "hgrn2_fourier_hybrid_deepnorm_encoder"


def _layernorm(x, g, b):
    xf = x.astype(jnp.float32)
    mu = jnp.mean(xf, axis=-1, keepdims=True)
    var = jnp.mean(jnp.square(xf - mu), axis=-1, keepdims=True)
    y = (xf - mu) * lax.rsqrt(var + LN_EPS) * g.astype(jnp.float32) + b.astype(jnp.float32)
    return y.astype(x.dtype)


def _chunk_recurrence(q, k, v, logf):
    B, T, H, K = q.shape
    V = v.shape[-1]
    N = T // CHUNK
    qc = q.reshape(B, N, CHUNK, H, K)
    kc = k.reshape(B, N, CHUNK, H, K)
    vc = v.reshape(B, N, CHUNK, H, V)
    bcum = jnp.cumsum(logf.reshape(B, N, CHUNK, H, K), axis=2)
    b_last = bcum[:, :, -1:]
    q_dec = qc * jnp.exp(bcum)
    k_inv = kc * jnp.exp(-bcum)
    k_end = kc * jnp.exp(b_last - bcum)
    scores = jnp.einsum('bnthk,bnshk->bnhts', q_dec, k_inv)
    tri = jnp.tril(jnp.ones((CHUNK, CHUNK), dtype=bool))
    scores = jnp.where(tri, scores, 0.0)
    o_intra = jnp.einsum('bnhts,bnshv->bnthv', scores, vc)
    d_state = jnp.einsum('bnshk,bnshv->bnhkv', k_end, vc)
    decay = jnp.exp(b_last[:, :, 0])

    def step(S, inp):
        d, ds = inp
        return d[..., None] * S + ds, S

    S0 = jnp.zeros((B, H, K, V), dtype=q.dtype)
    _, S_prev = lax.scan(step, S0, (jnp.moveaxis(decay, 1, 0), jnp.moveaxis(d_state, 1, 0)))
    S_prev = jnp.moveaxis(S_prev, 0, 1)
    o_inter = jnp.einsum('bnthk,bnhkv->bnthv', q_dec, S_prev)
    return (o_intra + o_inter).reshape(B, T, H, V)


def _forget(z, lb):
    f = lb + (1.0 - lb) * jax.nn.sigmoid(z.astype(jnp.float32))
    return jnp.log(f), 1.0 - f


def _hgrn2_group(q_raw, i_raw, zf, zb, g_raw, lb_f, lb_b, g_norm):
    B, T, _ = q_raw.shape
    shp_k = (B, T, HGRN_HEADS, HGRN_DK)
    shp_v = (B, T, HGRN_HEADS, HGRN_DV)
    q = jax.nn.silu(q_raw.astype(jnp.float32)).reshape(shp_k)
    v = i_raw.astype(jnp.float32).reshape(shp_v)
    logf_f, k_f = _forget(zf, lb_f)
    logf_b, k_b = _forget(zb, lb_b)
    o_fwd = _chunk_recurrence(q, k_f.reshape(shp_k), v, logf_f.reshape(shp_k))
    flip = lambda a: jnp.flip(a, axis=1)
    o_bwd = flip(_chunk_recurrence(flip(q), flip(k_b.reshape(shp_k)), flip(v), flip(logf_b.reshape(shp_k))))
    o = o_fwd + o_bwd
    o = o * lax.rsqrt(jnp.mean(jnp.square(o), axis=-1, keepdims=True) + RMS_EPS) * g_norm.astype(jnp.float32)
    o = o * jax.nn.silu(g_raw.astype(jnp.float32)).reshape(shp_v)
    return o.reshape(B, T, HGRN_WIDTH).astype(q_raw.dtype)


def _fourier_group(u):
    B, T, _ = u.shape
    ug = u.astype(jnp.float32).reshape(B, T, FOURIER_GROUPS, FOURIER_GDIM)
    y = jnp.real(jnp.fft.fft2(ug, axes=(1, 3), norm='ortho'))
    return y.reshape(B, T, FOURIER_WIDTH).astype(u.dtype)


def setup_inputs(seed: int = 0) -> dict:
    key = jax.random.key(seed)
    ks = jax.random.split(key, 16)
    f32 = jnp.float32
    x = jax.random.normal(ks[0], (BATCH, SEQ, D_MODEL), f32)
    ln_emb_g = 1.0 + 0.02 * jax.random.normal(ks[1], (D_MODEL,), f32)
    ln_emb_b = 0.02 * jax.random.normal(ks[2], (D_MODEL,), f32)
    w_in = jax.random.normal(ks[3], (DEPTH, D_MODEL, PROJ_WIDTH), f32) * D_MODEL ** -0.5
    lb_fwd_logits = 0.5 * jax.random.normal(ks[4], (DEPTH + 1, HGRN_WIDTH), f32)
    lb_bwd_logits = 0.5 * jax.random.normal(ks[5], (DEPTH + 1, HGRN_WIDTH), f32)
    g_norm = 1.0 + 0.02 * jax.random.normal(ks[6], (DEPTH, HGRN_DV), f32)
    w_out = jax.random.normal(ks[7], (DEPTH, MIX_WIDTH, D_MODEL), f32) * (MIX_WIDTH ** -0.5 * DEEPNORM_BETA)
    ln1_g = 1.0 + 0.02 * jax.random.normal(ks[8], (DEPTH, D_MODEL), f32)
    ln1_b = 0.02 * jax.random.normal(ks[9], (DEPTH, D_MODEL), f32)
    w_gate = jax.random.normal(ks[10], (DEPTH, D_MODEL, D_FF), f32) * D_MODEL ** -0.5
    w_up = jax.random.normal(ks[11], (DEPTH, D_MODEL, D_FF), f32) * D_MODEL ** -0.5
    w_down = jax.random.normal(ks[12], (DEPTH, D_FF, D_MODEL), f32) * (D_FF ** -0.5 * DEEPNORM_BETA)
    ln2_g = 1.0 + 0.02 * jax.random.normal(ks[13], (DEPTH, D_MODEL), f32)
    ln2_b = 0.02 * jax.random.normal(ks[14], (DEPTH, D_MODEL), f32)
    return {"x": x, "ln_emb_g": ln_emb_g, "ln_emb_b": ln_emb_b, "w_in": w_in,
            "lb_fwd_logits": lb_fwd_logits, "lb_bwd_logits": lb_bwd_logits, "g_norm": g_norm,
            "w_out": w_out, "ln1_g": ln1_g, "ln1_b": ln1_b, "w_gate": w_gate, "w_up": w_up,
            "w_down": w_down, "ln2_g": ln2_g, "ln2_b": ln2_b}


def reference(x, ln_emb_g, ln_emb_b, w_in, lb_fwd_logits, lb_bwd_logits, g_norm,
              w_out, ln1_g, ln1_b, w_gate, w_up, w_down, ln2_g, ln2_b):
    lb_fwd_all = jnp.cumsum(jax.nn.softmax(lb_fwd_logits.astype(jnp.float32), axis=0), axis=0)
    lb_bwd_all = jnp.cumsum(jax.nn.softmax(lb_bwd_logits.astype(jnp.float32), axis=0), axis=0)
    h = _layernorm(x, ln_emb_g, ln_emb_b)
    W = HGRN_WIDTH
    for l in range(DEPTH):
        u = h @ w_in[l]
        q_raw = u[..., 0 * W:1 * W]
        i_raw = u[..., 1 * W:2 * W]
        zf = u[..., 2 * W:3 * W]
        zb = u[..., 3 * W:4 * W]
        g_raw = u[..., 4 * W:5 * W]
        u_four = u[..., 5 * W:5 * W + FOURIER_WIDTH]
        o_hgrn = _hgrn2_group(q_raw, i_raw, zf, zb, g_raw, lb_fwd_all[l], lb_bwd_all[l], g_norm[l])
        o_four = _fourier_group(u_four)
        mix = jnp.concatenate([o_hgrn, o_four], axis=-1) @ w_out[l]
        h = _layernorm(DEEPNORM_ALPHA * h + mix, ln1_g[l], ln1_b[l])
        ffn = (jax.nn.silu(h @ w_gate[l]) * (h @ w_up[l])) @ w_down[l]
        h = _layernorm(DEEPNORM_ALPHA * h + ffn, ln2_g[l], ln2_b[l])
    return h
```

```python
import functools
import math

import numpy as np
import jax
import jax.numpy as jnp
from jax import lax
from jax.experimental import pallas as pl
from jax.experimental.pallas import tpu as pltpu

LN_EPS = 1e-5
RMS_EPS = 1e-6
HEAD_DIM = 128
FOURIER_GDIM = 256
FFT_N1 = 64
HGRN_CHUNK = 128
V7X_VMEM_CAP = 56 * 1024 * 1024

BF16 = jnp.bfloat16
F32 = jnp.float32


def _vmem_limit(pipelined_bytes, resident_bytes):
    return int(min(V7X_VMEM_CAP, 2 * pipelined_bytes + resident_bytes + (4 << 20)))


def _nbytes(shape, dtype):
    return int(np.prod(shape)) * jnp.dtype(dtype).itemsize


def _dot(a, b):
    return jnp.dot(a, b, preferred_element_type=F32)


def _layernorm(x, g, b):
    mu = jnp.mean(x, axis=-1, keepdims=True)
    xc = x - mu
    var = jnp.mean(xc * xc, axis=-1, keepdims=True)
    return xc * lax.rsqrt(var + LN_EPS) * g + b


def _sigmoid(x):
    return 1.0 / (1.0 + jnp.exp(-x))


def _ln_proj_kernel(x_ref, g_ref, b_ref, wz_ref, wq_ref, wf_ref, z_ref, q_ref, f_ref, h_scr):
    @pl.when(pl.program_id(1) == 0)
    def _():
        h_scr[...] = _layernorm(x_ref[...], g_ref[...], b_ref[...]).astype(BF16)

    h = h_scr[...]
    z_ref[...] = _dot(h, wz_ref[...])
    q_ref[...] = _dot(h, wq_ref[...]).astype(BF16)
    f_ref[...] = _dot(h, wf_ref[...]).astype(BF16)


def _ln_proj(x, g, b, wz, wq, wf, *, tm=512, nj=4):
    t, d = x.shape
    nz, nq, nf = wz.shape[1] // nj, wq.shape[1] // nj, wf.shape[1] // nj
    pipelined = (_nbytes((tm, d), F32) + _nbytes((d, nz + nq + nf), BF16)
                 + _nbytes((tm, nz), F32) + _nbytes((tm, nq + nf), BF16))
    resident = _nbytes((tm, d), BF16) + 2 * _nbytes((tm, nz + nq + nf), F32)
    return pl.pallas_call(
        _ln_proj_kernel,
        grid=(t // tm, nj),
        in_specs=[
            pl.BlockSpec((tm, d), lambda i, j: (i, 0)),
            pl.BlockSpec((1, d), lambda i, j: (0, 0)),
            pl.BlockSpec((1, d), lambda i, j: (0, 0)),
            pl.BlockSpec((d, nz), lambda i, j: (0, j)),
            pl.BlockSpec((d, nq), lambda i, j: (0, j)),
            pl.BlockSpec((d, nf), lambda i, j: (0, j)),
        ],
        out_specs=[
            pl.BlockSpec((tm, nz), lambda i, j: (i, j)),
            pl.BlockSpec((tm, nq), lambda i, j: (i, j)),
            pl.BlockSpec((tm, nf), lambda i, j: (i, j)),
        ],
        out_shape=[
            jax.ShapeDtypeStruct((t, wz.shape[1]), F32),
            jax.ShapeDtypeStruct((t, wq.shape[1]), BF16),
            jax.ShapeDtypeStruct((t, wf.shape[1]), BF16),
        ],
        scratch_shapes=[pltpu.VMEM((tm, d), BF16)],
        compiler_params=pltpu.CompilerParams(
            dimension_semantics=("parallel", "arbitrary"),
            vmem_limit_bytes=_vmem_limit(pipelined, resident)),
        name="ln_proj",
    )(x, g, b, wz, wq, wf)


def _hgrn_kernel(lbf_ref, lbb_ref, gn_ref, q_ref, v_ref, g_ref, z_ref, o_ref, st_ref, of_ref,
                 *, layer, nblk, blk_rows):
    c = HGRN_CHUNK
    half = c // 2
    nchunk = blk_rows // c
    phase = pl.program_id(1)
    t = pl.program_id(2)

    @pl.when(t == 0)
    def _():
        st_ref[...] = jnp.zeros_like(st_ref)

    def scan_block(fwd):
        logits = (lbf_ref if fwd else lbb_ref)[...]
        e = jnp.exp(logits - jnp.max(logits, axis=0, keepdims=True))
        lb = jnp.sum(e[:layer + 1], axis=0, keepdims=True) / jnp.sum(e, axis=0, keepdims=True)

        row = lax.broadcasted_iota(jnp.int32, (c, c), 0)
        col = lax.broadcasted_iota(jnp.int32, (c, c), 1)
        tri = (col <= row) if fwd else (col >= row)
        tri_f = tri.astype(F32)

        blk = t if fwd else nblk - 1 - t
        st = st_ref[...]
        for ci in range(nchunk):
            r0 = (ci if fwd else nchunk - 1 - ci) * c
            rows = pl.ds(r0, c)
            f = lb + (1.0 - lb) * _sigmoid(z_ref[rows, :])
            logf = jnp.log(f)
            kk = 1.0 - f
            qf = q_ref[rows, :].astype(F32)
            qs = qf * _sigmoid(qf)
            v = v_ref[rows, :]

            b = jnp.dot(tri_f, logf, precision=lax.Precision.HIGHEST, preferred_element_type=F32)
            b_mid = b[half - 1:half, :] if fwd else b[half:half + 1, :]
            b_end = b[c - 1:c, :] if fwd else b[0:1, :]
            qd = (qs * jnp.exp(b - b_mid)).astype(BF16)
            ki = (kk * jnp.exp(b_mid - b)).astype(BF16)
            ke = (kk * jnp.exp(b_end - b)).astype(BF16)

            s_mid = (st * jnp.exp(b_mid)).astype(BF16)
            rhs_t = jnp.concatenate([ki, s_mid], axis=0)
            res = lax.dot_general(qd, rhs_t, (((1,), (1,)), ((), ())), preferred_element_type=F32)
            scores = jnp.where(tri, res[:, :c], 0.0).astype(BF16)
            o = _dot(scores, v) + res[:, c:]
            dst = lax.dot_general(v, ke, (((0,), (0,)), ((), ())), preferred_element_type=F32)
            st = jnp.exp(b_end) * st + dst

            grow = pl.ds(pl.multiple_of(blk * blk_rows + r0, c), c)
            if fwd:
                of_ref[grow, :] = o
            else:
                ot = o + of_ref[grow, :]
                ot = ot * lax.rsqrt(jnp.mean(ot * ot, axis=-1, keepdims=True) + RMS_EPS) * gn_ref[...]
                gf = g_ref[rows, :].astype(F32)
                o_ref[rows, :] = (ot * (gf * _sigmoid(gf))).astype(BF16)
        st_ref[...] = st

    pl.when(phase == 0)(functools.partial(scan_block, True))
    pl.when(phase == 1)(functools.partial(scan_block, False))


def _hgrn(lbf_logits, lbb_logits, g_norm, qig, z, *, layer, blk_rows=1024):
    t = z.shape[0]
    w = z.shape[1] // 2
    heads = w // HEAD_DIM
    nblk = t // blk_rows
    d = HEAD_DIM
    slots = lbf_logits.shape[0]

    def tblk(h, p, i):
        return i + p * (nblk - 1 - 2 * i)

    pipelined = 4 * _nbytes((blk_rows, d), BF16) + _nbytes((blk_rows, d), F32)
    resident = _nbytes((t, d), F32) + 64 * _nbytes((d, d), F32)
    return pl.pallas_call(
        functools.partial(_hgrn_kernel, layer=layer, nblk=nblk, blk_rows=blk_rows),
        grid=(heads, 2, nblk),
        in_specs=[
            pl.BlockSpec((slots, d), lambda h, p, i: (0, h)),
            pl.BlockSpec((slots, d), lambda h, p, i: (0, h)),
            pl.BlockSpec((1, d), lambda h, p, i: (0, 0)),
            pl.BlockSpec((blk_rows, d), lambda h, p, i: (tblk(h, p, i), h)),
            pl.BlockSpec((blk_rows, d), lambda h, p, i: (tblk(h, p, i), heads + h)),
            pl.BlockSpec((blk_rows, d), lambda h, p, i: (tblk(h, p, i), 2 * heads + h)),
            pl.BlockSpec((blk_rows, d), lambda h, p, i: (tblk(h, p, i), p * heads + h)),
        ],
        out_specs=pl.BlockSpec((blk_rows, d), lambda h, p, i: (nblk - 1 - p * i, h)),
        out_shape=jax.ShapeDtypeStruct((t, w), BF16),
        scratch_shapes=[pltpu.VMEM((d, d), F32), pltpu.VMEM((t, d), F32)],
        compiler_params=pltpu.CompilerParams(
            dimension_semantics=("parallel", "arbitrary", "arbitrary"),
            vmem_limit_bytes=_vmem_limit(pipelined, resident)),
        name="hgrn",
    )(lbf_logits, lbb_logits, g_norm, qig, qig, qig, z)


def _dft_constants(t, n1):
    n2 = t // n1
    k1 = np.arange(n1)[None, :, None]
    t1 = np.arange(n1)[None, None, :]
    t2 = np.arange(n2)[:, None, None]
    ang = 2.0 * np.pi * ((k1 * (n2 * t1 + t2)) % t) / t
    g = np.concatenate([np.cos(ang), -np.sin(ang)], axis=1) / math.sqrt(n1)
    k2 = np.arange(n2)[:, None]
    s2 = np.arange(n2)[None, :]
    ang2 = 2.0 * np.pi * ((k2 * s2) % n2) / n2
    c2, sn2 = np.cos(ang2) / math.sqrt(n2), np.sin(ang2) / math.sqrt(n2)
    f2 = np.block([[c2, sn2], [-sn2, c2]])
    m = np.arange(FOURIER_GDIM)
    ang3 = 2.0 * np.pi * ((m[:, None] * m[None, :]) % FOURIER_GDIM) / FOURIER_GDIM
    c3 = np.cos(ang3) / math.sqrt(FOURIER_GDIM)
    s3 = np.sin(ang3) / math.sqrt(FOURIER_GDIM)
    as_bf16 = lambda a: jnp.asarray(a, dtype=F32).astype(BF16)
    return as_bf16(g), as_bf16(f2), as_bf16(c3), as_bf16(s3)


def _fft_t1_kernel(u_ref, g_ref, w_ref, *, tb, n1, m):
    for j in range(tb):
        r = _dot(g_ref[j], u_ref[:, j * m:(j + 1) * m])
        w_ref[0, j] = r[:n1].astype(BF16)
        w_ref[1, j] = r[n1:].astype(BF16)


def _fft_t1(four, g, *, tb=8):
    t, m = four.shape
    n2, _, n1 = g.shape
    u = four.reshape(n1, n2 * m)
    pipelined = (_nbytes((n1, tb * m), BF16) + _nbytes((tb, 2 * n1, n1), BF16)
                 + _nbytes((2, tb, n1, m), BF16))
    return pl.pallas_call(
        functools.partial(_fft_t1_kernel, tb=tb, n1=n1, m=m),
        grid=(n2 // tb,),
        in_specs=[
            pl.BlockSpec((n1, tb * m), lambda s: (0, s)),
            pl.BlockSpec((tb, 2 * n1, n1), lambda s: (s, 0, 0)),
        ],
        out_specs=pl.BlockSpec((2, tb, n1, m), lambda s: (0, s, 0, 0)),
        out_shape=jax.ShapeDtypeStruct((2, n2, n1, m), BF16),
        compiler_params=pltpu.CompilerParams(
            dimension_semantics=("parallel",),
            vmem_limit_bytes=_vmem_limit(pipelined, 2 * _nbytes((2 * n1, m), F32))),
        name="fft_t1",
    )(u, g)


def _fft_t2_kernel(f_ref, w_ref, p_ref):
    p_ref[...] = _dot(f_ref[...], w_ref[...]).astype(BF16)


def _fft_t2(w4, f2, *, cb=2048):
    _, n2, n1, m = w4.shape
    w = w4.reshape(2 * n2, n1 * m)
    pipelined = 2 * _nbytes((2 * n2, cb), BF16) + _nbytes((2 * n2, 2 * n2), BF16)
    p = pl.pallas_call(
        _fft_t2_kernel,
        grid=(n1 * m // cb,),
        in_specs=[
            pl.BlockSpec((2 * n2, 2 * n2), lambda s: (0, 0)),
            pl.BlockSpec((2 * n2, cb), lambda s: (0, s)),
        ],
        out_specs=pl.BlockSpec((2 * n2, cb), lambda s: (0, s)),
        out_shape=jax.ShapeDtypeStruct((2 * n2, n1 * m), BF16),
        compiler_params=pltpu.CompilerParams(
            dimension_semantics=("parallel",),
            vmem_limit_bytes=_vmem_limit(pipelined, _nbytes((2 * n2, cb), F32))),
        name="fft_t2",
    )(f2, w)
    return p.reshape(2, n2 * n1, m)


def _mix_ln_kernel(x_ref, ge_ref, be_ref, oh_ref, pr_ref, pi_ref, c3_ref, s3_ref, woh_ref, wof_ref,
                   g1_ref, b1_ref, h1_ref, h1b_ref, *, alpha):
    gd = FOURIER_GDIM
    c3, s3 = c3_ref[...], s3_ref[...]
    parts = []
    for g in range(pr_ref.shape[1] // gd):
        sl = slice(g * gd, (g + 1) * gd)
        parts.append((_dot(pr_ref[:, sl], c3) + _dot(pi_ref[:, sl], s3)).astype(BF16))
    o_four = jnp.concatenate(parts, axis=1)
    mix = _dot(oh_ref[...], woh_ref[...]) + _dot(o_four, wof_ref[...])
    h0 = _layernorm(x_ref[...], ge_ref[...], be_ref[...])
    h1 = _layernorm(alpha * h0 + mix, g1_ref[...], b1_ref[...])
    h1_ref[...] = h1
    h1b_ref[...] = h1.astype(BF16)


def _mix_ln(x, ge, be, o_hgrn, p, c3, s3, woh, wof, g1, b1, *, alpha, tm=256):
    t, d = x.shape
    wh = o_hgrn.shape[1]
    m = p.shape[2]
    gd = FOURIER_GDIM
    const = lambda shape: pl.BlockSpec(shape, lambda i: tuple(0 for _ in shape))
    pipelined = (_nbytes((tm, d), F32) + _nbytes((tm, wh + 2 * m), BF16) + _nbytes((wh + m, d), BF16)
                 + _nbytes((tm, d), F32) + _nbytes((tm, d), BF16))
    return pl.pallas_call(
        functools.partial(_mix_ln_kernel, alpha=alpha),
        grid=(t // tm,),
        in_specs=[
            pl.BlockSpec((tm, d), lambda i: (i, 0)),
            const((1, d)), const((1, d)),
            pl.BlockSpec((tm, wh), lambda i: (i, 0)),
            pl.BlockSpec((None, tm, m), lambda i: (0, i, 0)),
            pl.BlockSpec((None, tm, m), lambda i: (1, i, 0)),
            const((gd, gd)), const((gd, gd)),
            const((wh, d)), const((m, d)),
            const((1, d)), const((1, d)),
        ],
        out_specs=[pl.BlockSpec((tm, d), lambda i: (i, 0)), pl.BlockSpec((tm, d), lambda i: (i, 0))],
        out_shape=[jax.ShapeDtypeStruct((t, d), F32), jax.ShapeDtypeStruct((t, d), BF16)],
        compiler_params=pltpu.CompilerParams(
            dimension_semantics=("parallel",),
            vmem_limit_bytes=_vmem_limit(pipelined, 4 * _nbytes((tm, d), F32))),
        name="mix_ln",
    )(x, ge, be, o_hgrn, p, p, c3, s3, woh, wof, g1, b1)


def _ffn_ln_kernel(h1_ref, h1b_ref, wg_ref, wu_ref, wd_ref, g2_ref, b2_ref, o_ref, acc_ref, *, alpha):
    f = pl.program_id(1)
    hb = h1b_ref[...]
    gate = _dot(hb, wg_ref[...])
    up = _dot(hb, wu_ref[...])
    act = (gate * _sigmoid(gate) * up).astype(BF16)
    part = _dot(act, wd_ref[...])

    @pl.when(f == 0)
    def _():
        acc_ref[...] = part

    @pl.when(f > 0)
    def _():
        acc_ref[...] += part

    @pl.when(f == pl.num_programs(1) - 1)
    def _():
        o_ref[...] = _layernorm(alpha * h1_ref[...] + acc_ref[...], g2_ref[...], b2_ref[...])


def _ffn_ln(h1, h1b, wg, wu, wd, g2, b2, *, alpha, tm=512, tf=512):
    t, d = h1.shape
    dff = wg.shape[1]
    pipelined = (_nbytes((tm, d), F32) + _nbytes((tm, d), BF16) + 3 * _nbytes((d, tf), BF16)
                 + _nbytes((tm, d), F32))
    resident = _nbytes((tm, d), F32) + 3 * _nbytes((tm, tf), F32) + _nbytes((tm, d), F32)
    return pl.pallas_call(
        functools.partial(_ffn_ln_kernel, alpha=alpha),
        grid=(t // tm, dff // tf),
        in_specs=[
            pl.BlockSpec((tm, d), lambda i, f: (i, 0)),
            pl.BlockSpec((tm, d), lambda i, f: (i, 0)),
            pl.BlockSpec((d, tf), lambda i, f: (0, f)),
            pl.BlockSpec((d, tf), lambda i, f: (0, f)),
            pl.BlockSpec((tf, d), lambda i, f: (f, 0)),
            pl.BlockSpec((1, d), lambda i, f: (0, 0)),
            pl.BlockSpec((1, d), lambda i, f: (0, 0)),
        ],
        out_specs=pl.BlockSpec((tm, d), lambda i, f: (i, 0)),
        out_shape=jax.ShapeDtypeStruct((t, d), F32),
        scratch_shapes=[pltpu.VMEM((tm, d), F32)],
        compiler_params=pltpu.CompilerParams(
            dimension_semantics=("parallel", "arbitrary"),
            vmem_limit_bytes=_vmem_limit(pipelined, resident)),
        name="ffn_ln",
    )(h1, h1b, wg, wu, wd, g2, b2)


def kernel(x, ln_emb_g, ln_emb_b, w_in, lb_fwd_logits, lb_bwd_logits, g_norm, w_out, ln1_g, ln1_b,
           w_gate, w_up, w_down, ln2_g, ln2_b):
    depth = w_in.shape[0]
    assert depth == 1, "the embedding LayerNorm is fused into the single layer's projection"
    batch, seq, d = x.shape
    assert batch == 1
    w = lb_fwd_logits.shape[1]
    alpha = (2.0 * depth) ** 0.25
    layer = 0
    row = lambda a: a.reshape(1, -1).astype(F32)

    x2 = x.reshape(seq, d)
    wi = w_in[layer]
    wz = wi[:, 2 * w:4 * w].astype(BF16)
    wq = jnp.concatenate([wi[:, :2 * w], wi[:, 4 * w:5 * w]], axis=1).astype(BF16)
    wf = wi[:, 5 * w:].astype(BF16)
    z, qig, four = _ln_proj(x2, row(ln_emb_g), row(ln_emb_b), wz, wq, wf)

    o_hgrn = _hgrn(lb_fwd_logits.astype(F32), lb_bwd_logits.astype(F32), row(g_norm[layer]), qig, z,
                   layer=layer)

    g1c, f2c, c3, s3 = _dft_constants(seq, FFT_N1)
    p = _fft_t2(_fft_t1(four, g1c), f2c)

    wo = w_out[layer].astype(BF16)
    h1, h1b = _mix_ln(x2, row(ln_emb_g), row(ln_emb_b), o_hgrn, p, c3, s3, wo[:w], wo[w:],
                      row(ln1_g[layer]), row(ln1_b[layer]), alpha=alpha)

    out = _ffn_ln(h1, h1b, w_gate[layer].astype(BF16), w_up[layer].astype(BF16),
                  w_down[layer].astype(BF16), row(ln2_g[layer]), row(ln2_b[layer]), alpha=alpha)
    return out.reshape(batch, seq, d)
```

```python
import functools
import math

import numpy as np
import jax
import jax.numpy as jnp
from jax import lax
from jax.experimental import pallas as pl
from jax.experimental.pallas import tpu as pltpu

LN_EPS = 1e-5
RMS_EPS = 1e-6
HEAD_DIM = 128
FOURIER_GDIM = 256
FFT_N1 = 64
HGRN_CHUNK = 128
V7X_VMEM_CAP = 56 * 1024 * 1024

BF16 = jnp.bfloat16
F32 = jnp.float32


def _vmem_limit(pipelined_bytes, resident_bytes):
    return int(min(V7X_VMEM_CAP, 2 * pipelined_bytes + resident_bytes + (4 << 20)))


def _nbytes(shape, dtype):
    return int(np.prod(shape)) * jnp.dtype(dtype).itemsize


def _dot(a, b):
    return jnp.dot(a, b, preferred_element_type=F32)


def _layernorm(x, g, b):
    mu = jnp.mean(x, axis=-1, keepdims=True)
    xc = x - mu
    var = jnp.mean(xc * xc, axis=-1, keepdims=True)
    return xc * lax.rsqrt(var + LN_EPS) * g + b


def _sigmoid(x):
    return 1.0 / (1.0 + jnp.exp(-x))


def _ln_proj_kernel(x_ref, g_ref, b_ref, wz_ref, wq_ref, wf_ref, z_ref, q_ref, f_ref, h_scr):
    @pl.when(pl.program_id(1) == 0)
    def _():
        h_scr[...] = _layernorm(x_ref[...], g_ref[...], b_ref[...]).astype(BF16)

    h = h_scr[...]
    z_ref[...] = _dot(h, wz_ref[...])
    q_ref[...] = _dot(h, wq_ref[...]).astype(BF16)
    f_ref[...] = _dot(h, wf_ref[...]).astype(BF16)


def _ln_proj(x, g, b, wz, wq, wf, *, tm=512, nj=4):
    t, d = x.shape
    nz, nq, nf = wz.shape[1] // nj, wq.shape[1] // nj, wf.shape[1] // nj
    pipelined = (_nbytes((tm, d), F32) + _nbytes((d, nz + nq + nf), BF16)
                 + _nbytes((tm, nz), F32) + _nbytes((tm, nq + nf), BF16))
    resident = _nbytes((tm, d), BF16) + 2 * _nbytes((tm, nz + nq + nf), F32)
    return pl.pallas_call(
        _ln_proj_kernel,
        grid=(t // tm, nj),
        in_specs=[
            pl.BlockSpec((tm, d), lambda i, j: (i, 0)),
            pl.BlockSpec((1, d), lambda i, j: (0, 0)),
            pl.BlockSpec((1, d), lambda i, j: (0, 0)),
            pl.BlockSpec((d, nz), lambda i, j: (0, j)),
            pl.BlockSpec((d, nq), lambda i, j: (0, j)),
            pl.BlockSpec((d, nf), lambda i, j: (0, j)),
        ],
        out_specs=[
            pl.BlockSpec((tm, nz), lambda i, j: (i, j)),
            pl.BlockSpec((tm, nq), lambda i, j: (i, j)),
            pl.BlockSpec((tm, nf), lambda i, j: (i, j)),
        ],
        out_shape=[
            jax.ShapeDtypeStruct((t, wz.shape[1]), F32),
            jax.ShapeDtypeStruct((t, wq.shape[1]), BF16),
            jax.ShapeDtypeStruct((t, wf.shape[1]), BF16),
        ],
        scratch_shapes=[pltpu.VMEM((tm, d), BF16)],
        compiler_params=pltpu.CompilerParams(
            dimension_semantics=("parallel", "arbitrary"),
            vmem_limit_bytes=_vmem_limit(pipelined, resident)),
        name="ln_proj",
    )(x, g, b, wz, wq, wf)


def _hgrn_kernel(lbf_ref, lbb_ref, gn_ref, q_ref, v_ref, g_ref, z_ref, o_ref, st_ref, of_ref,
                 *, layer, nblk, blk_rows):
    c = HGRN_CHUNK
    half = c // 2
    nchunk = blk_rows // c
    phase = pl.program_id(1)
    t = pl.program_id(2)

    @pl.when(t == 0)
    def _():
        st_ref[...] = jnp.zeros_like(st_ref)

    def scan_block(fwd):
        logits = (lbf_ref if fwd else lbb_ref)[...]
        e = jnp.exp(logits - jnp.max(logits, axis=0, keepdims=True))
        lb = jnp.sum(e[:layer + 1], axis=0, keepdims=True) / jnp.sum(e, axis=0, keepdims=True)

        row = lax.broadcasted_iota(jnp.int32, (c, c), 0)
        col = lax.broadcasted_iota(jnp.int32, (c, c), 1)
        tri = (col <= row) if fwd else (col >= row)

        tri3 = jnp.concatenate([tri.astype(BF16)] * 3, axis=1)
        blk = t if fwd else nblk - 1 - t
        chunk_rows = [pl.ds((ci if fwd else nchunk - 1 - ci) * c, c) for ci in range(nchunk)]
        trans_b = (((1,), (1,)), ((), ()))
        trans_a = (((0,), (0,)), ((), ()))

        kks, bs = [], []
        for rows in chunk_rows:
            f = lb + (1.0 - lb) * _sigmoid(z_ref[rows, :])
            logf = jnp.log(f)
            kks.append(1.0 - f)
            hi = logf.astype(BF16)
            r1 = logf - hi.astype(F32)
            mid = r1.astype(BF16)
            lo = (r1 - mid.astype(F32)).astype(BF16)
            bs.append(_dot(tri3, jnp.concatenate([hi, mid, lo], axis=0)))

        qds, dsts, scs, decs, mids = [], [], [], [], []
        for rows, kk, b in zip(chunk_rows, kks, bs):
            b_mid = b[half - 1:half, :] if fwd else b[half:half + 1, :]
            b_end = b[c - 1:c, :] if fwd else b[0:1, :]
            qf = q_ref[rows, :].astype(F32)
            qd = (qf * _sigmoid(qf) * jnp.exp(b - b_mid)).astype(BF16)
            ki = (kk * jnp.exp(b_mid - b)).astype(BF16)
            ke = (kk * jnp.exp(b_end - b)).astype(BF16)
            qds.append(qd)
            scs.append(lax.dot_general(qd, ki, trans_b, preferred_element_type=F32))
            dsts.append(lax.dot_general(v_ref[rows, :], ke, trans_a, preferred_element_type=F32))
            decs.append(jnp.exp(b_end))
            mids.append(jnp.exp(b_mid))

        st = st_ref[...]
        s_mids = []
        for dec, emid, dst in zip(decs, mids, dsts):
            s_mids.append((st * emid).astype(BF16))
            st = dec * st + dst
        st_ref[...] = st

        for rows, qd, sc, s_mid in zip(chunk_rows, qds, scs, s_mids):
            scores = jnp.where(tri, sc, 0.0).astype(BF16)
            o = (_dot(scores, v_ref[rows, :])
                 + lax.dot_general(qd, s_mid, trans_b, preferred_element_type=F32))
            grow = pl.ds(pl.multiple_of(blk * blk_rows, c) + rows.start, c)
            if fwd:
                of_ref[grow, :] = o
            else:
                ot = o + of_ref[grow, :]
                ot = ot * lax.rsqrt(jnp.mean(ot * ot, axis=-1, keepdims=True) + RMS_EPS) * gn_ref[...]
                gf = g_ref[rows, :].astype(F32)
                o_ref[rows, :] = (ot * (gf * _sigmoid(gf))).astype(BF16)

    pl.when(phase == 0)(functools.partial(scan_block, True))
    pl.when(phase == 1)(functools.partial(scan_block, False))


def _hgrn(lbf_logits, lbb_logits, g_norm, qig, z, *, layer, blk_rows=1024):
    t = z.shape[0]
    w = z.shape[1] // 2
    heads = w // HEAD_DIM
    nblk = t // blk_rows
    d = HEAD_DIM
    slots = lbf_logits.shape[0]

    def tblk(h, p, i):
        return i + p * (nblk - 1 - 2 * i)

    pipelined = 4 * _nbytes((blk_rows, d), BF16) + _nbytes((blk_rows, d), F32)
    resident = _nbytes((t, d), F32) + 64 * _nbytes((d, d), F32)
    return pl.pallas_call(
        functools.partial(_hgrn_kernel, layer=layer, nblk=nblk, blk_rows=blk_rows),
        grid=(heads, 2, nblk),
        in_specs=[
            pl.BlockSpec((slots, d), lambda h, p, i: (0, h)),
            pl.BlockSpec((slots, d), lambda h, p, i: (0, h)),
            pl.BlockSpec((1, d), lambda h, p, i: (0, 0)),
            pl.BlockSpec((blk_rows, d), lambda h, p, i: (tblk(h, p, i), h)),
            pl.BlockSpec((blk_rows, d), lambda h, p, i: (tblk(h, p, i), heads + h)),
            pl.BlockSpec((blk_rows, d), lambda h, p, i: (tblk(h, p, i), 2 * heads + h)),
            pl.BlockSpec((blk_rows, d), lambda h, p, i: (tblk(h, p, i), p * heads + h)),
        ],
        out_specs=pl.BlockSpec((blk_rows, d), lambda h, p, i: (nblk - 1 - p * i, h)),
        out_shape=jax.ShapeDtypeStruct((t, w), BF16),
        scratch_shapes=[pltpu.VMEM((d, d), F32), pltpu.VMEM((t, d), F32)],
        compiler_params=pltpu.CompilerParams(
            dimension_semantics=("parallel", "arbitrary", "arbitrary"),
            vmem_limit_bytes=_vmem_limit(pipelined, resident)),
        name="hgrn",
    )(lbf_logits, lbb_logits, g_norm, qig, qig, qig, z)


def _dft_constants(t, n1):
    n2 = t // n1
    k1 = np.arange(n1)[None, :, None]
    t1 = np.arange(n1)[None, None, :]
    t2 = np.arange(n2)[:, None, None]
    ang = 2.0 * np.pi * ((k1 * (n2 * t1 + t2)) % t) / t
    g = np.concatenate([np.cos(ang), -np.sin(ang)], axis=1) / math.sqrt(n1)
    k2 = np.arange(n2)[:, None]
    s2 = np.arange(n2)[None, :]
    ang2 = 2.0 * np.pi * ((k2 * s2) % n2) / n2
    c2, sn2 = np.cos(ang2) / math.sqrt(n2), np.sin(ang2) / math.sqrt(n2)
    f2 = np.block([[c2, sn2], [-sn2, c2]])
    m = np.arange(FOURIER_GDIM)
    ang3 = 2.0 * np.pi * ((m[:, None] * m[None, :]) % FOURIER_GDIM) / FOURIER_GDIM
    c3 = np.cos(ang3) / math.sqrt(FOURIER_GDIM)
    s3 = np.sin(ang3) / math.sqrt(FOURIER_GDIM)
    as_bf16 = lambda a: jnp.asarray(a, dtype=F32).astype(BF16)
    return as_bf16(g), as_bf16(f2), as_bf16(c3), as_bf16(s3)


def _fft_t1_kernel(u_ref, g_ref, w_ref, *, tb, n1, m):
    for j in range(tb):
        r = _dot(g_ref[j], u_ref[:, j * m:(j + 1) * m])
        w_ref[0, j] = r[:n1].astype(BF16)
        w_ref[1, j] = r[n1:].astype(BF16)


def _fft_t1(four, g, *, tb=8):
    t, m = four.shape
    n2, _, n1 = g.shape
    u = four.reshape(n1, n2 * m)
    pipelined = (_nbytes((n1, tb * m), BF16) + _nbytes((tb, 2 * n1, n1), BF16)
                 + _nbytes((2, tb, n1, m), BF16))
    return pl.pallas_call(
        functools.partial(_fft_t1_kernel, tb=tb, n1=n1, m=m),
        grid=(n2 // tb,),
        in_specs=[
            pl.BlockSpec((n1, tb * m), lambda s: (0, s)),
            pl.BlockSpec((tb, 2 * n1, n1), lambda s: (s, 0, 0)),
        ],
        out_specs=pl.BlockSpec((2, tb, n1, m), lambda s: (0, s, 0, 0)),
        out_shape=jax.ShapeDtypeStruct((2, n2, n1, m), BF16),
        compiler_params=pltpu.CompilerParams(
            dimension_semantics=("parallel",),
            vmem_limit_bytes=_vmem_limit(pipelined, 2 * _nbytes((2 * n1, m), F32))),
        name="fft_t1",
    )(u, g)


def _fft_t2_kernel(f_ref, w_ref, p_ref):
    p_ref[...] = _dot(f_ref[...], w_ref[...]).astype(BF16)


def _fft_t2(w4, f2, *, cb=2048):
    _, n2, n1, m = w4.shape
    w = w4.reshape(2 * n2, n1 * m)
    pipelined = 2 * _nbytes((2 * n2, cb), BF16) + _nbytes((2 * n2, 2 * n2), BF16)
    p = pl.pallas_call(
        _fft_t2_kernel,
        grid=(n1 * m // cb,),
        in_specs=[
            pl.BlockSpec((2 * n2, 2 * n2), lambda s: (0, 0)),
            pl.BlockSpec((2 * n2, cb), lambda s: (0, s)),
        ],
        out_specs=pl.BlockSpec((2 * n2, cb), lambda s: (0, s)),
        out_shape=jax.ShapeDtypeStruct((2 * n2, n1 * m), BF16),
        compiler_params=pltpu.CompilerParams(
            dimension_semantics=("parallel",),
            vmem_limit_bytes=_vmem_limit(pipelined, _nbytes((2 * n2, cb), F32))),
        name="fft_t2",
    )(f2, w)
    return p.reshape(2, n2 * n1, m)


def _mix_ln_kernel(x_ref, ge_ref, be_ref, oh_ref, pr_ref, pi_ref, c3_ref, s3_ref, woh_ref, wof_ref,
                   g1_ref, b1_ref, h1_ref, h1b_ref, *, alpha):
    gd = FOURIER_GDIM
    c3, s3 = c3_ref[...], s3_ref[...]
    parts = []
    for g in range(pr_ref.shape[1] // gd):
        sl = slice(g * gd, (g + 1) * gd)
        parts.append((_dot(pr_ref[:, sl], c3) + _dot(pi_ref[:, sl], s3)).astype(BF16))
    o_four = jnp.concatenate(parts, axis=1)
    mix = _dot(oh_ref[...], woh_ref[...]) + _dot(o_four, wof_ref[...])
    h0 = _layernorm(x_ref[...], ge_ref[...], be_ref[...])
    h1 = _layernorm(alpha * h0 + mix, g1_ref[...], b1_ref[...])
    h1_ref[...] = h1
    h1b_ref[...] = h1.astype(BF16)


def _mix_ln(x, ge, be, o_hgrn, p, c3, s3, woh, wof, g1, b1, *, alpha, tm=256):
    t, d = x.shape
    wh = o_hgrn.shape[1]
    m = p.shape[2]
    gd = FOURIER_GDIM
    const = lambda shape: pl.BlockSpec(shape, lambda i: tuple(0 for _ in shape))
    pipelined = (_nbytes((tm, d), F32) + _nbytes((tm, wh + 2 * m), BF16) + _nbytes((wh + m, d), BF16)
                 + _nbytes((tm, d), F32) + _nbytes((tm, d), BF16))
    return pl.pallas_call(
        functools.partial(_mix_ln_kernel, alpha=alpha),
        grid=(t // tm,),
        in_specs=[
            pl.BlockSpec((tm, d), lambda i: (i, 0)),
            const((1, d)), const((1, d)),
            pl.BlockSpec((tm, wh), lambda i: (i, 0)),
            pl.BlockSpec((None, tm, m), lambda i: (0, i, 0)),
            pl.BlockSpec((None, tm, m), lambda i: (1, i, 0)),
            const((gd, gd)), const((gd, gd)),
            const((wh, d)), const((m, d)),
            const((1, d)), const((1, d)),
        ],
        out_specs=[pl.BlockSpec((tm, d), lambda i: (i, 0)), pl.BlockSpec((tm, d), lambda i: (i, 0))],
        out_shape=[jax.ShapeDtypeStruct((t, d), F32), jax.ShapeDtypeStruct((t, d), BF16)],
        compiler_params=pltpu.CompilerParams(
            dimension_semantics=("parallel",),
            vmem_limit_bytes=_vmem_limit(pipelined, 4 * _nbytes((tm, d), F32))),
        name="mix_ln",
    )(x, ge, be, o_hgrn, p, p, c3, s3, woh, wof, g1, b1)


def _ffn_ln_kernel(h1_ref, h1b_ref, wg_ref, wu_ref, wd_ref, g2_ref, b2_ref, o_ref, acc_ref, *, alpha):
    f = pl.program_id(1)

    @pl.when(f == 0)
    def _():
        acc_ref[...] = jnp.zeros_like(acc_ref)

    hb = h1b_ref[...]
    gate = _dot(hb, wg_ref[...])
    up = _dot(hb, wu_ref[...])
    act = (gate * _sigmoid(gate) * up).astype(BF16)
    acc_ref[...] += _dot(act, wd_ref[...])

    @pl.when(f == pl.num_programs(1) - 1)
    def _():
        o_ref[...] = _layernorm(alpha * h1_ref[...] + acc_ref[...], g2_ref[...], b2_ref[...])


def _ffn_ln(h1, h1b, wg, wu, wd, g2, b2, *, alpha, tm=512, tf=512):
    t, d = h1.shape
    dff = wg.shape[1]
    pipelined = (_nbytes((tm, d), F32) + _nbytes((tm, d), BF16) + 3 * _nbytes((d, tf), BF16)
                 + _nbytes((tm, d), F32))
    resident = _nbytes((tm, d), F32) + 3 * _nbytes((tm, tf), F32) + _nbytes((tm, d), F32)
    return pl.pallas_call(
        functools.partial(_ffn_ln_kernel, alpha=alpha),
        grid=(t // tm, dff // tf),
        in_specs=[
            pl.BlockSpec((tm, d), lambda i, f: (i, 0)),
            pl.BlockSpec((tm, d), lambda i, f: (i, 0)),
            pl.BlockSpec((d, tf), lambda i, f: (0, f)),
            pl.BlockSpec((d, tf), lambda i, f: (0, f)),
            pl.BlockSpec((tf, d), lambda i, f: (f, 0)),
            pl.BlockSpec((1, d), lambda i, f: (0, 0)),
            pl.BlockSpec((1, d), lambda i, f: (0, 0)),
        ],
        out_specs=pl.BlockSpec((tm, d), lambda i, f: (i, 0)),
        out_shape=jax.ShapeDtypeStruct((t, d), F32),
        scratch_shapes=[pltpu.VMEM((tm, d), F32)],
        compiler_params=pltpu.CompilerParams(
            dimension_semantics=("parallel", "arbitrary"),
            vmem_limit_bytes=_vmem_limit(pipelined, resident)),
        name="ffn_ln",
    )(h1, h1b, wg, wu, wd, g2, b2)


def kernel(x, ln_emb_g, ln_emb_b, w_in, lb_fwd_logits, lb_bwd_logits, g_norm, w_out, ln1_g, ln1_b,
           w_gate, w_up, w_down, ln2_g, ln2_b):
    depth = w_in.shape[0]
    assert depth == 1, "the embedding LayerNorm is fused into the single layer's projection"
    batch, seq, d = x.shape
    assert batch == 1
    w = lb_fwd_logits.shape[1]
    alpha = (2.0 * depth) ** 0.25
    layer = 0
    row = lambda a: a.reshape(1, -1).astype(F32)

    x2 = x.reshape(seq, d)
    wi = w_in[layer]
    wz = wi[:, 2 * w:4 * w].astype(BF16)
    wq = jnp.concatenate([wi[:, :2 * w], wi[:, 4 * w:5 * w]], axis=1).astype(BF16)
    wf = wi[:, 5 * w:].astype(BF16)
    z, qig, four = _ln_proj(x2, row(ln_emb_g), row(ln_emb_b), wz, wq, wf)

    o_hgrn = _hgrn(lb_fwd_logits.astype(F32), lb_bwd_logits.astype(F32), row(g_norm[layer]), qig, z,
                   layer=layer)

    g1c, f2c, c3, s3 = _dft_constants(seq, FFT_N1)
    p = _fft_t2(_fft_t1(four, g1c), f2c)

    wo = w_out[layer].astype(BF16)
    h1, h1b = _mix_ln(x2, row(ln_emb_g), row(ln_emb_b), o_hgrn, p, c3, s3, wo[:w], wo[w:],
                      row(ln1_g[layer]), row(ln1_b[layer]), alpha=alpha)

    out = _ffn_ln(h1, h1b, w_gate[layer].astype(BF16), w_up[layer].astype(BF16),
                  w_down[layer].astype(BF16), row(ln2_g[layer]), row(ln2_b[layer]), alpha=alpha)
    return out.reshape(batch, seq, d)
```

```python
import functools
import math

import numpy as np
import jax
import jax.numpy as jnp
from jax import lax
from jax.experimental import pallas as pl
from jax.experimental.pallas import tpu as pltpu

LN_EPS = 1e-5
RMS_EPS = 1e-6
HEAD_DIM = 128
FOURIER_GDIM = 256
FFT_N1 = 64
HGRN_CHUNK = 128
LN_PROJ_STEPS = 4
V7X_VMEM_CAP = 56 * 1024 * 1024

BF16 = jnp.bfloat16
F32 = jnp.float32


def _vmem_limit(pipelined_bytes, resident_bytes):
    return int(min(V7X_VMEM_CAP, 2 * pipelined_bytes + resident_bytes + (4 << 20)))


def _nbytes(shape, dtype):
    return int(np.prod(shape)) * jnp.dtype(dtype).itemsize


def _dot(a, b):
    return jnp.dot(a, b, preferred_element_type=F32)


def _layernorm(x, g, b):
    mu = jnp.mean(x, axis=-1, keepdims=True)
    xc = x - mu
    var = jnp.mean(xc * xc, axis=-1, keepdims=True)
    return xc * lax.rsqrt(var + LN_EPS) * g + b


def _sigmoid(x):
    return 1.0 / (1.0 + jnp.exp(-x))


def _ln_proj_kernel(x_ref, g_ref, b_ref, w_ref, z_ref, q_ref, f_ref, h_scr):
    @pl.when(pl.program_id(1) == 0)
    def _():
        h_scr[...] = _layernorm(x_ref[...], g_ref[...], b_ref[...]).astype(BF16)

    nz, nq = z_ref.shape[1], q_ref.shape[1]
    u = _dot(h_scr[...], w_ref[...])
    z_ref[...] = u[:, :nz]
    q_ref[...] = u[:, nz:nz + nq].astype(BF16)
    f_ref[...] = u[:, nz + nq:]


def _ln_proj(x, g, b, wcat, widths, *, tm=1024):
    t, d = x.shape
    nj, _, n = wcat.shape
    nz, nq, nf = widths
    pipelined = (_nbytes((tm, d), F32) + _nbytes((d, n), BF16)
                 + _nbytes((tm, nz + nf), F32) + _nbytes((tm, nq), BF16))
    resident = _nbytes((tm, d), BF16) + _nbytes((tm, n), F32)
    return pl.pallas_call(
        _ln_proj_kernel,
        grid=(t // tm, nj),
        in_specs=[
            pl.BlockSpec((tm, d), lambda i, j: (i, 0)),
            pl.BlockSpec((1, d), lambda i, j: (0, 0)),
            pl.BlockSpec((1, d), lambda i, j: (0, 0)),
            pl.BlockSpec((None, d, n), lambda i, j: (j, 0, 0)),
        ],
        out_specs=[
            pl.BlockSpec((tm, nz), lambda i, j: (i, j)),
            pl.BlockSpec((tm, nq), lambda i, j: (i, j)),
            pl.BlockSpec((tm, nf), lambda i, j: (i, j)),
        ],
        out_shape=[
            jax.ShapeDtypeStruct((t, nj * nz), F32),
            jax.ShapeDtypeStruct((t, nj * nq), BF16),
            jax.ShapeDtypeStruct((t, nj * nf), F32),
        ],
        scratch_shapes=[pltpu.VMEM((tm, d), BF16)],
        compiler_params=pltpu.CompilerParams(
            dimension_semantics=("parallel", "arbitrary"),
            vmem_limit_bytes=_vmem_limit(pipelined, resident)),
        name="ln_proj",
    )(x, g, b, wcat)


def _hgrn_kernel(lbf_ref, lbb_ref, gn_ref, q_ref, v_ref, g_ref, z_ref, o_ref, st_ref, of_ref,
                 *, layer, nblk, blk_rows):
    c = HGRN_CHUNK
    half = c // 2
    nchunk = blk_rows // c
    phase = pl.program_id(1)
    t = pl.program_id(2)

    @pl.when(t == 0)
    def _():
        st_ref[...] = jnp.zeros_like(st_ref)

    def scan_block(fwd):
        logits = (lbf_ref if fwd else lbb_ref)[...]
        e = jnp.exp(logits - jnp.max(logits, axis=0, keepdims=True))
        lb = jnp.sum(e[:layer + 1], axis=0, keepdims=True) / jnp.sum(e, axis=0, keepdims=True)

        row = lax.broadcasted_iota(jnp.int32, (c, c), 0)
        col = lax.broadcasted_iota(jnp.int32, (c, c), 1)
        tri = (col <= row) if fwd else (col >= row)

        tri3 = jnp.concatenate([tri.astype(BF16)] * 3, axis=1)
        blk = t if fwd else nblk - 1 - t
        chunk_rows = [pl.ds((ci if fwd else nchunk - 1 - ci) * c, c) for ci in range(nchunk)]
        trans_b = (((1,), (1,)), ((), ()))
        trans_a = (((0,), (0,)), ((), ()))

        kks, bs = [], []
        for rows in chunk_rows:
            f = lb + (1.0 - lb) * _sigmoid(z_ref[rows, :])
            logf = jnp.log(f)
            kks.append(1.0 - f)
            hi = logf.astype(BF16)
            r1 = logf - hi.astype(F32)
            mid = r1.astype(BF16)
            lo = (r1 - mid.astype(F32)).astype(BF16)
            bs.append(_dot(tri3, jnp.concatenate([hi, mid, lo], axis=0)))

        qds, dsts, scs, decs, mids = [], [], [], [], []
        for rows, kk, b in zip(chunk_rows, kks, bs):
            b_mid = b[half - 1:half, :] if fwd else b[half:half + 1, :]
            b_end = b[c - 1:c, :] if fwd else b[0:1, :]
            qf = q_ref[rows, :].astype(F32)
            qd = (qf * _sigmoid(qf) * jnp.exp(b - b_mid)).astype(BF16)
            ki = (kk * jnp.exp(b_mid - b)).astype(BF16)
            ke = (kk * jnp.exp(b_end - b)).astype(BF16)
            qds.append(qd)
            scs.append(lax.dot_general(qd, ki, trans_b, preferred_element_type=F32))
            dsts.append(lax.dot_general(v_ref[rows, :], ke, trans_a, preferred_element_type=F32))
            decs.append(jnp.exp(b_end))
            mids.append(jnp.exp(b_mid))

        st = st_ref[...]
        s_mids = []
        for dec, emid, dst in zip(decs, mids, dsts):
            s_mids.append((st * emid).astype(BF16))
            st = dec * st + dst
        st_ref[...] = st

        for rows, qd, sc, s_mid in zip(chunk_rows, qds, scs, s_mids):
            scores = jnp.where(tri, sc, 0.0).astype(BF16)
            o = (_dot(scores, v_ref[rows, :])
                 + lax.dot_general(qd, s_mid, trans_b, preferred_element_type=F32))
            grow = pl.ds(pl.multiple_of(blk * blk_rows, c) + rows.start, c)
            if fwd:
                of_ref[grow, :] = o
            else:
                ot = o + of_ref[grow, :]
                ot = ot * lax.rsqrt(jnp.mean(ot * ot, axis=-1, keepdims=True) + RMS_EPS) * gn_ref[...]
                gf = g_ref[rows, :].astype(F32)
                o_ref[rows, :] = (ot * (gf * _sigmoid(gf))).astype(BF16)

    pl.when(phase == 0)(functools.partial(scan_block, True))
    pl.when(phase == 1)(functools.partial(scan_block, False))


def _hgrn(lbf_logits, lbb_logits, g_norm, qig, z, *, layer, blk_rows=1024):
    t = z.shape[0]
    w = z.shape[1] // 2
    heads = w // HEAD_DIM
    nblk = t // blk_rows
    d = HEAD_DIM
    slots = lbf_logits.shape[0]

    def tblk(h, p, i):
        return i + p * (nblk - 1 - 2 * i)

    hps = heads // LN_PROJ_STEPS
    zcol = lambda h, p: (h // hps) * 2 * hps + p * hps + h % hps
    qcol = lambda h, sec: (h // hps) * 3 * hps + sec * hps + h % hps

    pipelined = 4 * _nbytes((blk_rows, d), BF16) + _nbytes((blk_rows, d), F32)
    resident = _nbytes((t, d), F32) + 64 * _nbytes((d, d), F32)
    return pl.pallas_call(
        functools.partial(_hgrn_kernel, layer=layer, nblk=nblk, blk_rows=blk_rows),
        grid=(heads, 2, nblk),
        in_specs=[
            pl.BlockSpec((slots, d), lambda h, p, i: (0, h)),
            pl.BlockSpec((slots, d), lambda h, p, i: (0, h)),
            pl.BlockSpec((1, d), lambda h, p, i: (0, 0)),
            pl.BlockSpec((blk_rows, d), lambda h, p, i: (tblk(h, p, i), qcol(h, 0))),
            pl.BlockSpec((blk_rows, d), lambda h, p, i: (tblk(h, p, i), qcol(h, 1))),
            pl.BlockSpec((blk_rows, d), lambda h, p, i: (tblk(h, p, i), qcol(h, 2))),
            pl.BlockSpec((blk_rows, d), lambda h, p, i: (tblk(h, p, i), zcol(h, p))),
        ],
        out_specs=pl.BlockSpec((blk_rows, d), lambda h, p, i: (nblk - 1 - p * i, h)),
        out_shape=jax.ShapeDtypeStruct((t, w), BF16),
        scratch_shapes=[pltpu.VMEM((d, d), F32), pltpu.VMEM((t, d), F32)],
        compiler_params=pltpu.CompilerParams(
            dimension_semantics=("parallel", "arbitrary", "arbitrary"),
            vmem_limit_bytes=_vmem_limit(pipelined, resident)),
        name="hgrn",
    )(lbf_logits, lbb_logits, g_norm, qig, qig, qig, z)


def _dft_constants(t, n1):
    n2 = t // n1
    k1 = np.arange(n1)[None, :, None]
    t1 = np.arange(n1)[None, None, :]
    t2 = np.arange(n2)[:, None, None]
    ang = 2.0 * np.pi * ((k1 * (n2 * t1 + t2)) % t) / t
    g = np.concatenate([np.cos(ang), -np.sin(ang)], axis=1) / math.sqrt(n1)
    k2 = np.arange(n2)[:, None]
    s2 = np.arange(n2)[None, :]
    ang2 = 2.0 * np.pi * ((k2 * s2) % n2) / n2
    c2, sn2 = np.cos(ang2) / math.sqrt(n2), np.sin(ang2) / math.sqrt(n2)
    f2 = np.block([[c2, sn2], [-sn2, c2]])
    m = np.arange(FOURIER_GDIM)
    ang3 = 2.0 * np.pi * ((m[:, None] * m[None, :]) % FOURIER_GDIM) / FOURIER_GDIM
    c3 = np.cos(ang3) / math.sqrt(FOURIER_GDIM)
    s3 = np.sin(ang3) / math.sqrt(FOURIER_GDIM)
    as_bf16 = lambda a: jnp.asarray(a, dtype=F32).astype(BF16)
    return as_bf16(g), as_bf16(f2), as_bf16(c3), as_bf16(s3)


def _fft_t1_kernel(u_ref, g_ref, w_ref, *, tb, n1):
    x = pltpu.einshape("tjm->(jt)m", u_ref[...]).astype(BF16)
    r = jnp.stack([_dot(g_ref[j], x[j * n1:(j + 1) * n1]) for j in range(tb)])
    r = pltpu.einshape("jam->ajm", r)
    w_ref[0] = r[:n1]
    w_ref[1] = r[n1:]


def _fft_t1(four, g, *, tb=8):
    t, m = four.shape
    n2, _, n1 = g.shape
    u = four.reshape(n1, n2, m)
    pipelined = (_nbytes((n1, tb, m), F32) + _nbytes((tb, 2 * n1, n1), BF16)
                 + _nbytes((2, n1, tb, m), F32))
    return pl.pallas_call(
        functools.partial(_fft_t1_kernel, tb=tb, n1=n1),
        grid=(n2 // tb,),
        in_specs=[
            pl.BlockSpec((n1, tb, m), lambda s: (0, s, 0)),
            pl.BlockSpec((tb, 2 * n1, n1), lambda s: (s, 0, 0)),
        ],
        out_specs=pl.BlockSpec((2, n1, tb, m), lambda s: (0, 0, s, 0)),
        out_shape=jax.ShapeDtypeStruct((2, n1, n2, m), F32),
        compiler_params=pltpu.CompilerParams(
            dimension_semantics=("parallel",),
            vmem_limit_bytes=_vmem_limit(pipelined, 4 * _nbytes((2 * n1, m), F32))),
        name="fft_t1",
    )(u, g)


def _fft_t2_kernel(f_ref, w_ref, p_ref, *, kb, n2):
    for j in range(kb):
        rhs = jnp.concatenate([w_ref[0, j], w_ref[1, j]], axis=0).astype(BF16)
        res = _dot(f_ref[...], rhs)
        p_ref[0, j] = res[:n2]
        p_ref[1, j] = res[n2:]


def _fft_t2(w4, f2, *, kb=4):
    _, n1, n2, m = w4.shape
    blk = (2, kb, n2, m)
    pipelined = 2 * _nbytes(blk, F32) + _nbytes((2 * n2, 2 * n2), BF16)
    return pl.pallas_call(
        functools.partial(_fft_t2_kernel, kb=kb, n2=n2),
        grid=(n1 // kb,),
        in_specs=[
            pl.BlockSpec((2 * n2, 2 * n2), lambda s: (0, 0)),
            pl.BlockSpec(blk, lambda s: (0, s, 0, 0)),
        ],
        out_specs=pl.BlockSpec(blk, lambda s: (0, s, 0, 0)),
        out_shape=jax.ShapeDtypeStruct((2, n1, n2, m), F32),
        compiler_params=pltpu.CompilerParams(
            dimension_semantics=("parallel",),
            vmem_limit_bytes=_vmem_limit(pipelined, 4 * _nbytes((2 * n2, m), F32))),
        name="fft_t2",
    )(f2, w4)


def _mix_ln_kernel(x_ref, ge_ref, be_ref, oh_ref, pr_ref, pi_ref, c3_ref, s3_ref, woh_ref, wof_ref,
                   g1_ref, b1_ref, h1_ref, h1b_ref, pr_scr, pi_scr, *, alpha):
    gd = FOURIER_GDIM
    n1 = pr_ref.shape[0]
    pr_scr[...] = pltpu.einshape("akm->(ka)m", pr_ref[...]).astype(BF16)
    pi_scr[...] = pltpu.einshape("akm->(ka)m", pi_ref[...]).astype(BF16)
    c3, s3 = c3_ref[...], s3_ref[...]
    parts = []
    for g in range(pr_scr.shape[1] // gd):
        sl = slice(g * gd, (g + 1) * gd)
        parts.append((_dot(pr_scr[:, sl], c3) + _dot(pi_scr[:, sl], s3)).astype(BF16))
    o_four = jnp.concatenate(parts, axis=1)
    mix = _dot(oh_ref[...], woh_ref[...]) + _dot(o_four, wof_ref[...])
    h0 = _layernorm(x_ref[...], ge_ref[...], be_ref[...])
    h1 = _layernorm(alpha * h0 + mix, g1_ref[...], b1_ref[...])
    h1_ref[...] = h1
    h1b_ref[...] = h1.astype(BF16)


def _mix_ln(x, ge, be, o_hgrn, p, c3, s3, woh, wof, g1, b1, *, alpha, tm=512):
    t, d = x.shape
    wh = o_hgrn.shape[1]
    _, n1, n2, m = p.shape
    kr = tm // n1
    gd = FOURIER_GDIM
    const = lambda shape: pl.BlockSpec(shape, lambda i: tuple(0 for _ in shape),
                                       pipeline_mode=pl.Buffered(1))
    pipelined = (_nbytes((tm, d), F32) + _nbytes((tm, wh), BF16) + 2 * _nbytes((n1, kr, m), F32)
                 + _nbytes((tm, d), F32) + _nbytes((tm, d), BF16))
    resident = _nbytes((wh + m, d), BF16) + 2 * _nbytes((tm, m), BF16) + 3 * _nbytes((tm, d), F32)
    return pl.pallas_call(
        functools.partial(_mix_ln_kernel, alpha=alpha),
        grid=(t // tm,),
        in_specs=[
            pl.BlockSpec((tm, d), lambda i: (i, 0)),
            const((1, d)), const((1, d)),
            pl.BlockSpec((tm, wh), lambda i: (i, 0)),
            pl.BlockSpec((None, n1, kr, m), lambda i: (0, 0, i, 0)),
            pl.BlockSpec((None, n1, kr, m), lambda i: (1, 0, i, 0)),
            const((gd, gd)), const((gd, gd)),
            const((wh, d)), const((m, d)),
            const((1, d)), const((1, d)),
        ],
        out_specs=[pl.BlockSpec((tm, d), lambda i: (i, 0)), pl.BlockSpec((tm, d), lambda i: (i, 0))],
        out_shape=[jax.ShapeDtypeStruct((t, d), F32), jax.ShapeDtypeStruct((t, d), BF16)],
        scratch_shapes=[pltpu.VMEM((tm, m), BF16), pltpu.VMEM((tm, m), BF16)],
        compiler_params=pltpu.CompilerParams(
            dimension_semantics=("parallel",),
            vmem_limit_bytes=_vmem_limit(pipelined, resident)),
        name="mix_ln",
    )(x, ge, be, o_hgrn, p, p, c3, s3, woh, wof, g1, b1)


def _ffn_ln_kernel(h1_ref, h1b_ref, wg_ref, wu_ref, wd_ref, g2_ref, b2_ref, o_ref, acc_ref, *, alpha):
    f = pl.program_id(1)

    @pl.when(f == 0)
    def _():
        acc_ref[...] = jnp.zeros_like(acc_ref)

    hb = h1b_ref[...]
    gate = _dot(hb, wg_ref[...])
    up = _dot(hb, wu_ref[...])
    act = (gate * _sigmoid(gate) * up).astype(BF16)
    acc_ref[...] += _dot(act, wd_ref[...])

    @pl.when(f == pl.num_programs(1) - 1)
    def _():
        o_ref[...] = _layernorm(alpha * h1_ref[...] + acc_ref[...], g2_ref[...], b2_ref[...])


def _ffn_ln(h1, h1b, wg, wu, wd, g2, b2, *, alpha, tm=512, tf=512):
    t, d = h1.shape
    dff = wg.shape[1]
    pipelined = (_nbytes((tm, d), F32) + _nbytes((tm, d), BF16) + 3 * _nbytes((d, tf), BF16)
                 + _nbytes((tm, d), F32))
    resident = _nbytes((tm, d), F32) + 3 * _nbytes((tm, tf), F32) + _nbytes((tm, d), F32)
    return pl.pallas_call(
        functools.partial(_ffn_ln_kernel, alpha=alpha),
        grid=(t // tm, dff // tf),
        in_specs=[
            pl.BlockSpec((tm, d), lambda i, f: (i, 0)),
            pl.BlockSpec((tm, d), lambda i, f: (i, 0)),
            pl.BlockSpec((d, tf), lambda i, f: (0, f)),
            pl.BlockSpec((d, tf), lambda i, f: (0, f)),
            pl.BlockSpec((tf, d), lambda i, f: (f, 0)),
            pl.BlockSpec((1, d), lambda i, f: (0, 0)),
            pl.BlockSpec((1, d), lambda i, f: (0, 0)),
        ],
        out_specs=pl.BlockSpec((tm, d), lambda i, f: (i, 0)),
        out_shape=jax.ShapeDtypeStruct((t, d), F32),
        scratch_shapes=[pltpu.VMEM((tm, d), F32)],
        compiler_params=pltpu.CompilerParams(
            dimension_semantics=("parallel", "arbitrary"),
            vmem_limit_bytes=_vmem_limit(pipelined, resident)),
        name="ffn_ln",
    )(h1, h1b, wg, wu, wd, g2, b2)


def kernel(x, ln_emb_g, ln_emb_b, w_in, lb_fwd_logits, lb_bwd_logits, g_norm, w_out, ln1_g, ln1_b,
           w_gate, w_up, w_down, ln2_g, ln2_b):
    depth = w_in.shape[0]
    assert depth == 1, "the embedding LayerNorm is fused into the single layer's projection"
    batch, seq, d = x.shape
    assert batch == 1
    w = lb_fwd_logits.shape[1]
    alpha = (2.0 * depth) ** 0.25
    layer = 0
    row = lambda a: a.reshape(1, -1).astype(F32)

    x2 = x.reshape(seq, d)
    wi = w_in[layer]
    sw = w // LN_PROJ_STEPS
    section_order = (2, 3, 0, 1, 4, 5)
    wcat = jnp.stack([
        jnp.concatenate([wi[:, s * w + j * sw:s * w + (j + 1) * sw] for s in section_order], axis=1)
        for j in range(LN_PROJ_STEPS)]).astype(BF16)
    z, qig, four = _ln_proj(x2, row(ln_emb_g), row(ln_emb_b), wcat, (2 * sw, 3 * sw, sw))

    o_hgrn = _hgrn(lb_fwd_logits.astype(F32), lb_bwd_logits.astype(F32), row(g_norm[layer]), qig, z,
                   layer=layer)

    g1c, f2c, c3, s3 = _dft_constants(seq, FFT_N1)
    p = _fft_t2(_fft_t1(four, g1c), f2c)

    wo = w_out[layer].astype(BF16)
    h1, h1b = _mix_ln(x2, row(ln_emb_g), row(ln_emb_b), o_hgrn, p, c3, s3, wo[:w], wo[w:],
                      row(ln1_g[layer]), row(ln1_b[layer]), alpha=alpha)

    out = _ffn_ln(h1, h1b, w_gate[layer].astype(BF16), w_up[layer].astype(BF16),
                  w_down[layer].astype(BF16), row(ln2_g[layer]), row(ln2_b[layer]), alpha=alpha)
    return out.reshape(batch, seq, d)
```

```python
import functools
import math

import numpy as np
import jax
import jax.numpy as jnp
from jax import lax
from jax.experimental import pallas as pl
from jax.experimental.pallas import tpu as pltpu

LN_EPS = 1e-5
RMS_EPS = 1e-6
HEAD_DIM = 128
FOURIER_GDIM = 256
FFT_N1 = 64
HGRN_CHUNK = 128
LN_PROJ_STEPS = 4
V7X_VMEM_CAP = 56 * 1024 * 1024

BF16 = jnp.bfloat16
F32 = jnp.float32


def _vmem_limit(pipelined_bytes, resident_bytes):
    return int(min(V7X_VMEM_CAP, 2 * pipelined_bytes + resident_bytes + (4 << 20)))


def _nbytes(shape, dtype):
    return int(np.prod(shape)) * jnp.dtype(dtype).itemsize


def _dot(a, b):
    return jnp.dot(a, b, preferred_element_type=F32)


def _layernorm(x, g, b):
    mu = jnp.mean(x, axis=-1, keepdims=True)
    xc = x - mu
    var = jnp.mean(xc * xc, axis=-1, keepdims=True)
    return xc * lax.rsqrt(var + LN_EPS) * g + b


def _sigmoid(x):
    return 1.0 / (1.0 + jnp.exp(-x))


def _silu(x):
    return x * _sigmoid(x)


def _lower_bound(logits_ref, layer):
    logits = logits_ref[...]
    e = jnp.exp(logits - jnp.max(logits, axis=0, keepdims=True))
    return jnp.sum(e[:layer + 1], axis=0, keepdims=True) / jnp.sum(e, axis=0, keepdims=True)


def _ln_proj_kernel(x_ref, g_ref, b_ref, lbf_ref, lbb_ref, wzf_ref, wzb_ref, wq_ref, wi_ref, wg_ref, wf_ref,
                    lf_ref, kk_ref, q_ref, f_ref, h_scr, *, layer):
    @pl.when(pl.program_id(1) == 0)
    def _():
        h_scr[...] = _layernorm(x_ref[...], g_ref[...], b_ref[...]).astype(BF16)

    h = h_scr[...]
    sw = wf_ref.shape[1]
    for k, (w_ref, lb_ref) in enumerate(((wzf_ref, lbf_ref), (wzb_ref, lbb_ref))):
        lb = _lower_bound(lb_ref, layer)
        f = lb + (1.0 - lb) * _sigmoid(_dot(h, w_ref[...]))
        lf_ref[:, k * sw:(k + 1) * sw] = jnp.log2(f)
        kk_ref[:, k * sw:(k + 1) * sw] = (1.0 - f).astype(BF16)
    q_ref[:, 0:sw] = _silu(_dot(h, wq_ref[...])).astype(BF16)
    q_ref[:, sw:2 * sw] = _dot(h, wi_ref[...]).astype(BF16)
    q_ref[:, 2 * sw:3 * sw] = _silu(_dot(h, wg_ref[...])).astype(BF16)
    f_ref[...] = _dot(h, wf_ref[...])


def _ln_proj(x, g, b, lbf_logits, lbb_logits, w, *, layer, tm=1024):
    t, d = x.shape
    nj = LN_PROJ_STEPS
    sw = w.shape[1] // (6 * nj)
    slots = lbf_logits.shape[0]
    wspec = lambda section: pl.BlockSpec((d, sw), lambda i, j: (0, section * nj + j))
    pipelined = (_nbytes((tm, d), F32) + _nbytes((d, 6 * sw), BF16)
                 + _nbytes((tm, 3 * sw), F32) + _nbytes((tm, 5 * sw), BF16))
    resident = _nbytes((tm, d), BF16) + _nbytes((tm, 6 * sw), F32)
    return pl.pallas_call(
        functools.partial(_ln_proj_kernel, layer=layer),
        grid=(t // tm, nj),
        in_specs=[
            pl.BlockSpec((tm, d), lambda i, j: (i, 0)),
            pl.BlockSpec((1, d), lambda i, j: (0, 0)),
            pl.BlockSpec((1, d), lambda i, j: (0, 0)),
            pl.BlockSpec((slots, sw), lambda i, j: (0, j)),
            pl.BlockSpec((slots, sw), lambda i, j: (0, j)),
            wspec(2), wspec(3), wspec(0), wspec(1), wspec(4), wspec(5),
        ],
        out_specs=[
            pl.BlockSpec((tm, 2 * sw), lambda i, j: (i, j)),
            pl.BlockSpec((tm, 2 * sw), lambda i, j: (i, j)),
            pl.BlockSpec((tm, 3 * sw), lambda i, j: (i, j)),
            pl.BlockSpec((tm, sw), lambda i, j: (i, j)),
        ],
        out_shape=[
            jax.ShapeDtypeStruct((t, nj * 2 * sw), F32),
            jax.ShapeDtypeStruct((t, nj * 2 * sw), BF16),
            jax.ShapeDtypeStruct((t, nj * 3 * sw), BF16),
            jax.ShapeDtypeStruct((t, nj * sw), F32),
        ],
        scratch_shapes=[pltpu.VMEM((tm, d), BF16)],
        compiler_params=pltpu.CompilerParams(
            dimension_semantics=("parallel", "arbitrary"),
            vmem_limit_bytes=_vmem_limit(pipelined, resident)),
        name="ln_proj",
    )(x, g, b, lbf_logits, lbb_logits, w, w, w, w, w, w)


def _hgrn_kernel(gn_ref, q_ref, v_ref, g_ref, lf_ref, kk_ref, o_ref, st_ref, of_ref, *, nblk, blk_rows):
    c = HGRN_CHUNK
    d = HEAD_DIM
    half = c // 2
    nchunk = blk_rows // c
    heads = st_ref.shape[0]
    phase = pl.program_id(1)
    t = pl.program_id(2)

    @pl.when(t == 0)
    def _():
        st_ref[...] = jnp.zeros_like(st_ref)

    def scan_block(fwd):
        row = lax.broadcasted_iota(jnp.int32, (c, c), 0)
        col = lax.broadcasted_iota(jnp.int32, (c, c), 1)
        tri = (col <= row) if fwd else (col >= row)
        tri2 = jnp.concatenate([tri.astype(BF16)] * 2, axis=1)
        blk = t if fwd else nblk - 1 - t
        chunk_rows = [pl.ds((ci if fwd else nchunk - 1 - ci) * c, c) for ci in range(nchunk)]
        head_cols = [slice(h * d, (h + 1) * d) for h in range(heads)]
        trans_b = (((1,), (1,)), ((), ()))
        trans_a = (((0,), (0,)), ((), ()))

        bs = []
        for rows in chunk_rows:
            lf = lf_ref[rows, :]
            hi = lf.astype(BF16)
            lo = (lf - hi.astype(F32)).astype(BF16)
            bs.append(_dot(tri2, jnp.concatenate([hi, lo], axis=0)))

        qds, dsts, scs, decs, mids = [], [], [], [], []
        for rows, b in zip(chunk_rows, bs):
            b_mid = b[half - 1:half, :] if fwd else b[half:half + 1, :]
            b_end = b[c - 1:c, :] if fwd else b[0:1, :]
            kk = kk_ref[rows, :].astype(F32)
            qd = (q_ref[rows, :].astype(F32) * jnp.exp2(b - b_mid)).astype(BF16)
            ki = (kk * jnp.exp2(b_mid - b)).astype(BF16)
            ke = (kk * jnp.exp2(b_end - b)).astype(BF16)
            v = v_ref[rows, :]
            qds.append(qd)
            scs.append([lax.dot_general(qd[:, hs], ki[:, hs], trans_b, preferred_element_type=F32)
                        for hs in head_cols])
            dsts.append([lax.dot_general(v[:, hs], ke[:, hs], trans_a, preferred_element_type=F32)
                         for hs in head_cols])
            decs.append(jnp.exp2(b_end))
            mids.append(jnp.exp2(b_mid))

        s_mids = [[] for _ in chunk_rows]
        for h, hs in enumerate(head_cols):
            st = st_ref[h]
            for ci in range(nchunk):
                s_mids[ci].append((st * mids[ci][:, hs]).astype(BF16))
                st = decs[ci][:, hs] * st + dsts[ci][h]
            st_ref[h] = st

        for ci, rows in enumerate(chunk_rows):
            v = v_ref[rows, :]
            outs = []
            for h, hs in enumerate(head_cols):
                scores = jnp.where(tri, scs[ci][h], 0.0).astype(BF16)
                o = (_dot(scores, v[:, hs])
                     + lax.dot_general(qds[ci][:, hs], s_mids[ci][h], trans_b, preferred_element_type=F32))
                if not fwd:
                    o = o + of_ref[pl.ds(pl.multiple_of(blk * blk_rows, c) + rows.start, c), hs]
                    o = o * lax.rsqrt(jnp.mean(o * o, axis=-1, keepdims=True) + RMS_EPS) * gn_ref[...]
                outs.append(o)
            o = jnp.concatenate(outs, axis=1)
            if fwd:
                of_ref[pl.ds(pl.multiple_of(blk * blk_rows, c) + rows.start, c), :] = o
            else:
                o_ref[rows, :] = (o * g_ref[rows, :].astype(F32)).astype(BF16)

    pl.when(phase == 0)(functools.partial(scan_block, True))
    pl.when(phase == 1)(functools.partial(scan_block, False))


def _hgrn(g_norm, qig, log2f, one_minus_f, *, blk_rows=1024):
    t = log2f.shape[0]
    nj = LN_PROJ_STEPS
    sw = log2f.shape[1] // (2 * nj)
    heads = sw // HEAD_DIM
    nblk = t // blk_rows
    d = HEAD_DIM

    tblk = lambda p, i: i + p * (nblk - 1 - 2 * i)
    qspec = lambda sec: pl.BlockSpec((blk_rows, sw), lambda j, p, i: (tblk(p, i), 3 * j + sec))
    fspec = pl.BlockSpec((blk_rows, sw), lambda j, p, i: (tblk(p, i), 2 * j + p))

    pipelined = 5 * _nbytes((blk_rows, sw), BF16) + _nbytes((blk_rows, sw), F32)
    resident = _nbytes((t, sw), F32) + 128 * _nbytes((d, d), F32)
    return pl.pallas_call(
        functools.partial(_hgrn_kernel, nblk=nblk, blk_rows=blk_rows),
        grid=(nj, 2, nblk),
        in_specs=[
            pl.BlockSpec((1, d), lambda j, p, i: (0, 0)),
            qspec(0), qspec(1), qspec(2), fspec, fspec,
        ],
        out_specs=pl.BlockSpec((blk_rows, sw), lambda j, p, i: (nblk - 1 - p * i, j)),
        out_shape=jax.ShapeDtypeStruct((t, nj * sw), BF16),
        scratch_shapes=[pltpu.VMEM((heads, d, d), F32), pltpu.VMEM((t, sw), F32)],
        compiler_params=pltpu.CompilerParams(
            dimension_semantics=("parallel", "arbitrary", "arbitrary"),
            vmem_limit_bytes=_vmem_limit(pipelined, resident)),
        name="hgrn",
    )(g_norm, qig, qig, qig, log2f, one_minus_f)


def _dft_constants(t, n1):
    n2 = t // n1
    k1 = np.arange(n1)[None, :, None]
    t1 = np.arange(n1)[None, None, :]
    t2 = np.arange(n2)[:, None, None]
    ang = 2.0 * np.pi * ((k1 * (n2 * t1 + t2)) % t) / t
    g = np.concatenate([np.cos(ang), -np.sin(ang)], axis=1) / math.sqrt(n1)
    k2 = np.arange(n2)[:, None]
    s2 = np.arange(n2)[None, :]
    ang2 = 2.0 * np.pi * ((k2 * s2) % n2) / n2
    c2, sn2 = np.cos(ang2) / math.sqrt(n2), np.sin(ang2) / math.sqrt(n2)
    f2 = np.block([[c2, sn2], [-sn2, c2]])
    m = np.arange(FOURIER_GDIM)
    ang3 = 2.0 * np.pi * ((m[:, None] * m[None, :]) % FOURIER_GDIM) / FOURIER_GDIM
    c3 = np.cos(ang3) / math.sqrt(FOURIER_GDIM)
    s3 = np.sin(ang3) / math.sqrt(FOURIER_GDIM)
    as_bf16 = lambda a: jnp.asarray(a, dtype=F32).astype(BF16)
    return as_bf16(g), as_bf16(f2), as_bf16(c3), as_bf16(s3)


def _fft_t1_kernel(u_ref, g_ref, w_ref, *, tb, n1):
    x = pltpu.einshape("tjm->(jt)m", u_ref[...]).astype(BF16)
    r = jnp.stack([_dot(g_ref[j], x[j * n1:(j + 1) * n1]) for j in range(tb)])
    r = pltpu.einshape("jam->ajm", r)
    w_ref[0] = r[:n1]
    w_ref[1] = r[n1:]


def _fft_t1(four, g, *, tb=8):
    t, m = four.shape
    n2, _, n1 = g.shape
    u = four.reshape(n1, n2, m)
    pipelined = (_nbytes((n1, tb, m), F32) + _nbytes((tb, 2 * n1, n1), BF16)
                 + _nbytes((2, n1, tb, m), F32))
    return pl.pallas_call(
        functools.partial(_fft_t1_kernel, tb=tb, n1=n1),
        grid=(n2 // tb,),
        in_specs=[
            pl.BlockSpec((n1, tb, m), lambda s: (0, s, 0)),
            pl.BlockSpec((tb, 2 * n1, n1), lambda s: (s, 0, 0)),
        ],
        out_specs=pl.BlockSpec((2, n1, tb, m), lambda s: (0, 0, s, 0)),
        out_shape=jax.ShapeDtypeStruct((2, n1, n2, m), F32),
        compiler_params=pltpu.CompilerParams(
            dimension_semantics=("parallel",),
            vmem_limit_bytes=_vmem_limit(pipelined, 4 * _nbytes((2 * n1, m), F32))),
        name="fft_t1",
    )(u, g)


def _fft_t2_kernel(f_ref, w_ref, p_ref, *, kb, n2):
    for j in range(kb):
        rhs = jnp.concatenate([w_ref[0, j], w_ref[1, j]], axis=0).astype(BF16)
        res = _dot(f_ref[...], rhs)
        p_ref[0, j] = res[:n2]
        p_ref[1, j] = res[n2:]


def _fft_t2(w4, f2, *, kb=4):
    _, n1, n2, m = w4.shape
    blk = (2, kb, n2, m)
    pipelined = 2 * _nbytes(blk, F32) + _nbytes((2 * n2, 2 * n2), BF16)
    return pl.pallas_call(
        functools.partial(_fft_t2_kernel, kb=kb, n2=n2),
        grid=(n1 // kb,),
        in_specs=[
            pl.BlockSpec((2 * n2, 2 * n2), lambda s: (0, 0)),
            pl.BlockSpec(blk, lambda s: (0, s, 0, 0)),
        ],
        out_specs=pl.BlockSpec(blk, lambda s: (0, s, 0, 0)),
        out_shape=jax.ShapeDtypeStruct((2, n1, n2, m), F32),
        compiler_params=pltpu.CompilerParams(
            dimension_semantics=("parallel",),
            vmem_limit_bytes=_vmem_limit(pipelined, 4 * _nbytes((2 * n2, m), F32))),
        name="fft_t2",
    )(f2, w4)


def _mix_ln_kernel(x_ref, ge_ref, be_ref, oh_ref, pr_ref, pi_ref, c3_ref, s3_ref, wo_ref,
                   g1_ref, b1_ref, h1_ref, h1b_ref, pr_scr, pi_scr, *, alpha):
    gd = FOURIER_GDIM
    n1 = pr_ref.shape[0]
    pr_scr[...] = pltpu.einshape("akm->(ka)m", pr_ref[...]).astype(BF16)
    pi_scr[...] = pltpu.einshape("akm->(ka)m", pi_ref[...]).astype(BF16)
    c3, s3 = c3_ref[...], s3_ref[...]
    parts = []
    for g in range(pr_scr.shape[1] // gd):
        sl = slice(g * gd, (g + 1) * gd)
        parts.append((_dot(pr_scr[:, sl], c3) + _dot(pi_scr[:, sl], s3)).astype(BF16))
    o_four = jnp.concatenate(parts, axis=1)
    wh = oh_ref.shape[1]
    mix = _dot(oh_ref[...], wo_ref[:wh, :]) + _dot(o_four, wo_ref[wh:, :])
    h0 = _layernorm(x_ref[...], ge_ref[...], be_ref[...])
    h1 = _layernorm(alpha * h0 + mix, g1_ref[...], b1_ref[...])
    h1_ref[...] = h1
    h1b_ref[...] = h1.astype(BF16)


def _mix_ln(x, ge, be, o_hgrn, p, c3, s3, wo, g1, b1, *, alpha, tm=512):
    t, d = x.shape
    wh = o_hgrn.shape[1]
    _, n1, n2, m = p.shape
    kr = tm // n1
    gd = FOURIER_GDIM
    const = lambda shape: pl.BlockSpec(shape, lambda i: tuple(0 for _ in shape),
                                       pipeline_mode=pl.Buffered(1))
    pipelined = (_nbytes((tm, d), F32) + _nbytes((tm, wh), BF16) + 2 * _nbytes((n1, kr, m), F32)
                 + _nbytes((tm, d), F32) + _nbytes((tm, d), BF16))
    resident = _nbytes((wh + m, d), BF16) + 2 * _nbytes((tm, m), BF16) + 3 * _nbytes((tm, d), F32)
    return pl.pallas_call(
        functools.partial(_mix_ln_kernel, alpha=alpha),
        grid=(t // tm,),
        in_specs=[
            pl.BlockSpec((tm, d), lambda i: (i, 0)),
            const((1, d)), const((1, d)),
            pl.BlockSpec((tm, wh), lambda i: (i, 0)),
            pl.BlockSpec((None, n1, kr, m), lambda i: (0, 0, i, 0)),
            pl.BlockSpec((None, n1, kr, m), lambda i: (1, 0, i, 0)),
            const((gd, gd)), const((gd, gd)),
            const((wh + m, d)),
            const((1, d)), const((1, d)),
        ],
        out_specs=[pl.BlockSpec((tm, d), lambda i: (i, 0)), pl.BlockSpec((tm, d), lambda i: (i, 0))],
        out_shape=[jax.ShapeDtypeStruct((t, d), F32), jax.ShapeDtypeStruct((t, d), BF16)],
        scratch_shapes=[pltpu.VMEM((tm, m), BF16), pltpu.VMEM((tm, m), BF16)],
        compiler_params=pltpu.CompilerParams(
            dimension_semantics=("parallel",),
            vmem_limit_bytes=_vmem_limit(pipelined, resident)),
        name="mix_ln",
    )(x, ge, be, o_hgrn, p, p, c3, s3, wo, g1, b1)


def _ffn_ln_kernel(h1_ref, h1b_ref, wg_ref, wu_ref, wd_ref, g2_ref, b2_ref, o_ref, acc_ref, *, alpha):
    f = pl.program_id(1)

    @pl.when(f == 0)
    def _():
        acc_ref[...] = jnp.zeros_like(acc_ref)

    hb = h1b_ref[...]
    gate = _dot(hb, wg_ref[...])
    up = _dot(hb, wu_ref[...])
    act = (gate * _sigmoid(gate) * up).astype(BF16)
    acc_ref[...] += _dot(act, wd_ref[...])

    @pl.when(f == pl.num_programs(1) - 1)
    def _():
        o_ref[...] = _layernorm(alpha * h1_ref[...] + acc_ref[...], g2_ref[...], b2_ref[...])


def _ffn_ln(h1, h1b, wg, wu, wd, g2, b2, *, alpha, tm=512, tf=512):
    t, d = h1.shape
    dff = wg.shape[1]
    pipelined = (_nbytes((tm, d), F32) + _nbytes((tm, d), BF16) + 3 * _nbytes((d, tf), BF16)
                 + _nbytes((tm, d), F32))
    resident = _nbytes((tm, d), F32) + 3 * _nbytes((tm, tf), F32) + _nbytes((tm, d), F32)
    return pl.pallas_call(
        functools.partial(_ffn_ln_kernel, alpha=alpha),
        grid=(t // tm, dff // tf),
        in_specs=[
            pl.BlockSpec((tm, d), lambda i, f: (i, 0)),
            pl.BlockSpec((tm, d), lambda i, f: (i, 0)),
            pl.BlockSpec((d, tf), lambda i, f: (0, f)),
            pl.BlockSpec((d, tf), lambda i, f: (0, f)),
            pl.BlockSpec((tf, d), lambda i, f: (f, 0)),
            pl.BlockSpec((1, d), lambda i, f: (0, 0)),
            pl.BlockSpec((1, d), lambda i, f: (0, 0)),
        ],
        out_specs=pl.BlockSpec((tm, d), lambda i, f: (i, 0)),
        out_shape=jax.ShapeDtypeStruct((t, d), F32),
        scratch_shapes=[pltpu.VMEM((tm, d), F32)],
        compiler_params=pltpu.CompilerParams(
            dimension_semantics=("parallel", "arbitrary"),
            vmem_limit_bytes=_vmem_limit(pipelined, resident)),
        name="ffn_ln",
    )(h1, h1b, wg, wu, wd, g2, b2)


def kernel(x, ln_emb_g, ln_emb_b, w_in, lb_fwd_logits, lb_bwd_logits, g_norm, w_out, ln1_g, ln1_b,
           w_gate, w_up, w_down, ln2_g, ln2_b):
    depth = w_in.shape[0]
    assert depth == 1, "the embedding LayerNorm is fused into the single layer's projection"
    batch, seq, d = x.shape
    assert batch == 1
    w = lb_fwd_logits.shape[1]
    alpha = (2.0 * depth) ** 0.25
    layer = 0
    row = lambda a: a.reshape(1, -1).astype(F32)

    x2 = x.reshape(seq, d)
    log2f, one_minus_f, qig, four = _ln_proj(
        x2, row(ln_emb_g), row(ln_emb_b), lb_fwd_logits.astype(F32), lb_bwd_logits.astype(F32),
        w_in[layer].astype(BF16), layer=layer)

    o_hgrn = _hgrn(row(g_norm[layer]), qig, log2f, one_minus_f)

    g1c, f2c, c3, s3 = _dft_constants(seq, FFT_N1)
    p = _fft_t2(_fft_t1(four, g1c), f2c)

    wo = w_out[layer].astype(BF16)
    h1, h1b = _mix_ln(x2, row(ln_emb_g), row(ln_emb_b), o_hgrn, p, c3, s3, wo,
                      row(ln1_g[layer]), row(ln1_b[layer]), alpha=alpha)

    out = _ffn_ln(h1, h1b, w_gate[layer].astype(BF16), w_up[layer].astype(BF16),
                  w_down[layer].astype(BF16), row(ln2_g[layer]), row(ln2_b[layer]), alpha=alpha)
    return out.reshape(batch, seq, d)
```

```python
import functools
import math

import numpy as np
import jax
import jax.numpy as jnp
from jax import lax
from jax.experimental import pallas as pl
from jax.experimental.pallas import tpu as pltpu

LN_EPS = 1e-5
RMS_EPS = 1e-6
HEAD_DIM = 128
FOURIER_GDIM = 256
FFT_N1 = 32
BF16_ROWS = 16
HGRN_CHUNK = 128
LN_PROJ_STEPS = 4
V7X_VMEM_CAP = 56 * 1024 * 1024

BF16 = jnp.bfloat16
F32 = jnp.float32


def _vmem_limit(pipelined_bytes, resident_bytes):
    return int(min(V7X_VMEM_CAP, 2 * pipelined_bytes + resident_bytes + (4 << 20)))


def _nbytes(shape, dtype):
    return int(np.prod(shape)) * jnp.dtype(dtype).itemsize


def _dot(a, b):
    return jnp.dot(a, b, preferred_element_type=F32)


def _layernorm(x, g, b):
    mu = jnp.mean(x, axis=-1, keepdims=True)
    xc = x - mu
    var = jnp.mean(xc * xc, axis=-1, keepdims=True)
    return xc * lax.rsqrt(var + LN_EPS) * g + b


def _sigmoid(x):
    return 1.0 / (1.0 + jnp.exp(-x))


def _silu(x):
    return x * _sigmoid(x)


def _lower_bound(logits_ref, layer):
    logits = logits_ref[...]
    e = jnp.exp(logits - jnp.max(logits, axis=0, keepdims=True))
    return jnp.sum(e[:layer + 1], axis=0, keepdims=True) / jnp.sum(e, axis=0, keepdims=True)


def _ln_proj_kernel(x_ref, g_ref, b_ref, lbf_ref, lbb_ref, wzf_ref, wzb_ref, wq_ref, wi_ref, wg_ref, wf_ref,
                    lf_ref, kk_ref, q_ref, f_ref, h_scr, *, layer):
    @pl.when(pl.program_id(1) == 0)
    def _():
        h_scr[...] = _layernorm(x_ref[...], g_ref[...], b_ref[...]).astype(BF16)

    h = h_scr[...]
    sw = wf_ref.shape[1]
    for k, (w_ref, lb_ref) in enumerate(((wzf_ref, lbf_ref), (wzb_ref, lbb_ref))):
        lb = _lower_bound(lb_ref, layer)
        f = lb + (1.0 - lb) * _sigmoid(_dot(h, w_ref[...]))
        lf_ref[:, k * sw:(k + 1) * sw] = jnp.log2(f)
        kk_ref[:, k * sw:(k + 1) * sw] = (1.0 - f).astype(BF16)
    q_ref[:, 0:sw] = _silu(_dot(h, wq_ref[...])).astype(BF16)
    q_ref[:, sw:2 * sw] = _dot(h, wi_ref[...]).astype(BF16)
    q_ref[:, 2 * sw:3 * sw] = _silu(_dot(h, wg_ref[...])).astype(BF16)
    f_ref[...] = _dot(h, wf_ref[...]).astype(BF16)


def _ln_proj(x, g, b, lbf_logits, lbb_logits, w, *, layer, tm=1024):
    t, d = x.shape
    nj = LN_PROJ_STEPS
    sw = w.shape[1] // (6 * nj)
    slots = lbf_logits.shape[0]
    wspec = lambda section: pl.BlockSpec((d, sw), lambda i, j: (0, section * nj + j))
    pipelined = (_nbytes((tm, d), F32) + _nbytes((d, 6 * sw), BF16)
                 + _nbytes((tm, 2 * sw), F32) + _nbytes((tm, 6 * sw), BF16))
    resident = _nbytes((tm, d), BF16) + _nbytes((tm, 6 * sw), F32)
    return pl.pallas_call(
        functools.partial(_ln_proj_kernel, layer=layer),
        grid=(t // tm, nj),
        in_specs=[
            pl.BlockSpec((tm, d), lambda i, j: (i, 0)),
            pl.BlockSpec((1, d), lambda i, j: (0, 0)),
            pl.BlockSpec((1, d), lambda i, j: (0, 0)),
            pl.BlockSpec((slots, sw), lambda i, j: (0, j)),
            pl.BlockSpec((slots, sw), lambda i, j: (0, j)),
            wspec(2), wspec(3), wspec(0), wspec(1), wspec(4), wspec(5),
        ],
        out_specs=[
            pl.BlockSpec((tm, 2 * sw), lambda i, j: (i, j)),
            pl.BlockSpec((tm, 2 * sw), lambda i, j: (i, j)),
            pl.BlockSpec((tm, 3 * sw), lambda i, j: (i, j)),
            pl.BlockSpec((tm, sw), lambda i, j: (i, j)),
        ],
        out_shape=[
            jax.ShapeDtypeStruct((t, nj * 2 * sw), F32),
            jax.ShapeDtypeStruct((t, nj * 2 * sw), BF16),
            jax.ShapeDtypeStruct((t, nj * 3 * sw), BF16),
            jax.ShapeDtypeStruct((t, nj * sw), BF16),
        ],
        scratch_shapes=[pltpu.VMEM((tm, d), BF16)],
        compiler_params=pltpu.CompilerParams(
            dimension_semantics=("parallel", "arbitrary"),
            vmem_limit_bytes=_vmem_limit(pipelined, resident)),
        name="ln_proj",
    )(x, g, b, lbf_logits, lbb_logits, w, w, w, w, w, w)


def _hgrn_kernel(gn_ref, q_ref, v_ref, g_ref, lf_ref, kk_ref, o_ref, st_ref, of_ref, *, nblk, blk_rows):
    c = HGRN_CHUNK
    d = HEAD_DIM
    half = c // 2
    nchunk = blk_rows // c
    heads = st_ref.shape[0]
    phase = pl.program_id(1)
    t = pl.program_id(2)

    @pl.when(t == 0)
    def _():
        st_ref[...] = jnp.zeros_like(st_ref)

    def scan_block(fwd):
        row = lax.broadcasted_iota(jnp.int32, (c, c), 0)
        col = lax.broadcasted_iota(jnp.int32, (c, c), 1)
        tri = (col <= row) if fwd else (col >= row)
        tri2 = jnp.concatenate([tri.astype(BF16)] * 2, axis=1)
        blk = t if fwd else nblk - 1 - t
        chunk_rows = [pl.ds((ci if fwd else nchunk - 1 - ci) * c, c) for ci in range(nchunk)]
        head_cols = [slice(h * d, (h + 1) * d) for h in range(heads)]
        trans_b = (((1,), (1,)), ((), ()))
        trans_a = (((0,), (0,)), ((), ()))

        bs = []
        for rows in chunk_rows:
            lf = lf_ref[rows, :]
            hi = lf.astype(BF16)
            lo = (lf - hi.astype(F32)).astype(BF16)
            bs.append(_dot(tri2, jnp.concatenate([hi, lo], axis=0)))

        qds, dsts, scs, decs, mids = [], [], [], [], []
        for rows, b in zip(chunk_rows, bs):
            b_mid = b[half - 1:half, :] if fwd else b[half:half + 1, :]
            b_end = b[c - 1:c, :] if fwd else b[0:1, :]
            kk = kk_ref[rows, :].astype(F32)
            qd = (q_ref[rows, :].astype(F32) * jnp.exp2(b - b_mid)).astype(BF16)
            ki = (kk * jnp.exp2(b_mid - b)).astype(BF16)
            ke = (kk * jnp.exp2(b_end - b)).astype(BF16)
            v = v_ref[rows, :]
            qds.append(qd)
            scs.append([lax.dot_general(qd[:, hs], ki[:, hs], trans_b, preferred_element_type=F32)
                        for hs in head_cols])
            dsts.append([lax.dot_general(v[:, hs], ke[:, hs], trans_a, preferred_element_type=F32)
                         for hs in head_cols])
            decs.append(jnp.exp2(b_end))
            mids.append(jnp.exp2(b_mid))

        s_mids = [[] for _ in chunk_rows]
        for h, hs in enumerate(head_cols):
            st = st_ref[h]
            for ci in range(nchunk):
                s_mids[ci].append((st * mids[ci][:, hs]).astype(BF16))
                st = decs[ci][:, hs] * st + dsts[ci][h]
            st_ref[h] = st

        for ci, rows in enumerate(chunk_rows):
            v = v_ref[rows, :]
            outs = []
            for h, hs in enumerate(head_cols):
                scores = jnp.where(tri, scs[ci][h], 0.0).astype(BF16)
                o = (_dot(scores, v[:, hs])
                     + lax.dot_general(qds[ci][:, hs], s_mids[ci][h], trans_b, preferred_element_type=F32))
                if not fwd:
                    o = o + of_ref[pl.ds(pl.multiple_of(blk * blk_rows, c) + rows.start, c), hs]
                    o = o * lax.rsqrt(jnp.mean(o * o, axis=-1, keepdims=True) + RMS_EPS) * gn_ref[...]
                outs.append(o)
            o = jnp.concatenate(outs, axis=1)
            if fwd:
                of_ref[pl.ds(pl.multiple_of(blk * blk_rows, c) + rows.start, c), :] = o
            else:
                o_ref[rows, :] = (o * g_ref[rows, :].astype(F32)).astype(BF16)

    pl.when(phase == 0)(functools.partial(scan_block, True))
    pl.when(phase == 1)(functools.partial(scan_block, False))


def _hgrn(g_norm, qig, log2f, one_minus_f, *, blk_rows=1024):
    t = log2f.shape[0]
    nj = LN_PROJ_STEPS
    sw = log2f.shape[1] // (2 * nj)
    heads = sw // HEAD_DIM
    nblk = t // blk_rows
    d = HEAD_DIM

    tblk = lambda p, i: i + p * (nblk - 1 - 2 * i)
    qspec = lambda sec: pl.BlockSpec((blk_rows, sw), lambda j, p, i: (tblk(p, i), 3 * j + sec))
    fspec = pl.BlockSpec((blk_rows, sw), lambda j, p, i: (tblk(p, i), 2 * j + p))

    pipelined = 5 * _nbytes((blk_rows, sw), BF16) + _nbytes((blk_rows, sw), F32)
    resident = _nbytes((t, sw), F32) + 128 * _nbytes((d, d), F32)
    return pl.pallas_call(
        functools.partial(_hgrn_kernel, nblk=nblk, blk_rows=blk_rows),
        grid=(nj, 2, nblk),
        in_specs=[
            pl.BlockSpec((1, d), lambda j, p, i: (0, 0)),
            qspec(0), qspec(1),
            pl.BlockSpec((blk_rows, sw), lambda j, p, i: (nblk - 1 - p * i, 3 * j + 2)),
            fspec, fspec,
        ],
        out_specs=pl.BlockSpec((blk_rows, sw), lambda j, p, i: (nblk - 1 - p * i, j)),
        out_shape=jax.ShapeDtypeStruct((t, nj * sw), BF16),
        scratch_shapes=[pltpu.VMEM((heads, d, d), F32), pltpu.VMEM((t, sw), F32)],
        compiler_params=pltpu.CompilerParams(
            dimension_semantics=("parallel", "arbitrary", "arbitrary"),
            vmem_limit_bytes=_vmem_limit(pipelined, resident)),
        name="hgrn",
    )(g_norm, qig, qig, qig, log2f, one_minus_f)


def _dft_constants(t, n1):
    n2 = t // n1
    k1 = np.arange(n1)[None, :, None]
    t1 = np.arange(n1)[None, None, :]
    t2 = np.arange(n2)[:, None, None]
    ang = 2.0 * np.pi * ((k1 * (n2 * t1 + t2)) % t) / t
    g = np.concatenate([np.cos(ang), -np.sin(ang)], axis=1) / math.sqrt(n1)
    k2 = np.arange(n2)[:, None]
    s2 = np.arange(n2)[None, :]
    ang2 = 2.0 * np.pi * ((k2 * s2) % n2) / n2
    c2, sn2 = np.cos(ang2) / math.sqrt(n2), np.sin(ang2) / math.sqrt(n2)
    f2 = np.block([[c2, sn2], [-sn2, c2]])
    m = np.arange(FOURIER_GDIM)
    ang3 = 2.0 * np.pi * ((m[:, None] * m[None, :]) % FOURIER_GDIM) / FOURIER_GDIM
    c3 = np.cos(ang3) / math.sqrt(FOURIER_GDIM)
    s3 = np.sin(ang3) / math.sqrt(FOURIER_GDIM)
    as_bf16 = lambda a: jnp.asarray(a, dtype=F32).astype(BF16)
    return as_bf16(g), as_bf16(f2), as_bf16(c3), as_bf16(s3)


def _fft_t1_kernel(u_ref, g_ref, w_ref, *, tb, n1):
    x = pltpu.einshape("tjm->(jt)m", u_ref[...].astype(F32)).astype(BF16)
    r = jnp.stack([_dot(g_ref[j], x[j * n1:(j + 1) * n1]) for j in range(tb)])
    r = pltpu.einshape("jam->ajm", r).astype(BF16)
    w_ref[0] = r[:n1]
    w_ref[1] = r[n1:]


def _fft_t1(four, g, *, tb=BF16_ROWS):
    t, m = four.shape
    n2, _, n1 = g.shape
    u = four.reshape(n1, n2, m)
    pipelined = (_nbytes((n1, tb, m), BF16) + _nbytes((tb, 2 * n1, n1), BF16)
                 + _nbytes((2, n1, tb, m), BF16))
    return pl.pallas_call(
        functools.partial(_fft_t1_kernel, tb=tb, n1=n1),
        grid=(n2 // tb,),
        in_specs=[
            pl.BlockSpec((n1, tb, m), lambda s: (0, s, 0)),
            pl.BlockSpec((tb, 2 * n1, n1), lambda s: (s, 0, 0)),
        ],
        out_specs=pl.BlockSpec((2, n1, tb, m), lambda s: (0, 0, s, 0)),
        out_shape=jax.ShapeDtypeStruct((2, n1, n2, m), BF16),
        compiler_params=pltpu.CompilerParams(
            dimension_semantics=("parallel",),
            vmem_limit_bytes=_vmem_limit(pipelined, 6 * _nbytes((tb * n1, m), F32))),
        name="fft_t1",
    )(u, g)


def _fft_t2_kernel(f_ref, w_ref, p_ref, *, kb, n2):
    for j in range(kb):
        rhs = jnp.concatenate([w_ref[0, j], w_ref[1, j]], axis=0)
        res = _dot(f_ref[...], rhs)
        p_ref[0, j] = res[:n2].astype(BF16)
        p_ref[1, j] = res[n2:].astype(BF16)


def _fft_t2(w4, f2, *, kb=4):
    _, n1, n2, m = w4.shape
    blk = (2, kb, n2, m)
    pipelined = 2 * _nbytes(blk, BF16) + _nbytes((2 * n2, 2 * n2), BF16)
    return pl.pallas_call(
        functools.partial(_fft_t2_kernel, kb=kb, n2=n2),
        grid=(n1 // kb,),
        in_specs=[
            pl.BlockSpec((2 * n2, 2 * n2), lambda s: (0, 0)),
            pl.BlockSpec(blk, lambda s: (0, s, 0, 0)),
        ],
        out_specs=pl.BlockSpec(blk, lambda s: (0, s, 0, 0)),
        out_shape=jax.ShapeDtypeStruct((2, n1, n2, m), BF16),
        compiler_params=pltpu.CompilerParams(
            dimension_semantics=("parallel",),
            vmem_limit_bytes=_vmem_limit(pipelined, 4 * _nbytes((2 * n2, m), F32))),
        name="fft_t2",
    )(f2, w4)


def _mix_ln_kernel(x_ref, ge_ref, be_ref, oh_ref, pr_ref, pi_ref, c3_ref, s3_ref, wo_ref,
                   g1_ref, b1_ref, h1_ref, h1b_ref, pr_scr, pi_scr, *, alpha):
    gd = FOURIER_GDIM
    n1 = pr_ref.shape[0]
    pr_scr[...] = pltpu.einshape("akm->(ka)m", pr_ref[...].astype(F32)).astype(BF16)
    pi_scr[...] = pltpu.einshape("akm->(ka)m", pi_ref[...].astype(F32)).astype(BF16)
    c3, s3 = c3_ref[...], s3_ref[...]
    parts = []
    for g in range(pr_scr.shape[1] // gd):
        sl = slice(g * gd, (g + 1) * gd)
        parts.append((_dot(pr_scr[:, sl], c3) + _dot(pi_scr[:, sl], s3)).astype(BF16))
    o_four = jnp.concatenate(parts, axis=1)
    wh = oh_ref.shape[1]
    mix = _dot(oh_ref[...], wo_ref[:wh, :]) + _dot(o_four, wo_ref[wh:, :])
    h0 = _layernorm(x_ref[...], ge_ref[...], be_ref[...])
    h1 = _layernorm(alpha * h0 + mix, g1_ref[...], b1_ref[...])
    h1_ref[...] = h1
    h1b_ref[...] = h1.astype(BF16)


def _mix_ln(x, ge, be, o_hgrn, p, c3, s3, wo, g1, b1, *, alpha, tm=512):
    t, d = x.shape
    wh = o_hgrn.shape[1]
    _, n1, n2, m = p.shape
    kr = tm // n1
    gd = FOURIER_GDIM
    const = lambda shape: pl.BlockSpec(shape, lambda i: tuple(0 for _ in shape),
                                       pipeline_mode=pl.Buffered(1))
    pipelined = (_nbytes((tm, d), F32) + _nbytes((tm, wh), BF16) + 2 * _nbytes((n1, kr, m), BF16)
                 + _nbytes((tm, d), F32) + _nbytes((tm, d), BF16))
    resident = (_nbytes((wh + m, d), BF16) + 2 * _nbytes((tm, m), BF16) + 2 * _nbytes((tm, m), F32)
                + 3 * _nbytes((tm, d), F32))
    return pl.pallas_call(
        functools.partial(_mix_ln_kernel, alpha=alpha),
        grid=(t // tm,),
        in_specs=[
            pl.BlockSpec((tm, d), lambda i: (i, 0)),
            const((1, d)), const((1, d)),
            pl.BlockSpec((tm, wh), lambda i: (i, 0)),
            pl.BlockSpec((None, n1, kr, m), lambda i: (0, 0, i, 0)),
            pl.BlockSpec((None, n1, kr, m), lambda i: (1, 0, i, 0)),
            const((gd, gd)), const((gd, gd)),
            const((wh + m, d)),
            const((1, d)), const((1, d)),
        ],
        out_specs=[pl.BlockSpec((tm, d), lambda i: (i, 0)), pl.BlockSpec((tm, d), lambda i: (i, 0))],
        out_shape=[jax.ShapeDtypeStruct((t, d), F32), jax.ShapeDtypeStruct((t, d), BF16)],
        scratch_shapes=[pltpu.VMEM((tm, m), BF16), pltpu.VMEM((tm, m), BF16)],
        compiler_params=pltpu.CompilerParams(
            dimension_semantics=("parallel",),
            vmem_limit_bytes=_vmem_limit(pipelined, resident)),
        name="mix_ln",
    )(x, ge, be, o_hgrn, p, p, c3, s3, wo, g1, b1)


def _ffn_ln_kernel(h1_ref, h1b_ref, wg_ref, wu_ref, wd_ref, g2_ref, b2_ref, o_ref, acc_ref, *, alpha):
    f = pl.program_id(1)

    @pl.when(f == 0)
    def _():
        acc_ref[...] = jnp.zeros_like(acc_ref)

    hb = h1b_ref[...]
    gate = _dot(hb, wg_ref[...])
    up = _dot(hb, wu_ref[...])
    act = (gate * _sigmoid(gate) * up).astype(BF16)
    acc_ref[...] += _dot(act, wd_ref[...])

    @pl.when(f == pl.num_programs(1) - 1)
    def _():
        o_ref[...] = _layernorm(alpha * h1_ref[...] + acc_ref[...], g2_ref[...], b2_ref[...])


def _ffn_ln(h1, h1b, wg, wu, wd, g2, b2, *, alpha, tm=512, tf=512):
    t, d = h1.shape
    dff = wg.shape[1]
    pipelined = (_nbytes((tm, d), F32) + _nbytes((tm, d), BF16) + 3 * _nbytes((d, tf), BF16)
                 + _nbytes((tm, d), F32))
    resident = _nbytes((tm, d), F32) + 3 * _nbytes((tm, tf), F32) + _nbytes((tm, d), F32)
    return pl.pallas_call(
        functools.partial(_ffn_ln_kernel, alpha=alpha),
        grid=(t // tm, dff // tf),
        in_specs=[
            pl.BlockSpec((tm, d), lambda i, f: (i, 0)),
            pl.BlockSpec((tm, d), lambda i, f: (i, 0)),
            pl.BlockSpec((d, tf), lambda i, f: (0, f)),
            pl.BlockSpec((d, tf), lambda i, f: (0, f)),
            pl.BlockSpec((tf, d), lambda i, f: (f, 0)),
            pl.BlockSpec((1, d), lambda i, f: (0, 0)),
            pl.BlockSpec((1, d), lambda i, f: (0, 0)),
        ],
        out_specs=pl.BlockSpec((tm, d), lambda i, f: (i, 0)),
        out_shape=jax.ShapeDtypeStruct((t, d), F32),
        scratch_shapes=[pltpu.VMEM((tm, d), F32)],
        compiler_params=pltpu.CompilerParams(
            dimension_semantics=("parallel", "arbitrary"),
            vmem_limit_bytes=_vmem_limit(pipelined, resident)),
        name="ffn_ln",
    )(h1, h1b, wg, wu, wd, g2, b2)


def kernel(x, ln_emb_g, ln_emb_b, w_in, lb_fwd_logits, lb_bwd_logits, g_norm, w_out, ln1_g, ln1_b,
           w_gate, w_up, w_down, ln2_g, ln2_b):
    depth = w_in.shape[0]
    assert depth == 1, "the embedding LayerNorm is fused into the single layer's projection"
    batch, seq, d = x.shape
    assert batch == 1
    w = lb_fwd_logits.shape[1]
    alpha = (2.0 * depth) ** 0.25
    layer = 0
    row = lambda a: a.reshape(1, -1).astype(F32)

    x2 = x.reshape(seq, d)
    log2f, one_minus_f, qig, four = _ln_proj(
        x2, row(ln_emb_g), row(ln_emb_b), lb_fwd_logits.astype(F32), lb_bwd_logits.astype(F32),
        w_in[layer].astype(BF16), layer=layer)

    o_hgrn = _hgrn(row(g_norm[layer]), qig, log2f, one_minus_f)

    g1c, f2c, c3, s3 = _dft_constants(seq, FFT_N1)
    p = _fft_t2(_fft_t1(four, g1c), f2c)

    wo = w_out[layer].astype(BF16)
    h1, h1b = _mix_ln(x2, row(ln_emb_g), row(ln_emb_b), o_hgrn, p, c3, s3, wo,
                      row(ln1_g[layer]), row(ln1_b[layer]), alpha=alpha)

    out = _ffn_ln(h1, h1b, w_gate[layer].astype(BF16), w_up[layer].astype(BF16),
                  w_down[layer].astype(BF16), row(ln2_g[layer]), row(ln2_b[layer]), alpha=alpha)
    return out.reshape(batch, seq, d)
```

```python
import functools
import math

import numpy as np
import jax
import jax.numpy as jnp
from jax import lax
from jax.experimental import pallas as pl
from jax.experimental.pallas import tpu as pltpu

LN_EPS = 1e-5
RMS_EPS = 1e-6
HEAD_DIM = 128
FOURIER_GDIM = 256
FFT_N1 = 32
BF16_ROWS = 16
LN_ROWS = 256
HGRN_CHUNK = 128
LN_PROJ_STEPS = 4
V7X_VMEM_CAP = 60 * 1024 * 1024

BF16 = jnp.bfloat16
F32 = jnp.float32


def _vmem_limit(pipelined_bytes, resident_bytes):
    return int(min(V7X_VMEM_CAP, 2 * pipelined_bytes + resident_bytes + (4 << 20)))


def _nbytes(shape, dtype):
    return int(np.prod(shape)) * jnp.dtype(dtype).itemsize


def _dot(a, b):
    return jnp.dot(a, b, preferred_element_type=F32)


def _layernorm(x, g, b):
    mu = jnp.mean(x, axis=-1, keepdims=True)
    xc = x - mu
    var = jnp.mean(xc * xc, axis=-1, keepdims=True)
    return xc * lax.rsqrt(var + LN_EPS) * g + b


def _sigmoid(x):
    return 1.0 / (1.0 + jnp.exp(-x))


def _silu(x):
    return x * _sigmoid(x)


def _lower_bound(logits_ref, layer):
    logits = logits_ref[...]
    e = jnp.exp(logits - jnp.max(logits, axis=0, keepdims=True))
    return jnp.sum(e[:layer + 1], axis=0, keepdims=True) / jnp.sum(e, axis=0, keepdims=True)


def _ln_proj_kernel(x_ref, g_ref, b_ref, lbf_ref, lbb_ref, wzf_ref, wzb_ref, wq_ref, wi_ref, wg_ref, wf_ref,
                    lf_ref, kk_ref, q_ref, f_ref, h_scr, *, layer):
    @pl.when(pl.program_id(1) == 0)
    def _():
        for r in range(0, x_ref.shape[0], LN_ROWS):
            rows = slice(r, r + LN_ROWS)
            h_scr[rows, :] = _layernorm(x_ref[rows, :], g_ref[...], b_ref[...]).astype(BF16)

    h = h_scr[...]
    sw = wf_ref.shape[1]
    for k, (w_ref, lb_ref) in enumerate(((wzf_ref, lbf_ref), (wzb_ref, lbb_ref))):
        lb = _lower_bound(lb_ref, layer)
        f = lb + (1.0 - lb) * _sigmoid(_dot(h, w_ref[...]))
        lf_ref[:, k * sw:(k + 1) * sw] = jnp.log2(f)
        kk_ref[:, k * sw:(k + 1) * sw] = (1.0 - f).astype(BF16)
    q_ref[:, 0:sw] = _silu(_dot(h, wq_ref[...])).astype(BF16)
    q_ref[:, sw:2 * sw] = _dot(h, wi_ref[...]).astype(BF16)
    q_ref[:, 2 * sw:3 * sw] = _silu(_dot(h, wg_ref[...])).astype(BF16)
    f_ref[...] = _dot(h, wf_ref[...]).astype(BF16)


def _ln_proj(x, g, b, lbf_logits, lbb_logits, w, *, layer, tm=1024):
    t, d = x.shape
    nj = LN_PROJ_STEPS
    sw = w.shape[1] // (6 * nj)
    slots = lbf_logits.shape[0]
    wspec = lambda section: pl.BlockSpec((d, sw), lambda i, j: (0, section * nj + j))
    pipelined = (_nbytes((tm, d), F32) + _nbytes((d, 6 * sw), BF16)
                 + _nbytes((tm, 2 * sw), F32) + _nbytes((tm, 6 * sw), BF16))
    resident = _nbytes((tm, d), BF16) + _nbytes((tm, 6 * sw), F32)
    return pl.pallas_call(
        functools.partial(_ln_proj_kernel, layer=layer),
        grid=(t // tm, nj),
        in_specs=[
            pl.BlockSpec((tm, d), lambda i, j: (i, 0)),
            pl.BlockSpec((1, d), lambda i, j: (0, 0)),
            pl.BlockSpec((1, d), lambda i, j: (0, 0)),
            pl.BlockSpec((slots, sw), lambda i, j: (0, j)),
            pl.BlockSpec((slots, sw), lambda i, j: (0, j)),
            wspec(2), wspec(3), wspec(0), wspec(1), wspec(4), wspec(5),
        ],
        out_specs=[
            pl.BlockSpec((tm, 2 * sw), lambda i, j: (i, j)),
            pl.BlockSpec((tm, 2 * sw), lambda i, j: (i, j)),
            pl.BlockSpec((tm, 3 * sw), lambda i, j: (i, j)),
            pl.BlockSpec((tm, sw), lambda i, j: (i, j)),
        ],
        out_shape=[
            jax.ShapeDtypeStruct((t, nj * 2 * sw), F32),
            jax.ShapeDtypeStruct((t, nj * 2 * sw), BF16),
            jax.ShapeDtypeStruct((t, nj * 3 * sw), BF16),
            jax.ShapeDtypeStruct((t, nj * sw), BF16),
        ],
        scratch_shapes=[pltpu.VMEM((tm, d), BF16)],
        compiler_params=pltpu.CompilerParams(
            dimension_semantics=("parallel", "arbitrary"),
            vmem_limit_bytes=_vmem_limit(pipelined, resident)),
        name="ln_proj",
    )(x, g, b, lbf_logits, lbb_logits, w, w, w, w, w, w)


def _hgrn_kernel(gn_ref, q_ref, v_ref, g_ref, lf_ref, kk_ref, o_ref, st_ref, of_ref, *, nblk, blk_rows):
    c = HGRN_CHUNK
    d = HEAD_DIM
    half = c // 2
    nchunk = blk_rows // c
    heads = st_ref.shape[0]
    phase = pl.program_id(1)
    t = pl.program_id(2)

    @pl.when(t == 0)
    def _():
        st_ref[...] = jnp.zeros_like(st_ref)

    def scan_block(fwd):
        row = lax.broadcasted_iota(jnp.int32, (c, c), 0)
        col = lax.broadcasted_iota(jnp.int32, (c, c), 1)
        tri = (col <= row) if fwd else (col >= row)
        tri2 = jnp.concatenate([tri.astype(BF16)] * 2, axis=1)
        blk = t if fwd else nblk - 1 - t
        chunk_rows = [pl.ds((ci if fwd else nchunk - 1 - ci) * c, c) for ci in range(nchunk)]
        head_cols = [slice(h * d, (h + 1) * d) for h in range(heads)]
        trans_b = (((1,), (1,)), ((), ()))
        trans_a = (((0,), (0,)), ((), ()))

        bs = []
        for rows in chunk_rows:
            lf = lf_ref[rows, :]
            hi = lf.astype(BF16)
            lo = (lf - hi.astype(F32)).astype(BF16)
            bs.append(_dot(tri2, jnp.concatenate([hi, lo], axis=0)))

        qds, dsts, scs, decs, mids = [], [], [], [], []
        for rows, b in zip(chunk_rows, bs):
            b_mid = b[half - 1:half, :] if fwd else b[half:half + 1, :]
            b_end = b[c - 1:c, :] if fwd else b[0:1, :]
            kk = kk_ref[rows, :].astype(F32)
            qd = (q_ref[rows, :].astype(F32) * jnp.exp2(b - b_mid)).astype(BF16)
            ki = (kk * jnp.exp2(b_mid - b)).astype(BF16)
            ke = (kk * jnp.exp2(b_end - b)).astype(BF16)
            v = v_ref[rows, :]
            qds.append(qd)
            scs.append([lax.dot_general(qd[:, hs], ki[:, hs], trans_b, preferred_element_type=F32)
                        for hs in head_cols])
            dsts.append([lax.dot_general(v[:, hs], ke[:, hs], trans_a, preferred_element_type=F32)
                         for hs in head_cols])
            decs.append(jnp.exp2(b_end))
            mids.append(jnp.exp2(b_mid))

        s_mids = [[] for _ in chunk_rows]
        for h, hs in enumerate(head_cols):
            st = st_ref[h]
            for ci in range(nchunk):
                s_mids[ci].append((st * mids[ci][:, hs]).astype(BF16))
                st = decs[ci][:, hs] * st + dsts[ci][h]
            st_ref[h] = st

        for ci, rows in enumerate(chunk_rows):
            v = v_ref[rows, :]
            outs = []
            for h, hs in enumerate(head_cols):
                scores = jnp.where(tri, scs[ci][h], 0.0).astype(BF16)
                o = (_dot(scores, v[:, hs])
                     + lax.dot_general(qds[ci][:, hs], s_mids[ci][h], trans_b, preferred_element_type=F32))
                if not fwd:
                    o = o + of_ref[pl.ds(pl.multiple_of(blk * blk_rows, c) + rows.start, c), hs]
                    o = o * lax.rsqrt(jnp.mean(o * o, axis=-1, keepdims=True) + RMS_EPS) * gn_ref[...]
                outs.append(o)
            o = jnp.concatenate(outs, axis=1)
            if fwd:
                of_ref[pl.ds(pl.multiple_of(blk * blk_rows, c) + rows.start, c), :] = o
            else:
                o_ref[rows, :] = (o * g_ref[rows, :].astype(F32)).astype(BF16)

    pl.when(phase == 0)(functools.partial(scan_block, True))
    pl.when(phase == 1)(functools.partial(scan_block, False))


def _hgrn(g_norm, qig, log2f, one_minus_f, *, blk_rows=2048):
    t = log2f.shape[0]
    nj = LN_PROJ_STEPS
    sw = log2f.shape[1] // (2 * nj)
    heads = sw // HEAD_DIM
    nblk = t // blk_rows
    d = HEAD_DIM

    tblk = lambda p, i: i + p * (nblk - 1 - 2 * i)
    qspec = lambda sec: pl.BlockSpec((blk_rows, sw), lambda j, p, i: (tblk(p, i), 3 * j + sec))
    fspec = pl.BlockSpec((blk_rows, sw), lambda j, p, i: (tblk(p, i), 2 * j + p))

    pipelined = 5 * _nbytes((blk_rows, sw), BF16) + _nbytes((blk_rows, sw), F32)
    resident = _nbytes((t, sw), F32) + 128 * _nbytes((d, d), F32)
    return pl.pallas_call(
        functools.partial(_hgrn_kernel, nblk=nblk, blk_rows=blk_rows),
        grid=(nj, 2, nblk),
        in_specs=[
            pl.BlockSpec((1, d), lambda j, p, i: (0, 0)),
            qspec(0), qspec(1),
            pl.BlockSpec((blk_rows, sw), lambda j, p, i: (nblk - 1 - p * i, 3 * j + 2)),
            fspec, fspec,
        ],
        out_specs=pl.BlockSpec((blk_rows, sw), lambda j, p, i: (nblk - 1 - p * i, j)),
        out_shape=jax.ShapeDtypeStruct((t, nj * sw), BF16),
        scratch_shapes=[pltpu.VMEM((heads, d, d), F32), pltpu.VMEM((t, sw), F32)],
        compiler_params=pltpu.CompilerParams(
            dimension_semantics=("parallel", "arbitrary", "arbitrary"),
            vmem_limit_bytes=_vmem_limit(pipelined, resident)),
        name="hgrn",
    )(g_norm, qig, qig, qig, log2f, one_minus_f)


def _dft_constants(t, n1):
    n2 = t // n1
    k1 = np.arange(n1)[None, :, None]
    t1 = np.arange(n1)[None, None, :]
    t2 = np.arange(n2)[:, None, None]
    ang = 2.0 * np.pi * ((k1 * (n2 * t1 + t2)) % t) / t
    g = np.concatenate([np.cos(ang), -np.sin(ang)], axis=1) / math.sqrt(n1)
    k2 = np.arange(n2)[:, None]
    s2 = np.arange(n2)[None, :]
    ang2 = 2.0 * np.pi * ((k2 * s2) % n2) / n2
    c2, sn2 = np.cos(ang2) / math.sqrt(n2), np.sin(ang2) / math.sqrt(n2)
    f2 = np.block([[c2, sn2], [-sn2, c2]])
    m = np.arange(FOURIER_GDIM)
    ang3 = 2.0 * np.pi * ((m[:, None] * m[None, :]) % FOURIER_GDIM) / FOURIER_GDIM
    c3 = np.cos(ang3) / math.sqrt(FOURIER_GDIM)
    s3 = np.sin(ang3) / math.sqrt(FOURIER_GDIM)
    as_bf16 = lambda a: jnp.asarray(a, dtype=F32).astype(BF16)
    return as_bf16(g), as_bf16(f2), as_bf16(c3), as_bf16(s3)


def _fft_t1_kernel(u_ref, g_ref, w_ref, *, tb, n1):
    x = pltpu.einshape("tjm->(jt)m", u_ref[...].astype(F32)).astype(BF16)
    r = jnp.stack([_dot(g_ref[j], x[j * n1:(j + 1) * n1]) for j in range(tb)])
    r = pltpu.einshape("jam->ajm", r).astype(BF16)
    w_ref[0] = r[:n1]
    w_ref[1] = r[n1:]


def _fft_t1(four, g, *, tb=BF16_ROWS):
    t, m = four.shape
    n2, _, n1 = g.shape
    u = four.reshape(n1, n2, m)
    pipelined = (_nbytes((n1, tb, m), BF16) + _nbytes((tb, 2 * n1, n1), BF16)
                 + _nbytes((2, n1, tb, m), BF16))
    return pl.pallas_call(
        functools.partial(_fft_t1_kernel, tb=tb, n1=n1),
        grid=(n2 // tb,),
        in_specs=[
            pl.BlockSpec((n1, tb, m), lambda s: (0, s, 0)),
            pl.BlockSpec((tb, 2 * n1, n1), lambda s: (s, 0, 0)),
        ],
        out_specs=pl.BlockSpec((2, n1, tb, m), lambda s: (0, 0, s, 0)),
        out_shape=jax.ShapeDtypeStruct((2, n1, n2, m), BF16),
        compiler_params=pltpu.CompilerParams(
            dimension_semantics=("parallel",),
            vmem_limit_bytes=_vmem_limit(pipelined, 6 * _nbytes((tb * n1, m), F32))),
        name="fft_t1",
    )(u, g)


def _fft_t2_kernel(f_ref, w_ref, p_ref, *, kb, n2):
    for j in range(kb):
        rhs = jnp.concatenate([w_ref[0, j], w_ref[1, j]], axis=0)
        res = _dot(f_ref[...], rhs)
        p_ref[0, j] = res[:n2].astype(BF16)
        p_ref[1, j] = res[n2:].astype(BF16)


def _fft_t2(w4, f2, *, kb=4):
    _, n1, n2, m = w4.shape
    blk = (2, kb, n2, m)
    pipelined = 2 * _nbytes(blk, BF16) + _nbytes((2 * n2, 2 * n2), BF16)
    return pl.pallas_call(
        functools.partial(_fft_t2_kernel, kb=kb, n2=n2),
        grid=(n1 // kb,),
        in_specs=[
            pl.BlockSpec((2 * n2, 2 * n2), lambda s: (0, 0)),
            pl.BlockSpec(blk, lambda s: (0, s, 0, 0)),
        ],
        out_specs=pl.BlockSpec(blk, lambda s: (0, s, 0, 0)),
        out_shape=jax.ShapeDtypeStruct((2, n1, n2, m), BF16),
        compiler_params=pltpu.CompilerParams(
            dimension_semantics=("parallel",),
            vmem_limit_bytes=_vmem_limit(pipelined, 4 * _nbytes((2 * n2, m), F32))),
        name="fft_t2",
    )(f2, w4)


def _mix_ln_kernel(x_ref, ge_ref, be_ref, oh_ref, pr_ref, pi_ref, c3_ref, s3_ref, wo_ref,
                   g1_ref, b1_ref, h1_ref, pr_scr, pi_scr, *, alpha):
    gd = FOURIER_GDIM
    n1 = pr_ref.shape[0]
    pr_scr[...] = pltpu.einshape("akm->(ka)m", pr_ref[...].astype(F32)).astype(BF16)
    pi_scr[...] = pltpu.einshape("akm->(ka)m", pi_ref[...].astype(F32)).astype(BF16)
    c3, s3 = c3_ref[...], s3_ref[...]
    parts = []
    for g in range(pr_scr.shape[1] // gd):
        sl = slice(g * gd, (g + 1) * gd)
        parts.append((_dot(pr_scr[:, sl], c3) + _dot(pi_scr[:, sl], s3)).astype(BF16))
    o_four = jnp.concatenate(parts, axis=1)
    wh = oh_ref.shape[1]
    mix = _dot(oh_ref[...], wo_ref[:wh, :]) + _dot(o_four, wo_ref[wh:, :])
    h0 = _layernorm(x_ref[...], ge_ref[...], be_ref[...])
    h1_ref[...] = _layernorm(alpha * h0 + mix, g1_ref[...], b1_ref[...])


def _mix_ln(x, ge, be, o_hgrn, p, c3, s3, wo, g1, b1, *, alpha, tm=512):
    t, d = x.shape
    wh = o_hgrn.shape[1]
    _, n1, n2, m = p.shape
    kr = tm // n1
    gd = FOURIER_GDIM
    const = lambda shape: pl.BlockSpec(shape, lambda i: tuple(0 for _ in shape),
                                       pipeline_mode=pl.Buffered(1))
    pipelined = (_nbytes((tm, d), F32) + _nbytes((tm, wh), BF16) + 2 * _nbytes((n1, kr, m), BF16)
                 + _nbytes((tm, d), F32))
    resident = (_nbytes((wh + m, d), BF16) + 2 * _nbytes((tm, m), BF16) + 2 * _nbytes((tm, m), F32)
                + 3 * _nbytes((tm, d), F32))
    return pl.pallas_call(
        functools.partial(_mix_ln_kernel, alpha=alpha),
        grid=(t // tm,),
        in_specs=[
            pl.BlockSpec((tm, d), lambda i: (i, 0)),
            const((1, d)), const((1, d)),
            pl.BlockSpec((tm, wh), lambda i: (i, 0)),
            pl.BlockSpec((None, n1, kr, m), lambda i: (0, 0, i, 0)),
            pl.BlockSpec((None, n1, kr, m), lambda i: (1, 0, i, 0)),
            const((gd, gd)), const((gd, gd)),
            const((wh + m, d)),
            const((1, d)), const((1, d)),
        ],
        out_specs=pl.BlockSpec((tm, d), lambda i: (i, 0)),
        out_shape=jax.ShapeDtypeStruct((t, d), F32),
        scratch_shapes=[pltpu.VMEM((tm, m), BF16), pltpu.VMEM((tm, m), BF16)],
        compiler_params=pltpu.CompilerParams(
            dimension_semantics=("parallel",),
            vmem_limit_bytes=_vmem_limit(pipelined, resident)),
        name="mix_ln",
    )(x, ge, be, o_hgrn, p, p, c3, s3, wo, g1, b1)


def _ffn_ln_kernel(h1_ref, wg_ref, wu_ref, wd_ref, g2_ref, b2_ref, o_ref, hb_scr, *, alpha):
    f = pl.program_id(1)
    tf = wd_ref.shape[0]

    @pl.when(f == 0)
    def _():
        hb_scr[...] = h1_ref[...].astype(BF16)
        o_ref[...] = jnp.zeros_like(o_ref)

    hb = hb_scr[...]
    gate = _dot(hb, wg_ref[...])
    up = _dot(hb, wu_ref[...])
    act = (gate * _sigmoid(gate) * up).astype(BF16)
    for n in range(o_ref.shape[1] // tf):
        cols = slice(n * tf, (n + 1) * tf)
        o_ref[:, cols] += _dot(act, wd_ref[:, cols])

    @pl.when(f == pl.num_programs(1) - 1)
    def _():
        for r in range(0, o_ref.shape[0], LN_ROWS):
            rows = slice(r, r + LN_ROWS)
            o_ref[rows, :] = _layernorm(alpha * h1_ref[rows, :] + o_ref[rows, :], g2_ref[...], b2_ref[...])


def _ffn_ln(h1, wg, wu, wd, g2, b2, *, alpha, tm=1024, tf=512):
    t, d = h1.shape
    dff = wg.shape[1]
    pipelined = 2 * _nbytes((tm, d), F32) + 3 * _nbytes((d, tf), BF16)
    resident = _nbytes((tm, d), BF16) + 4 * _nbytes((tm, tf), F32)
    return pl.pallas_call(
        functools.partial(_ffn_ln_kernel, alpha=alpha),
        grid=(t // tm, dff // tf),
        in_specs=[
            pl.BlockSpec((tm, d), lambda i, f: (i, 0)),
            pl.BlockSpec((d, tf), lambda i, f: (0, f)),
            pl.BlockSpec((d, tf), lambda i, f: (0, f)),
            pl.BlockSpec((tf, d), lambda i, f: (f, 0)),
            pl.BlockSpec((1, d), lambda i, f: (0, 0)),
            pl.BlockSpec((1, d), lambda i, f: (0, 0)),
        ],
        out_specs=pl.BlockSpec((tm, d), lambda i, f: (i, 0)),
        out_shape=jax.ShapeDtypeStruct((t, d), F32),
        scratch_shapes=[pltpu.VMEM((tm, d), BF16)],
        compiler_params=pltpu.CompilerParams(
            dimension_semantics=("parallel", "arbitrary"),
            vmem_limit_bytes=_vmem_limit(pipelined, resident)),
        name="ffn_ln",
    )(h1, wg, wu, wd, g2, b2)


def kernel(x, ln_emb_g, ln_emb_b, w_in, lb_fwd_logits, lb_bwd_logits, g_norm, w_out, ln1_g, ln1_b,
           w_gate, w_up, w_down, ln2_g, ln2_b):
    depth = w_in.shape[0]
    assert depth == 1, "the embedding LayerNorm is fused into the single layer's projection"
    batch, seq, d = x.shape
    assert batch == 1
    w = lb_fwd_logits.shape[1]
    alpha = (2.0 * depth) ** 0.25
    layer = 0
    row = lambda a: a.reshape(1, -1).astype(F32)

    x2 = x.reshape(seq, d)
    log2f, one_minus_f, qig, four = _ln_proj(
        x2, row(ln_emb_g), row(ln_emb_b), lb_fwd_logits.astype(F32), lb_bwd_logits.astype(F32),
        w_in[layer].astype(BF16), layer=layer)

    o_hgrn = _hgrn(row(g_norm[layer]), qig, log2f, one_minus_f)

    g1c, f2c, c3, s3 = _dft_constants(seq, FFT_N1)
    p = _fft_t2(_fft_t1(four, g1c), f2c)

    wo = w_out[layer].astype(BF16)
    h1 = _mix_ln(x2, row(ln_emb_g), row(ln_emb_b), o_hgrn, p, c3, s3, wo,
                 row(ln1_g[layer]), row(ln1_b[layer]), alpha=alpha)

    out = _ffn_ln(h1, w_gate[layer].astype(BF16), w_up[layer].astype(BF16),
                  w_down[layer].astype(BF16), row(ln2_g[layer]), row(ln2_b[layer]), alpha=alpha)
    return out.reshape(batch, seq, d)
```

```python
import functools
import math

import numpy as np
import jax
import jax.numpy as jnp
from jax import lax
from jax.experimental import pallas as pl
from jax.experimental.pallas import tpu as pltpu

LN_EPS = 1e-5
RMS_EPS = 1e-6
HEAD_DIM = 128
FOURIER_GDIM = 256
FFT_N1 = 32
BF16_ROWS = 16
LN_ROWS = 256
FFN_OUT_SLAB = 512
HGRN_CHUNK = 128
LN_PROJ_STEPS = 4
V7X_VMEM_CAP = 60 * 1024 * 1024

BF16 = jnp.bfloat16
F32 = jnp.float32


def _vmem_limit(pipelined_bytes, resident_bytes):
    return int(min(V7X_VMEM_CAP, 2 * pipelined_bytes + resident_bytes + (4 << 20)))


def _nbytes(shape, dtype):
    return int(np.prod(shape)) * jnp.dtype(dtype).itemsize


def _dot(a, b):
    return jnp.dot(a, b, preferred_element_type=F32)


def _layernorm(x, g, b):
    mu = jnp.mean(x, axis=-1, keepdims=True)
    xc = x - mu
    var = jnp.mean(xc * xc, axis=-1, keepdims=True)
    return xc * lax.rsqrt(var + LN_EPS) * g + b


def _sigmoid(x):
    return 1.0 / (1.0 + jnp.exp(-x))


def _silu(x):
    return x * _sigmoid(x)


def _lower_bound(logits_ref, layer):
    logits = logits_ref[...]
    e = jnp.exp(logits - jnp.max(logits, axis=0, keepdims=True))
    return jnp.sum(e[:layer + 1], axis=0, keepdims=True) / jnp.sum(e, axis=0, keepdims=True)


def _ln_proj_kernel(x_ref, g_ref, b_ref, lbf_ref, lbb_ref, wzf_ref, wzb_ref, wq_ref, wi_ref, wg_ref, wf_ref,
                    lf_ref, kk_ref, q_ref, f_ref, h_scr, *, layer):
    @pl.when(pl.program_id(1) == 0)
    def _():
        for r in range(0, x_ref.shape[0], LN_ROWS):
            rows = slice(r, r + LN_ROWS)
            h_scr[rows, :] = _layernorm(x_ref[rows, :], g_ref[...], b_ref[...]).astype(BF16)

    h = h_scr[...]
    sw = wf_ref.shape[1]
    for k, (w_ref, lb_ref) in enumerate(((wzf_ref, lbf_ref), (wzb_ref, lbb_ref))):
        lb = _lower_bound(lb_ref, layer)
        f = lb + (1.0 - lb) * _sigmoid(_dot(h, w_ref[...]))
        lf_ref[:, k * sw:(k + 1) * sw] = jnp.log2(f)
        kk_ref[:, k * sw:(k + 1) * sw] = (1.0 - f).astype(BF16)
    q_ref[:, 0:sw] = _silu(_dot(h, wq_ref[...])).astype(BF16)
    q_ref[:, sw:2 * sw] = _dot(h, wi_ref[...]).astype(BF16)
    q_ref[:, 2 * sw:3 * sw] = _silu(_dot(h, wg_ref[...])).astype(BF16)
    f_ref[...] = _dot(h, wf_ref[...]).astype(BF16)


def _ln_proj(x, g, b, lbf_logits, lbb_logits, w, *, layer, tm=1024):
    t, d = x.shape
    nj = LN_PROJ_STEPS
    sw = w.shape[1] // (6 * nj)
    slots = lbf_logits.shape[0]
    wspec = lambda section: pl.BlockSpec((d, sw), lambda i, j: (0, section * nj + j))
    pipelined = (_nbytes((tm, d), F32) + _nbytes((d, 6 * sw), BF16)
                 + _nbytes((tm, 2 * sw), F32) + _nbytes((tm, 6 * sw), BF16))
    resident = _nbytes((tm, d), BF16) + _nbytes((tm, 6 * sw), F32)
    return pl.pallas_call(
        functools.partial(_ln_proj_kernel, layer=layer),
        grid=(t // tm, nj),
        in_specs=[
            pl.BlockSpec((tm, d), lambda i, j: (i, 0)),
            pl.BlockSpec((1, d), lambda i, j: (0, 0)),
            pl.BlockSpec((1, d), lambda i, j: (0, 0)),
            pl.BlockSpec((slots, sw), lambda i, j: (0, j)),
            pl.BlockSpec((slots, sw), lambda i, j: (0, j)),
            wspec(2), wspec(3), wspec(0), wspec(1), wspec(4), wspec(5),
        ],
        out_specs=[
            pl.BlockSpec((tm, 2 * sw), lambda i, j: (i, j)),
            pl.BlockSpec((tm, 2 * sw), lambda i, j: (i, j)),
            pl.BlockSpec((tm, 3 * sw), lambda i, j: (i, j)),
            pl.BlockSpec((tm, sw), lambda i, j: (i, j)),
        ],
        out_shape=[
            jax.ShapeDtypeStruct((t, nj * 2 * sw), F32),
            jax.ShapeDtypeStruct((t, nj * 2 * sw), BF16),
            jax.ShapeDtypeStruct((t, nj * 3 * sw), BF16),
            jax.ShapeDtypeStruct((t, nj * sw), BF16),
        ],
        scratch_shapes=[pltpu.VMEM((tm, d), BF16)],
        compiler_params=pltpu.CompilerParams(
            dimension_semantics=("parallel", "arbitrary"),
            vmem_limit_bytes=_vmem_limit(pipelined, resident)),
        name="ln_proj",
    )(x, g, b, lbf_logits, lbb_logits, w, w, w, w, w, w)


def _hgrn_kernel(gn_ref, q_ref, v_ref, g_ref, lf_ref, kk_ref, o_ref, st_ref, of_ref, *, nblk, blk_rows):
    c = HGRN_CHUNK
    d = HEAD_DIM
    half = c // 2
    nchunk = blk_rows // c
    heads = st_ref.shape[0]
    phase = pl.program_id(1)
    t = pl.program_id(2)

    @pl.when(t == 0)
    def _():
        st_ref[...] = jnp.zeros_like(st_ref)

    def scan_block(fwd):
        row = lax.broadcasted_iota(jnp.int32, (c, c), 0)
        col = lax.broadcasted_iota(jnp.int32, (c, c), 1)
        tri = (col <= row) if fwd else (col >= row)
        tri2 = jnp.concatenate([tri.astype(BF16)] * 2, axis=1)
        blk = t if fwd else nblk - 1 - t
        chunk_rows = [pl.ds((ci if fwd else nchunk - 1 - ci) * c, c) for ci in range(nchunk)]
        head_cols = [slice(h * d, (h + 1) * d) for h in range(heads)]
        trans_b = (((1,), (1,)), ((), ()))
        trans_a = (((0,), (0,)), ((), ()))

        bs = []
        for rows in chunk_rows:
            lf = lf_ref[rows, :]
            hi = lf.astype(BF16)
            lo = (lf - hi.astype(F32)).astype(BF16)
            bs.append(_dot(tri2, jnp.concatenate([hi, lo], axis=0)))

        qds, dsts, scs, decs, mids = [], [], [], [], []
        for rows, b in zip(chunk_rows, bs):
            b_mid = b[half - 1:half, :] if fwd else b[half:half + 1, :]
            b_end = b[c - 1:c, :] if fwd else b[0:1, :]
            kk = kk_ref[rows, :].astype(F32)
            qd = (q_ref[rows, :].astype(F32) * jnp.exp2(b - b_mid)).astype(BF16)
            ki = (kk * jnp.exp2(b_mid - b)).astype(BF16)
            ke = (kk * jnp.exp2(b_end - b)).astype(BF16)
            v = v_ref[rows, :]
            qds.append(qd)
            scs.append([lax.dot_general(qd[:, hs], ki[:, hs], trans_b, preferred_element_type=F32)
                        for hs in head_cols])
            dsts.append([lax.dot_general(v[:, hs], ke[:, hs], trans_a, preferred_element_type=F32)
                         for hs in head_cols])
            decs.append(jnp.exp2(b_end))
            mids.append(jnp.exp2(b_mid))

        s_mids = [[] for _ in chunk_rows]
        for h, hs in enumerate(head_cols):
            st = st_ref[h]
            for ci in range(nchunk):
                s_mids[ci].append((st * mids[ci][:, hs]).astype(BF16))
                st = decs[ci][:, hs] * st + dsts[ci][h]
            st_ref[h] = st

        for ci, rows in enumerate(chunk_rows):
            v = v_ref[rows, :]
            outs = []
            for h, hs in enumerate(head_cols):
                scores = jnp.where(tri, scs[ci][h], 0.0).astype(BF16)
                o = (_dot(scores, v[:, hs])
                     + lax.dot_general(qds[ci][:, hs], s_mids[ci][h], trans_b, preferred_element_type=F32))
                if not fwd:
                    o = o + of_ref[pl.ds(pl.multiple_of(blk * blk_rows, c) + rows.start, c), hs]
                    o = o * lax.rsqrt(jnp.mean(o * o, axis=-1, keepdims=True) + RMS_EPS) * gn_ref[...]
                outs.append(o)
            o = jnp.concatenate(outs, axis=1)
            if fwd:
                of_ref[pl.ds(pl.multiple_of(blk * blk_rows, c) + rows.start, c), :] = o
            else:
                o_ref[rows, :] = (o * g_ref[rows, :].astype(F32)).astype(BF16)

    pl.when(phase == 0)(functools.partial(scan_block, True))
    pl.when(phase == 1)(functools.partial(scan_block, False))


def _hgrn(g_norm, qig, log2f, one_minus_f, *, blk_rows=2048):
    t = log2f.shape[0]
    nj = LN_PROJ_STEPS
    sw = log2f.shape[1] // (2 * nj)
    heads = sw // HEAD_DIM
    nblk = t // blk_rows
    d = HEAD_DIM

    tblk = lambda p, i: i + p * (nblk - 1 - 2 * i)
    qspec = lambda sec: pl.BlockSpec((blk_rows, sw), lambda j, p, i: (tblk(p, i), 3 * j + sec))
    fspec = pl.BlockSpec((blk_rows, sw), lambda j, p, i: (tblk(p, i), 2 * j + p))

    pipelined = 5 * _nbytes((blk_rows, sw), BF16) + _nbytes((blk_rows, sw), F32)
    resident = _nbytes((t, sw), F32) + 128 * _nbytes((d, d), F32)
    return pl.pallas_call(
        functools.partial(_hgrn_kernel, nblk=nblk, blk_rows=blk_rows),
        grid=(nj, 2, nblk),
        in_specs=[
            pl.BlockSpec((1, d), lambda j, p, i: (0, 0)),
            qspec(0), qspec(1),
            pl.BlockSpec((blk_rows, sw), lambda j, p, i: (nblk - 1 - p * i, 3 * j + 2)),
            fspec, fspec,
        ],
        out_specs=pl.BlockSpec((blk_rows, sw), lambda j, p, i: (nblk - 1 - p * i, j)),
        out_shape=jax.ShapeDtypeStruct((t, nj * sw), BF16),
        scratch_shapes=[pltpu.VMEM((heads, d, d), F32), pltpu.VMEM((t, sw), F32)],
        compiler_params=pltpu.CompilerParams(
            dimension_semantics=("parallel", "arbitrary", "arbitrary"),
            vmem_limit_bytes=_vmem_limit(pipelined, resident)),
        name="hgrn",
    )(g_norm, qig, qig, qig, log2f, one_minus_f)


def _dft_constants(t, n1):
    n2 = t // n1
    k1 = np.arange(n1)[None, :, None]
    t1 = np.arange(n1)[None, None, :]
    t2 = np.arange(n2)[:, None, None]
    ang = 2.0 * np.pi * ((k1 * (n2 * t1 + t2)) % t) / t
    g = np.concatenate([np.cos(ang), -np.sin(ang)], axis=1) / math.sqrt(n1)
    k2 = np.arange(n2)[:, None]
    s2 = np.arange(n2)[None, :]
    ang2 = 2.0 * np.pi * ((k2 * s2) % n2) / n2
    c2, sn2 = np.cos(ang2) / math.sqrt(n2), np.sin(ang2) / math.sqrt(n2)
    f2 = np.block([[c2, sn2], [-sn2, c2]])
    m = np.arange(FOURIER_GDIM)
    ang3 = 2.0 * np.pi * ((m[:, None] * m[None, :]) % FOURIER_GDIM) / FOURIER_GDIM
    c3 = np.cos(ang3) / math.sqrt(FOURIER_GDIM)
    s3 = np.sin(ang3) / math.sqrt(FOURIER_GDIM)
    as_bf16 = lambda a: jnp.asarray(a, dtype=F32).astype(BF16)
    return as_bf16(g), as_bf16(f2), as_bf16(c3), as_bf16(s3)


def _fft_t1_kernel(u_ref, g_ref, w_ref, *, tb, n1):
    x = pltpu.einshape("tjm->(jt)m", u_ref[...].astype(F32)).astype(BF16)
    r = jnp.stack([_dot(g_ref[j], x[j * n1:(j + 1) * n1]) for j in range(tb)])
    r = pltpu.einshape("jam->ajm", r).astype(BF16)
    w_ref[0] = r[:n1]
    w_ref[1] = r[n1:]


def _fft_t1(four, g, *, tb=BF16_ROWS):
    t, m = four.shape
    n2, _, n1 = g.shape
    u = four.reshape(n1, n2, m)
    pipelined = (_nbytes((n1, tb, m), BF16) + _nbytes((tb, 2 * n1, n1), BF16)
                 + _nbytes((2, n1, tb, m), BF16))
    return pl.pallas_call(
        functools.partial(_fft_t1_kernel, tb=tb, n1=n1),
        grid=(n2 // tb,),
        in_specs=[
            pl.BlockSpec((n1, tb, m), lambda s: (0, s, 0)),
            pl.BlockSpec((tb, 2 * n1, n1), lambda s: (s, 0, 0)),
        ],
        out_specs=pl.BlockSpec((2, n1, tb, m), lambda s: (0, 0, s, 0)),
        out_shape=jax.ShapeDtypeStruct((2, n1, n2, m), BF16),
        compiler_params=pltpu.CompilerParams(
            dimension_semantics=("parallel",),
            vmem_limit_bytes=_vmem_limit(pipelined, 6 * _nbytes((tb * n1, m), F32))),
        name="fft_t1",
    )(u, g)


def _fft_t2_kernel(f_ref, w_ref, p_ref, *, kb, n2):
    for j in range(kb):
        rhs = jnp.concatenate([w_ref[0, j], w_ref[1, j]], axis=0)
        res = _dot(f_ref[...], rhs)
        p_ref[0, j] = res[:n2].astype(BF16)
        p_ref[1, j] = res[n2:].astype(BF16)


def _fft_t2(w4, f2, *, kb=4):
    _, n1, n2, m = w4.shape
    blk = (2, kb, n2, m)
    pipelined = 2 * _nbytes(blk, BF16) + _nbytes((2 * n2, 2 * n2), BF16)
    return pl.pallas_call(
        functools.partial(_fft_t2_kernel, kb=kb, n2=n2),
        grid=(n1 // kb,),
        in_specs=[
            pl.BlockSpec((2 * n2, 2 * n2), lambda s: (0, 0)),
            pl.BlockSpec(blk, lambda s: (0, s, 0, 0)),
        ],
        out_specs=pl.BlockSpec(blk, lambda s: (0, s, 0, 0)),
        out_shape=jax.ShapeDtypeStruct((2, n1, n2, m), BF16),
        compiler_params=pltpu.CompilerParams(
            dimension_semantics=("parallel",),
            vmem_limit_bytes=_vmem_limit(pipelined, 4 * _nbytes((2 * n2, m), F32))),
        name="fft_t2",
    )(f2, w4)


def _mix_ln_kernel(x_ref, ge_ref, be_ref, oh_ref, pr_ref, pi_ref, c3_ref, s3_ref, wo_ref,
                   g1_ref, b1_ref, h1_ref, pr_scr, pi_scr, *, alpha):
    gd = FOURIER_GDIM
    n1 = pr_ref.shape[0]
    pr_scr[...] = pltpu.einshape("akm->(ka)m", pr_ref[...].astype(F32)).astype(BF16)
    pi_scr[...] = pltpu.einshape("akm->(ka)m", pi_ref[...].astype(F32)).astype(BF16)
    c3, s3 = c3_ref[...], s3_ref[...]
    parts = []
    for g in range(pr_scr.shape[1] // gd):
        sl = slice(g * gd, (g + 1) * gd)
        parts.append((_dot(pr_scr[:, sl], c3) + _dot(pi_scr[:, sl], s3)).astype(BF16))
    o_four = jnp.concatenate(parts, axis=1)
    wh = oh_ref.shape[1]
    mix = _dot(oh_ref[...], wo_ref[:wh, :]) + _dot(o_four, wo_ref[wh:, :])
    h0 = _layernorm(x_ref[...], ge_ref[...], be_ref[...])
    h1_ref[...] = _layernorm(alpha * h0 + mix, g1_ref[...], b1_ref[...])


def _mix_ln(x, ge, be, o_hgrn, p, c3, s3, wo, g1, b1, *, alpha, tm=512):
    t, d = x.shape
    wh = o_hgrn.shape[1]
    _, n1, n2, m = p.shape
    kr = tm // n1
    gd = FOURIER_GDIM
    const = lambda shape: pl.BlockSpec(shape, lambda i: tuple(0 for _ in shape),
                                       pipeline_mode=pl.Buffered(1))
    pipelined = (_nbytes((tm, d), F32) + _nbytes((tm, wh), BF16) + 2 * _nbytes((n1, kr, m), BF16)
                 + _nbytes((tm, d), F32))
    resident = (_nbytes((wh + m, d), BF16) + 2 * _nbytes((tm, m), BF16) + 2 * _nbytes((tm, m), F32)
                + 3 * _nbytes((tm, d), F32))
    return pl.pallas_call(
        functools.partial(_mix_ln_kernel, alpha=alpha),
        grid=(t // tm,),
        in_specs=[
            pl.BlockSpec((tm, d), lambda i: (i, 0)),
            const((1, d)), const((1, d)),
            pl.BlockSpec((tm, wh), lambda i: (i, 0)),
            pl.BlockSpec((None, n1, kr, m), lambda i: (0, 0, i, 0)),
            pl.BlockSpec((None, n1, kr, m), lambda i: (1, 0, i, 0)),
            const((gd, gd)), const((gd, gd)),
            const((wh + m, d)),
            const((1, d)), const((1, d)),
        ],
        out_specs=pl.BlockSpec((tm, d), lambda i: (i, 0)),
        out_shape=jax.ShapeDtypeStruct((t, d), F32),
        scratch_shapes=[pltpu.VMEM((tm, m), BF16), pltpu.VMEM((tm, m), BF16)],
        compiler_params=pltpu.CompilerParams(
            dimension_semantics=("parallel",),
            vmem_limit_bytes=_vmem_limit(pipelined, resident)),
        name="mix_ln",
    )(x, ge, be, o_hgrn, p, p, c3, s3, wo, g1, b1)


def _ffn_ln_kernel(h1_ref, wg_ref, wu_ref, wd_ref, g2_ref, b2_ref, o_ref, hb_scr, *, alpha):
    f = pl.program_id(1)

    @pl.when(f == 0)
    def _():
        hb_scr[...] = h1_ref[...].astype(BF16)
        o_ref[...] = jnp.zeros_like(o_ref)

    hb = hb_scr[...]
    gate = _dot(hb, wg_ref[...].astype(BF16))
    up = _dot(hb, wu_ref[...].astype(BF16))
    act = (gate * _sigmoid(gate) * up).astype(BF16)
    cw = FFN_OUT_SLAB
    for n in range(o_ref.shape[1] // cw):
        cols = slice(n * cw, (n + 1) * cw)
        o_ref[:, cols] += _dot(act, wd_ref[:, cols].astype(BF16))

    @pl.when(f == pl.num_programs(1) - 1)
    def _():
        for r in range(0, o_ref.shape[0], LN_ROWS):
            rows = slice(r, r + LN_ROWS)
            o_ref[rows, :] = _layernorm(alpha * h1_ref[rows, :] + o_ref[rows, :], g2_ref[...], b2_ref[...])


def _ffn_ln(h1, wg, wu, wd, g2, b2, *, alpha, tm=1024, tf=256):
    t, d = h1.shape
    dff = wg.shape[1]
    pipelined = 2 * _nbytes((tm, d), F32) + 3 * _nbytes((d, tf), F32)
    resident = _nbytes((tm, d), BF16) + 2 * _nbytes((tm, tf), F32) + 2 * _nbytes((tm, FFN_OUT_SLAB), F32)
    return pl.pallas_call(
        functools.partial(_ffn_ln_kernel, alpha=alpha),
        grid=(t // tm, dff // tf),
        in_specs=[
            pl.BlockSpec((tm, d), lambda i, f: (i, 0)),
            pl.BlockSpec((d, tf), lambda i, f: (0, f)),
            pl.BlockSpec((d, tf), lambda i, f: (0, f)),
            pl.BlockSpec((tf, d), lambda i, f: (f, 0)),
            pl.BlockSpec((1, d), lambda i, f: (0, 0)),
            pl.BlockSpec((1, d), lambda i, f: (0, 0)),
        ],
        out_specs=pl.BlockSpec((tm, d), lambda i, f: (i, 0)),
        out_shape=jax.ShapeDtypeStruct((t, d), F32),
        scratch_shapes=[pltpu.VMEM((tm, d), BF16)],
        compiler_params=pltpu.CompilerParams(
            dimension_semantics=("parallel", "arbitrary"),
            vmem_limit_bytes=_vmem_limit(pipelined, resident)),
        name="ffn_ln",
    )(h1, wg, wu, wd, g2, b2)


def kernel(x, ln_emb_g, ln_emb_b, w_in, lb_fwd_logits, lb_bwd_logits, g_norm, w_out, ln1_g, ln1_b,
           w_gate, w_up, w_down, ln2_g, ln2_b):
    depth = w_in.shape[0]
    assert depth == 1, "the embedding LayerNorm is fused into the single layer's projection"
    batch, seq, d = x.shape
    assert batch == 1
    w = lb_fwd_logits.shape[1]
    alpha = (2.0 * depth) ** 0.25
    layer = 0
    row = lambda a: a.reshape(1, -1).astype(F32)

    x2 = x.reshape(seq, d)
    log2f, one_minus_f, qig, four = _ln_proj(
        x2, row(ln_emb_g), row(ln_emb_b), lb_fwd_logits.astype(F32), lb_bwd_logits.astype(F32),
        w_in[layer].astype(BF16), layer=layer)

    o_hgrn = _hgrn(row(g_norm[layer]), qig, log2f, one_minus_f)

    g1c, f2c, c3, s3 = _dft_constants(seq, FFT_N1)
    p = _fft_t2(_fft_t1(four, g1c), f2c)

    wo = w_out[layer].astype(BF16)
    h1 = _mix_ln(x2, row(ln_emb_g), row(ln_emb_b), o_hgrn, p, c3, s3, wo,
                 row(ln1_g[layer]), row(ln1_b[layer]), alpha=alpha)

    out = _ffn_ln(h1, w_gate[layer], w_up[layer], w_down[layer],
                  row(ln2_g[layer]), row(ln2_b[layer]), alpha=alpha)
    return out.reshape(batch, seq, d)
```

```python
import functools
import math

import numpy as np
import jax
import jax.numpy as jnp
from jax import lax
from jax.experimental import pallas as pl
from jax.experimental.pallas import tpu as pltpu

LN_EPS = 1e-5
RMS_EPS = 1e-6
HEAD_DIM = 128
FOURIER_GDIM = 256
FFT_N1 = 32
BF16_ROWS = 16
LN_ROWS = 256
FFN_OUT_SLAB = 512
HGRN_CHUNK = 128
LN_PROJ_STEPS = 4
V7X_VMEM_CAP = 60 * 1024 * 1024

BF16 = jnp.bfloat16
F32 = jnp.float32


def _vmem_limit(pipelined_bytes, resident_bytes):
    return int(min(V7X_VMEM_CAP, 2 * pipelined_bytes + resident_bytes + (4 << 20)))


def _nbytes(shape, dtype):
    return int(np.prod(shape)) * jnp.dtype(dtype).itemsize


def _dot(a, b):
    return jnp.dot(a, b, preferred_element_type=F32)


def _layernorm(x, g, b):
    mu = jnp.mean(x, axis=-1, keepdims=True)
    xc = x - mu
    var = jnp.mean(xc * xc, axis=-1, keepdims=True)
    return xc * lax.rsqrt(var + LN_EPS) * g + b


def _sigmoid(x):
    return 1.0 / (1.0 + jnp.exp(-x))


def _silu(x):
    return x * _sigmoid(x)


def _lower_bound(logits_ref, layer):
    logits = logits_ref[...]
    e = jnp.exp(logits - jnp.max(logits, axis=0, keepdims=True))
    return jnp.sum(e[:layer + 1], axis=0, keepdims=True) / jnp.sum(e, axis=0, keepdims=True)


def _ln_proj_kernel(x_ref, g_ref, b_ref, lbf_ref, lbb_ref, wzf_ref, wzb_ref, wq_ref, wi_ref, wg_ref, wf_ref,
                    lf_ref, kk_ref, q_ref, f_ref, h_scr, *, layer):
    @pl.when(pl.program_id(1) == 0)
    def _():
        for r in range(0, x_ref.shape[0], LN_ROWS):
            rows = slice(r, r + LN_ROWS)
            h_scr[rows, :] = _layernorm(x_ref[rows, :], g_ref[...], b_ref[...]).astype(BF16)

    h = h_scr[...]
    for k, (w_ref, lb_ref) in enumerate(((wzf_ref, lbf_ref), (wzb_ref, lbb_ref))):
        lb = _lower_bound(lb_ref, layer)
        f = lb + (1.0 - lb) * _sigmoid(_dot(h, w_ref[...]))
        lf = jnp.log2(f)
        hi = lf.astype(BF16)
        lf_ref[k, 0] = hi
        lf_ref[k, 1] = (lf - hi.astype(F32)).astype(BF16)
        kk_ref[k] = (1.0 - f).astype(BF16)
    q_ref[0] = _silu(_dot(h, wq_ref[...])).astype(BF16)
    q_ref[2] = _silu(_dot(h, wg_ref[...])).astype(BF16)
    q_ref[1] = _dot(h, wi_ref[...]).astype(BF16)
    f_ref[...] = _dot(h, wf_ref[...]).astype(BF16)


def _ln_proj(x, g, b, lbf_logits, lbb_logits, w, *, layer, tm=1024):
    t, d = x.shape
    nj = LN_PROJ_STEPS
    sw = w.shape[1] // (6 * nj)
    slots = lbf_logits.shape[0]
    wspec = lambda section: pl.BlockSpec((d, sw), lambda i, j: (0, section * nj + j))
    pipelined = _nbytes((tm, d), F32) + _nbytes((d, 6 * sw), BF16) + _nbytes((tm, 10 * sw), BF16)
    resident = _nbytes((tm, d), BF16) + _nbytes((tm, 6 * sw), F32)
    return pl.pallas_call(
        functools.partial(_ln_proj_kernel, layer=layer),
        grid=(t // tm, nj),
        in_specs=[
            pl.BlockSpec((tm, d), lambda i, j: (i, 0)),
            pl.BlockSpec((1, d), lambda i, j: (0, 0)),
            pl.BlockSpec((1, d), lambda i, j: (0, 0)),
            pl.BlockSpec((slots, sw), lambda i, j: (0, j)),
            pl.BlockSpec((slots, sw), lambda i, j: (0, j)),
            wspec(2), wspec(3), wspec(0), wspec(1), wspec(4), wspec(5),
        ],
        out_specs=[
            pl.BlockSpec((None, 2, 2, tm, sw), lambda i, j: (j, 0, 0, i, 0)),
            pl.BlockSpec((None, 2, tm, sw), lambda i, j: (j, 0, i, 0)),
            pl.BlockSpec((None, 3, tm, sw), lambda i, j: (j, 0, i, 0)),
            pl.BlockSpec((tm, sw), lambda i, j: (i, j)),
        ],
        out_shape=[
            jax.ShapeDtypeStruct((nj, 2, 2, t, sw), BF16),
            jax.ShapeDtypeStruct((nj, 2, t, sw), BF16),
            jax.ShapeDtypeStruct((nj, 3, t, sw), BF16),
            jax.ShapeDtypeStruct((t, nj * sw), BF16),
        ],
        scratch_shapes=[pltpu.VMEM((tm, d), BF16)],
        compiler_params=pltpu.CompilerParams(
            dimension_semantics=("parallel", "arbitrary"),
            vmem_limit_bytes=_vmem_limit(pipelined, resident)),
        name="ln_proj",
    )(x, g, b, lbf_logits, lbb_logits, w, w, w, w, w, w)


def _hgrn_kernel(gn_ref, q_ref, v_ref, g_ref, hi_ref, lo_ref, kk_ref, o_ref, st_ref, of_ref,
                 *, nblk, blk_rows):
    c = HGRN_CHUNK
    d = HEAD_DIM
    half = c // 2
    nchunk = blk_rows // c
    heads = st_ref.shape[0]
    phase = pl.program_id(1)
    t = pl.program_id(2)

    @pl.when(t == 0)
    def _():
        st_ref[...] = jnp.zeros_like(st_ref)

    def scan_block(fwd):
        row = lax.broadcasted_iota(jnp.int32, (c, c), 0)
        col = lax.broadcasted_iota(jnp.int32, (c, c), 1)
        tri = (col <= row) if fwd else (col >= row)
        tri2 = jnp.concatenate([tri.astype(BF16)] * 2, axis=1)
        blk = t if fwd else nblk - 1 - t
        chunk_rows = [pl.ds((ci if fwd else nchunk - 1 - ci) * c, c) for ci in range(nchunk)]
        head_cols = [slice(h * d, (h + 1) * d) for h in range(heads)]
        trans_b = (((1,), (1,)), ((), ()))
        trans_a = (((0,), (0,)), ((), ()))

        bs = [_dot(tri2, jnp.concatenate([hi_ref[rows, :], lo_ref[rows, :]], axis=0)) for rows in chunk_rows]

        qds, dsts, scs, decs, mids = [], [], [], [], []
        for rows, b in zip(chunk_rows, bs):
            b_mid = b[half - 1:half, :] if fwd else b[half:half + 1, :]
            b_end = b[c - 1:c, :] if fwd else b[0:1, :]
            kk = kk_ref[rows, :].astype(F32)
            qd = (q_ref[rows, :].astype(F32) * jnp.exp2(b - b_mid)).astype(BF16)
            ki = (kk * jnp.exp2(b_mid - b)).astype(BF16)
            ke = (kk * jnp.exp2(b_end - b)).astype(BF16)
            v = v_ref[rows, :]
            qds.append(qd)
            scs.append([lax.dot_general(qd[:, hs], ki[:, hs], trans_b, preferred_element_type=F32)
                        for hs in head_cols])
            dsts.append([lax.dot_general(v[:, hs], ke[:, hs], trans_a, preferred_element_type=F32)
                         for hs in head_cols])
            decs.append(jnp.exp2(b_end))
            mids.append(jnp.exp2(b_mid))

        s_mids = [[] for _ in chunk_rows]
        for h, hs in enumerate(head_cols):
            st = st_ref[h]
            for ci in range(nchunk):
                s_mids[ci].append((st * mids[ci][:, hs]).astype(BF16))
                st = decs[ci][:, hs] * st + dsts[ci][h]
            st_ref[h] = st

        for ci, rows in enumerate(chunk_rows):
            v = v_ref[rows, :]
            outs = []
            for h, hs in enumerate(head_cols):
                scores = jnp.where(tri, scs[ci][h], 0.0).astype(BF16)
                o = (_dot(scores, v[:, hs])
                     + lax.dot_general(qds[ci][:, hs], s_mids[ci][h], trans_b, preferred_element_type=F32))
                if not fwd:
                    o = o + of_ref[pl.ds(pl.multiple_of(blk * blk_rows, c) + rows.start, c), hs]
                    o = o * lax.rsqrt(jnp.mean(o * o, axis=-1, keepdims=True) + RMS_EPS) * gn_ref[...]
                outs.append(o)
            o = jnp.concatenate(outs, axis=1)
            if fwd:
                of_ref[pl.ds(pl.multiple_of(blk * blk_rows, c) + rows.start, c), :] = o
            else:
                o_ref[rows, :] = (o * g_ref[rows, :].astype(F32)).astype(BF16)

    pl.when(phase == 0)(functools.partial(scan_block, True))
    pl.when(phase == 1)(functools.partial(scan_block, False))


def _hgrn(g_norm, qig, log2f, one_minus_f, *, blk_rows=2048):
    nj, _, t, sw = qig.shape
    heads = sw // HEAD_DIM
    nblk = t // blk_rows
    d = HEAD_DIM

    tblk = lambda p, i: i + p * (nblk - 1 - 2 * i)
    parked = lambda p, i: nblk - 1 - p * i
    qspec = lambda sec: pl.BlockSpec((None, None, blk_rows, sw), lambda j, p, i: (j, sec, tblk(p, i), 0))
    fspec = lambda part: pl.BlockSpec((None, None, None, blk_rows, sw),
                                      lambda j, p, i: (j, p, part, tblk(p, i), 0))

    pipelined = 6 * _nbytes((blk_rows, sw), BF16)
    resident = _nbytes((t, sw), F32) + 128 * _nbytes((d, d), F32)
    return pl.pallas_call(
        functools.partial(_hgrn_kernel, nblk=nblk, blk_rows=blk_rows),
        grid=(nj, 2, nblk),
        in_specs=[
            pl.BlockSpec((1, d), lambda j, p, i: (0, 0)),
            qspec(0), qspec(1),
            pl.BlockSpec((None, None, blk_rows, sw), lambda j, p, i: (j, 2, parked(p, i), 0)),
            fspec(0), fspec(1),
            pl.BlockSpec((None, None, blk_rows, sw), lambda j, p, i: (j, p, tblk(p, i), 0)),
        ],
        out_specs=pl.BlockSpec((None, blk_rows, sw), lambda j, p, i: (j, parked(p, i), 0)),
        out_shape=jax.ShapeDtypeStruct((nj, t, sw), BF16),
        scratch_shapes=[pltpu.VMEM((heads, d, d), F32), pltpu.VMEM((t, sw), F32)],
        compiler_params=pltpu.CompilerParams(
            dimension_semantics=("parallel", "arbitrary", "arbitrary"),
            vmem_limit_bytes=_vmem_limit(pipelined, resident)),
        name="hgrn",
    )(g_norm, qig, qig, qig, log2f, log2f, one_minus_f)


def _dft_constants(t, n1):
    n2 = t // n1
    k1 = np.arange(n1)[None, :, None]
    t1 = np.arange(n1)[None, None, :]
    t2 = np.arange(n2)[:, None, None]
    ang = 2.0 * np.pi * ((k1 * (n2 * t1 + t2)) % t) / t
    g = np.concatenate([np.cos(ang), -np.sin(ang)], axis=1) / math.sqrt(n1)
    k2 = np.arange(n2)[:, None]
    s2 = np.arange(n2)[None, :]
    ang2 = 2.0 * np.pi * ((k2 * s2) % n2) / n2
    c2, sn2 = np.cos(ang2) / math.sqrt(n2), np.sin(ang2) / math.sqrt(n2)
    f2 = np.block([[c2, sn2], [-sn2, c2]])
    m = np.arange(FOURIER_GDIM)
    ang3 = 2.0 * np.pi * ((m[:, None] * m[None, :]) % FOURIER_GDIM) / FOURIER_GDIM
    c3 = np.cos(ang3) / math.sqrt(FOURIER_GDIM)
    s3 = np.sin(ang3) / math.sqrt(FOURIER_GDIM)
    as_bf16 = lambda a: jnp.asarray(a, dtype=F32).astype(BF16)
    return as_bf16(g), as_bf16(f2), as_bf16(c3), as_bf16(s3)


def _fft_t1_kernel(u_ref, g_ref, w_ref, *, tb, n1):
    x = pltpu.einshape("tjm->(jt)m", u_ref[...].astype(F32)).astype(BF16)
    r = jnp.stack([_dot(g_ref[j], x[j * n1:(j + 1) * n1]) for j in range(tb)])
    r = pltpu.einshape("jam->ajm", r).astype(BF16)
    w_ref[0] = r[:n1]
    w_ref[1] = r[n1:]


def _fft_t1(four, g, *, tb=BF16_ROWS):
    t, m = four.shape
    n2, _, n1 = g.shape
    u = four.reshape(n1, n2, m)
    pipelined = (_nbytes((n1, tb, m), BF16) + _nbytes((tb, 2 * n1, n1), BF16)
                 + _nbytes((2, n1, tb, m), BF16))
    return pl.pallas_call(
        functools.partial(_fft_t1_kernel, tb=tb, n1=n1),
        grid=(n2 // tb,),
        in_specs=[
            pl.BlockSpec((n1, tb, m), lambda s: (0, s, 0)),
            pl.BlockSpec((tb, 2 * n1, n1), lambda s: (s, 0, 0)),
        ],
        out_specs=pl.BlockSpec((2, n1, tb, m), lambda s: (0, 0, s, 0)),
        out_shape=jax.ShapeDtypeStruct((2, n1, n2, m), BF16),
        compiler_params=pltpu.CompilerParams(
            dimension_semantics=("parallel",),
            vmem_limit_bytes=_vmem_limit(pipelined, 6 * _nbytes((tb * n1, m), F32))),
        name="fft_t1",
    )(u, g)


def _fft_t2_kernel(f_ref, w_ref, p_ref, *, kb, n2):
    for j in range(kb):
        rhs = jnp.concatenate([w_ref[0, j], w_ref[1, j]], axis=0)
        res = _dot(f_ref[...], rhs)
        p_ref[0, j] = res[:n2].astype(BF16)
        p_ref[1, j] = res[n2:].astype(BF16)


def _fft_t2(w4, f2, *, kb=4):
    _, n1, n2, m = w4.shape
    blk = (2, kb, n2, m)
    pipelined = 2 * _nbytes(blk, BF16) + _nbytes((2 * n2, 2 * n2), BF16)
    return pl.pallas_call(
        functools.partial(_fft_t2_kernel, kb=kb, n2=n2),
        grid=(n1 // kb,),
        in_specs=[
            pl.BlockSpec((2 * n2, 2 * n2), lambda s: (0, 0)),
            pl.BlockSpec(blk, lambda s: (0, s, 0, 0)),
        ],
        out_specs=pl.BlockSpec(blk, lambda s: (0, s, 0, 0)),
        out_shape=jax.ShapeDtypeStruct((2, n1, n2, m), BF16),
        compiler_params=pltpu.CompilerParams(
            dimension_semantics=("parallel",),
            vmem_limit_bytes=_vmem_limit(pipelined, 4 * _nbytes((2 * n2, m), F32))),
        name="fft_t2",
    )(f2, w4)


def _mix_ln_kernel(x_ref, ge_ref, be_ref, oh_ref, pr_ref, pi_ref, c3_ref, s3_ref, wo_ref,
                   g1_ref, b1_ref, h1_ref, pr_scr, pi_scr, *, alpha):
    gd = FOURIER_GDIM
    o_hgrn = jnp.concatenate([oh_ref[j] for j in range(oh_ref.shape[0])], axis=1)
    wh = o_hgrn.shape[1]
    mix = _dot(o_hgrn, wo_ref[:wh, :])
    pr_scr[...] = pltpu.einshape("akm->(ka)m", pr_ref[...].astype(F32)).astype(BF16)
    pi_scr[...] = pltpu.einshape("akm->(ka)m", pi_ref[...].astype(F32)).astype(BF16)
    c3, s3 = c3_ref[...], s3_ref[...]
    parts = []
    for g in range(pr_scr.shape[1] // gd):
        sl = slice(g * gd, (g + 1) * gd)
        parts.append((_dot(pr_scr[:, sl], c3) + _dot(pi_scr[:, sl], s3)).astype(BF16))
    mix = mix + _dot(jnp.concatenate(parts, axis=1), wo_ref[wh:, :])
    h0 = _layernorm(x_ref[...], ge_ref[...], be_ref[...])
    h1_ref[...] = _layernorm(alpha * h0 + mix, g1_ref[...], b1_ref[...])


def _mix_ln(x, ge, be, o_hgrn, p, c3, s3, wo, g1, b1, *, alpha, tm=512):
    t, d = x.shape
    nj, _, sw = o_hgrn.shape
    wh = nj * sw
    _, n1, n2, m = p.shape
    kr = tm // n1
    gd = FOURIER_GDIM
    const = lambda shape: pl.BlockSpec(shape, lambda i: tuple(0 for _ in shape),
                                       pipeline_mode=pl.Buffered(1))
    pipelined = (_nbytes((tm, d), F32) + _nbytes((tm, wh), BF16) + 2 * _nbytes((n1, kr, m), BF16)
                 + _nbytes((tm, d), F32))
    resident = (_nbytes((wh + m, d), BF16) + 2 * _nbytes((tm, m), BF16) + 2 * _nbytes((tm, m), F32)
                + 3 * _nbytes((tm, d), F32))
    return pl.pallas_call(
        functools.partial(_mix_ln_kernel, alpha=alpha),
        grid=(t // tm,),
        in_specs=[
            pl.BlockSpec((tm, d), lambda i: (i, 0)),
            const((1, d)), const((1, d)),
            pl.BlockSpec((nj, tm, sw), lambda i: (0, i, 0)),
            pl.BlockSpec((None, n1, kr, m), lambda i: (0, 0, i, 0)),
            pl.BlockSpec((None, n1, kr, m), lambda i: (1, 0, i, 0)),
            const((gd, gd)), const((gd, gd)),
            const((wh + m, d)),
            const((1, d)), const((1, d)),
        ],
        out_specs=pl.BlockSpec((tm, d), lambda i: (i, 0)),
        out_shape=jax.ShapeDtypeStruct((t, d), F32),
        scratch_shapes=[pltpu.VMEM((tm, m), BF16), pltpu.VMEM((tm, m), BF16)],
        compiler_params=pltpu.CompilerParams(
            dimension_semantics=("parallel",),
            vmem_limit_bytes=_vmem_limit(pipelined, resident)),
        name="mix_ln",
    )(x, ge, be, o_hgrn, p, p, c3, s3, wo, g1, b1)


def _ffn_ln_kernel(h1_ref, wg_ref, wu_ref, wd_ref, g2_ref, b2_ref, o_ref, hb_scr, *, alpha):
    f = pl.program_id(1)

    @pl.when(f == 0)
    def _():
        hb_scr[...] = h1_ref[...].astype(BF16)
        o_ref[...] = jnp.zeros_like(o_ref)

    hb = hb_scr[...]
    gate = _dot(hb, wg_ref[...].astype(BF16))
    up = _dot(hb, wu_ref[...].astype(BF16))
    act = (gate * _sigmoid(gate) * up).astype(BF16)
    cw = FFN_OUT_SLAB
    for n in range(o_ref.shape[1] // cw):
        cols = slice(n * cw, (n + 1) * cw)
        o_ref[:, cols] += _dot(act, wd_ref[:, cols].astype(BF16))

    @pl.when(f == pl.num_programs(1) - 1)
    def _():
        for r in range(0, o_ref.shape[0], LN_ROWS):
            rows = slice(r, r + LN_ROWS)
            o_ref[rows, :] = _layernorm(alpha * h1_ref[rows, :] + o_ref[rows, :], g2_ref[...], b2_ref[...])


def _ffn_ln(h1, wg, wu, wd, g2, b2, *, alpha, tm=1024, tf=256):
    t, d = h1.shape
    dff = wg.shape[1]
    pipelined = 2 * _nbytes((tm, d), F32) + 3 * _nbytes((d, tf), F32)
    resident = _nbytes((tm, d), BF16) + 2 * _nbytes((tm, tf), F32) + 2 * _nbytes((tm, FFN_OUT_SLAB), F32)
    return pl.pallas_call(
        functools.partial(_ffn_ln_kernel, alpha=alpha),
        grid=(t // tm, dff // tf),
        in_specs=[
            pl.BlockSpec((tm, d), lambda i, f: (i, 0)),
            pl.BlockSpec((d, tf), lambda i, f: (0, f)),
            pl.BlockSpec((d, tf), lambda i, f: (0, f)),
            pl.BlockSpec((tf, d), lambda i, f: (f, 0)),
            pl.BlockSpec((1, d), lambda i, f: (0, 0)),
            pl.BlockSpec((1, d), lambda i, f: (0, 0)),
        ],
        out_specs=pl.BlockSpec((tm, d), lambda i, f: (i, 0)),
        out_shape=jax.ShapeDtypeStruct((t, d), F32),
        scratch_shapes=[pltpu.VMEM((tm, d), BF16)],
        compiler_params=pltpu.CompilerParams(
            dimension_semantics=("parallel", "arbitrary"),
            vmem_limit_bytes=_vmem_limit(pipelined, resident)),
        name="ffn_ln",
    )(h1, wg, wu, wd, g2, b2)


def kernel(x, ln_emb_g, ln_emb_b, w_in, lb_fwd_logits, lb_bwd_logits, g_norm, w_out, ln1_g, ln1_b,
           w_gate, w_up, w_down, ln2_g, ln2_b):
    depth = w_in.shape[0]
    assert depth == 1, "the embedding LayerNorm is fused into the single layer's projection"
    batch, seq, d = x.shape
    assert batch == 1
    w = lb_fwd_logits.shape[1]
    alpha = (2.0 * depth) ** 0.25
    layer = 0
    row = lambda a: a.reshape(1, -1).astype(F32)

    x2 = x.reshape(seq, d)
    log2f, one_minus_f, qig, four = _ln_proj(
        x2, row(ln_emb_g), row(ln_emb_b), lb_fwd_logits.astype(F32), lb_bwd_logits.astype(F32),
        w_in[layer].astype(BF16), layer=layer)

    o_hgrn = _hgrn(row(g_norm[layer]), qig, log2f, one_minus_f)

    g1c, f2c, c3, s3 = _dft_constants(seq, FFT_N1)
    p = _fft_t2(_fft_t1(four, g1c), f2c)

    wo = w_out[layer].astype(BF16)
    h1 = _mix_ln(x2, row(ln_emb_g), row(ln_emb_b), o_hgrn, p, c3, s3, wo,
                 row(ln1_g[layer]), row(ln1_b[layer]), alpha=alpha)

    out = _ffn_ln(h1, w_gate[layer], w_up[layer], w_down[layer],
                  row(ln2_g[layer]), row(ln2_b[layer]), alpha=alpha)
    return out.reshape(batch, seq, d)
```

```python
import functools
import math

import numpy as np
import jax
import jax.numpy as jnp
from jax import lax
from jax.experimental import pallas as pl
from jax.experimental.pallas import tpu as pltpu

LN_EPS = 1e-5
RMS_EPS = 1e-6
HEAD_DIM = 128
FOURIER_GDIM = 256
FFT_N1 = 32
BF16_ROWS = 16
LN_ROWS = 256
FFN_OUT_SLAB = 512
HGRN_CHUNK = 128
LN_PROJ_STEPS = 4
V7X_VMEM_CAP = 60 * 1024 * 1024

BF16 = jnp.bfloat16
F32 = jnp.float32


def _vmem_limit(pipelined_bytes, resident_bytes):
    return int(min(V7X_VMEM_CAP, 2 * pipelined_bytes + resident_bytes + (4 << 20)))


def _nbytes(shape, dtype):
    return int(np.prod(shape)) * jnp.dtype(dtype).itemsize


def _dot(a, b):
    return jnp.dot(a, b, preferred_element_type=F32)


def _layernorm(x, g, b):
    mu = jnp.mean(x, axis=-1, keepdims=True)
    xc = x - mu
    var = jnp.mean(xc * xc, axis=-1, keepdims=True)
    return xc * lax.rsqrt(var + LN_EPS) * g + b


def _sigmoid(x):
    return 1.0 / (1.0 + jnp.exp(-x))


def _silu(x):
    return x * _sigmoid(x)


def _lower_bound(logits_ref, layer):
    logits = logits_ref[...]
    e = jnp.exp(logits - jnp.max(logits, axis=0, keepdims=True))
    return jnp.sum(e[:layer + 1], axis=0, keepdims=True) / jnp.sum(e, axis=0, keepdims=True)


def _ln_proj_kernel(x_ref, g_ref, b_ref, lbf_ref, lbb_ref, wzf_ref, wzb_ref, wq_ref, wi_ref, wg_ref, wf_ref,
                    lf_ref, kk_ref, q_ref, f_ref, h_scr, *, layer):
    @pl.when(pl.program_id(1) == 0)
    def _():
        for r in range(0, x_ref.shape[0], LN_ROWS):
            rows = slice(r, r + LN_ROWS)
            h_scr[rows, :] = _layernorm(x_ref[rows, :], g_ref[...], b_ref[...]).astype(BF16)

    h = h_scr[...]
    sw = wf_ref.shape[1]
    for k, (w_ref, lb_ref) in enumerate(((wzf_ref, lbf_ref), (wzb_ref, lbb_ref))):
        lb = _lower_bound(lb_ref, layer)
        f = lb + (1.0 - lb) * _sigmoid(_dot(h, w_ref[...]))
        lf_ref[:, k * sw:(k + 1) * sw] = jnp.log2(f)
        kk_ref[:, k * sw:(k + 1) * sw] = (1.0 - f).astype(BF16)
    q_ref[:, 0:sw] = _silu(_dot(h, wq_ref[...])).astype(BF16)
    q_ref[:, sw:2 * sw] = _dot(h, wi_ref[...]).astype(BF16)
    q_ref[:, 2 * sw:3 * sw] = _silu(_dot(h, wg_ref[...])).astype(BF16)
    f_ref[...] = _dot(h, wf_ref[...]).astype(BF16)


def _ln_proj(x, g, b, lbf_logits, lbb_logits, w, *, layer, tm=1024):
    t, d = x.shape
    nj = LN_PROJ_STEPS
    sw = w.shape[1] // (6 * nj)
    slots = lbf_logits.shape[0]
    wspec = lambda section: pl.BlockSpec((d, sw), lambda i, j: (0, section * nj + j))
    pipelined = (_nbytes((tm, d), F32) + _nbytes((d, 6 * sw), BF16)
                 + _nbytes((tm, 2 * sw), F32) + _nbytes((tm, 6 * sw), BF16))
    resident = _nbytes((tm, d), BF16) + _nbytes((tm, 6 * sw), F32)
    return pl.pallas_call(
        functools.partial(_ln_proj_kernel, layer=layer),
        grid=(t // tm, nj),
        in_specs=[
            pl.BlockSpec((tm, d), lambda i, j: (i, 0)),
            pl.BlockSpec((1, d), lambda i, j: (0, 0)),
            pl.BlockSpec((1, d), lambda i, j: (0, 0)),
            pl.BlockSpec((slots, sw), lambda i, j: (0, j)),
            pl.BlockSpec((slots, sw), lambda i, j: (0, j)),
            wspec(2), wspec(3), wspec(0), wspec(1), wspec(4), wspec(5),
        ],
        out_specs=[
            pl.BlockSpec((tm, 2 * sw), lambda i, j: (i, j)),
            pl.BlockSpec((tm, 2 * sw), lambda i, j: (i, j)),
            pl.BlockSpec((tm, 3 * sw), lambda i, j: (i, j)),
            pl.BlockSpec((tm, sw), lambda i, j: (i, j)),
        ],
        out_shape=[
            jax.ShapeDtypeStruct((t, nj * 2 * sw), F32),
            jax.ShapeDtypeStruct((t, nj * 2 * sw), BF16),
            jax.ShapeDtypeStruct((t, nj * 3 * sw), BF16),
            jax.ShapeDtypeStruct((t, nj * sw), BF16),
        ],
        scratch_shapes=[pltpu.VMEM((tm, d), BF16)],
        compiler_params=pltpu.CompilerParams(
            dimension_semantics=("parallel", "arbitrary"),
            vmem_limit_bytes=_vmem_limit(pipelined, resident)),
        name="ln_proj",
    )(x, g, b, lbf_logits, lbb_logits, w, w, w, w, w, w)


def _hgrn_kernel(gn_ref, q_ref, v_ref, g_ref, lf_ref, kk_ref, o_ref, st_ref, of_ref, *, nblk, blk_rows):
    c = HGRN_CHUNK
    d = HEAD_DIM
    half = c // 2
    nchunk = blk_rows // c
    heads = st_ref.shape[0]
    phase = pl.program_id(1)
    t = pl.program_id(2)

    @pl.when(t == 0)
    def _():
        st_ref[...] = jnp.zeros_like(st_ref)

    def scan_block(fwd):
        row = lax.broadcasted_iota(jnp.int32, (c, c), 0)
        col = lax.broadcasted_iota(jnp.int32, (c, c), 1)
        tri = (col <= row) if fwd else (col >= row)
        tri2 = jnp.concatenate([tri.astype(BF16)] * 2, axis=1)
        blk = t if fwd else nblk - 1 - t
        chunk_rows = [pl.ds((ci if fwd else nchunk - 1 - ci) * c, c) for ci in range(nchunk)]
        head_cols = [slice(h * d, (h + 1) * d) for h in range(heads)]
        trans_b = (((1,), (1,)), ((), ()))
        trans_a = (((0,), (0,)), ((), ()))

        bs = []
        for rows in chunk_rows:
            lf = lf_ref[rows, :]
            hi = lf.astype(BF16)
            lo = (lf - hi.astype(F32)).astype(BF16)
            bs.append(_dot(tri2, jnp.concatenate([hi, lo], axis=0)))

        qds, dsts, scs, decs, mids = [], [], [], [], []
        for rows, b in zip(chunk_rows, bs):
            b_mid = b[half - 1:half, :] if fwd else b[half:half + 1, :]
            b_end = b[c - 1:c, :] if fwd else b[0:1, :]
            kk = kk_ref[rows, :].astype(F32)
            qd = (q_ref[rows, :].astype(F32) * jnp.exp2(b - b_mid)).astype(BF16)
            ki = (kk * jnp.exp2(b_mid - b)).astype(BF16)
            ke = (kk * jnp.exp2(b_end - b)).astype(BF16)
            v = v_ref[rows, :]
            qds.append(qd)
            scs.append([lax.dot_general(qd[:, hs], ki[:, hs], trans_b, preferred_element_type=F32)
                        for hs in head_cols])
            dsts.append([lax.dot_general(v[:, hs], ke[:, hs], trans_a, preferred_element_type=F32)
                         for hs in head_cols])
            decs.append(jnp.exp2(b_end))
            mids.append(jnp.exp2(b_mid))

        s_mids = [[] for _ in chunk_rows]
        for h, hs in enumerate(head_cols):
            st = st_ref[h]
            for ci in range(nchunk):
                s_mids[ci].append((st * mids[ci][:, hs]).astype(BF16))
                st = decs[ci][:, hs] * st + dsts[ci][h]
            st_ref[h] = st

        for ci, rows in enumerate(chunk_rows):
            v = v_ref[rows, :]
            outs = []
            for h, hs in enumerate(head_cols):
                scores = jnp.where(tri, scs[ci][h], 0.0).astype(BF16)
                o = (_dot(scores, v[:, hs])
                     + lax.dot_general(qds[ci][:, hs], s_mids[ci][h], trans_b, preferred_element_type=F32))
                if not fwd:
                    o = o + of_ref[pl.ds(pl.multiple_of(blk * blk_rows, c) + rows.start, c), hs]
                    o = o * lax.rsqrt(jnp.mean(o * o, axis=-1, keepdims=True) + RMS_EPS) * gn_ref[...]
                outs.append(o)
            o = jnp.concatenate(outs, axis=1)
            if fwd:
                of_ref[pl.ds(pl.multiple_of(blk * blk_rows, c) + rows.start, c), :] = o
            else:
                o_ref[rows, :] = (o * g_ref[rows, :].astype(F32)).astype(BF16)

    pl.when(phase == 0)(functools.partial(scan_block, True))
    pl.when(phase == 1)(functools.partial(scan_block, False))


def _hgrn(g_norm, qig, log2f, one_minus_f, *, blk_rows=4096):
    t = log2f.shape[0]
    nj = LN_PROJ_STEPS
    sw = log2f.shape[1] // (2 * nj)
    heads = sw // HEAD_DIM
    nblk = t // blk_rows
    d = HEAD_DIM

    tblk = lambda p, i: i + p * (nblk - 1 - 2 * i)
    qspec = lambda sec: pl.BlockSpec((blk_rows, sw), lambda j, p, i: (tblk(p, i), 3 * j + sec))
    fspec = pl.BlockSpec((blk_rows, sw), lambda j, p, i: (tblk(p, i), 2 * j + p))

    pipelined = 5 * _nbytes((blk_rows, sw), BF16) + _nbytes((blk_rows, sw), F32)
    resident = _nbytes((t, sw), F32) + 128 * _nbytes((d, d), F32)
    return pl.pallas_call(
        functools.partial(_hgrn_kernel, nblk=nblk, blk_rows=blk_rows),
        grid=(nj, 2, nblk),
        in_specs=[
            pl.BlockSpec((1, d), lambda j, p, i: (0, 0)),
            qspec(0), qspec(1),
            pl.BlockSpec((blk_rows, sw), lambda j, p, i: (nblk - 1 - p * i, 3 * j + 2)),
            fspec, fspec,
        ],
        out_specs=pl.BlockSpec((blk_rows, sw), lambda j, p, i: (nblk - 1 - p * i, j)),
        out_shape=jax.ShapeDtypeStruct((t, nj * sw), BF16),
        scratch_shapes=[pltpu.VMEM((heads, d, d), F32), pltpu.VMEM((t, sw), F32)],
        compiler_params=pltpu.CompilerParams(
            dimension_semantics=("parallel", "arbitrary", "arbitrary"),
            vmem_limit_bytes=_vmem_limit(pipelined, resident)),
        name="hgrn",
    )(g_norm, qig, qig, qig, log2f, one_minus_f)


def _dft_constants(t, n1):
    n2 = t // n1
    k1 = np.arange(n1)[None, :, None]
    t1 = np.arange(n1)[None, None, :]
    t2 = np.arange(n2)[:, None, None]
    ang = 2.0 * np.pi * ((k1 * (n2 * t1 + t2)) % t) / t
    g = np.concatenate([np.cos(ang), -np.sin(ang)], axis=1) / math.sqrt(n1)
    k2 = np.arange(n2)[:, None]
    s2 = np.arange(n2)[None, :]
    ang2 = 2.0 * np.pi * ((k2 * s2) % n2) / n2
    c2, sn2 = np.cos(ang2) / math.sqrt(n2), np.sin(ang2) / math.sqrt(n2)
    f2 = np.block([[c2, sn2], [-sn2, c2]])
    m = np.arange(FOURIER_GDIM)
    ang3 = 2.0 * np.pi * ((m[:, None] * m[None, :]) % FOURIER_GDIM) / FOURIER_GDIM
    c3 = np.cos(ang3) / math.sqrt(FOURIER_GDIM)
    s3 = np.sin(ang3) / math.sqrt(FOURIER_GDIM)
    as_bf16 = lambda a: jnp.asarray(a, dtype=F32).astype(BF16)
    return as_bf16(g), as_bf16(f2), as_bf16(c3), as_bf16(s3)


def _fft_t1_kernel(u_ref, g_ref, w_ref, *, tb, n1):
    x = pltpu.einshape("tjm->(jt)m", u_ref[...])
    r = jnp.stack([_dot(g_ref[j], x[j * n1:(j + 1) * n1]).astype(BF16) for j in range(tb)])
    r = pltpu.einshape("jam->ajm", r)
    w_ref[0] = r[:n1]
    w_ref[1] = r[n1:]


def _fft_t1(four, g, *, tb=BF16_ROWS):
    t, m = four.shape
    n2, _, n1 = g.shape
    u = four.reshape(n1, n2, m)
    pipelined = (_nbytes((n1, tb, m), BF16) + _nbytes((tb, 2 * n1, n1), BF16)
                 + _nbytes((2, n1, tb, m), BF16))
    return pl.pallas_call(
        functools.partial(_fft_t1_kernel, tb=tb, n1=n1),
        grid=(n2 // tb,),
        in_specs=[
            pl.BlockSpec((n1, tb, m), lambda s: (0, s, 0)),
            pl.BlockSpec((tb, 2 * n1, n1), lambda s: (s, 0, 0)),
        ],
        out_specs=pl.BlockSpec((2, n1, tb, m), lambda s: (0, 0, s, 0)),
        out_shape=jax.ShapeDtypeStruct((2, n1, n2, m), BF16),
        compiler_params=pltpu.CompilerParams(
            dimension_semantics=("parallel",),
            vmem_limit_bytes=_vmem_limit(pipelined, 6 * _nbytes((tb * n1, m), F32))),
        name="fft_t1",
    )(u, g)


def _fft_t2_kernel(f_ref, w_ref, p_ref, *, kb, n2):
    for j in range(kb):
        rhs = jnp.concatenate([w_ref[0, j], w_ref[1, j]], axis=0)
        res = _dot(f_ref[...], rhs)
        p_ref[0, j] = res[:n2].astype(BF16)
        p_ref[1, j] = res[n2:].astype(BF16)


def _fft_t2(w4, f2, *, kb=4):
    _, n1, n2, m = w4.shape
    blk = (2, kb, n2, m)
    pipelined = 2 * _nbytes(blk, BF16) + _nbytes((2 * n2, 2 * n2), BF16)
    return pl.pallas_call(
        functools.partial(_fft_t2_kernel, kb=kb, n2=n2),
        grid=(n1 // kb,),
        in_specs=[
            pl.BlockSpec((2 * n2, 2 * n2), lambda s: (0, 0)),
            pl.BlockSpec(blk, lambda s: (0, s, 0, 0)),
        ],
        out_specs=pl.BlockSpec(blk, lambda s: (0, s, 0, 0)),
        out_shape=jax.ShapeDtypeStruct((2, n1, n2, m), BF16),
        compiler_params=pltpu.CompilerParams(
            dimension_semantics=("parallel",),
            vmem_limit_bytes=_vmem_limit(pipelined, 4 * _nbytes((2 * n2, m), F32))),
        name="fft_t2",
    )(f2, w4)


def _mix_ln_kernel(x_ref, ge_ref, be_ref, oh_ref, pr_ref, pi_ref, c3_ref, s3_ref, wo_ref,
                   g1_ref, b1_ref, h1_ref, pr_scr, pi_scr, *, alpha):
    gd = FOURIER_GDIM
    pr_scr[...] = pltpu.einshape("akm->(ka)m", pr_ref[...])
    pi_scr[...] = pltpu.einshape("akm->(ka)m", pi_ref[...])
    c3, s3 = c3_ref[...], s3_ref[...]
    parts = []
    for g in range(pr_scr.shape[1] // gd):
        sl = slice(g * gd, (g + 1) * gd)
        parts.append((_dot(pr_scr[:, sl], c3) + _dot(pi_scr[:, sl], s3)).astype(BF16))
    o_four = jnp.concatenate(parts, axis=1)
    wh = oh_ref.shape[1]
    mix = _dot(oh_ref[...], wo_ref[:wh, :]) + _dot(o_four, wo_ref[wh:, :])
    h0 = _layernorm(x_ref[...], ge_ref[...], be_ref[...])
    h1_ref[...] = _layernorm(alpha * h0 + mix, g1_ref[...], b1_ref[...])


def _mix_ln(x, ge, be, o_hgrn, p, c3, s3, wo, g1, b1, *, alpha, tm=512):
    t, d = x.shape
    wh = o_hgrn.shape[1]
    _, n1, n2, m = p.shape
    kr = tm // n1
    gd = FOURIER_GDIM
    const = lambda shape: pl.BlockSpec(shape, lambda i: tuple(0 for _ in shape),
                                       pipeline_mode=pl.Buffered(1))
    pipelined = (_nbytes((tm, d), F32) + _nbytes((tm, wh), BF16) + 2 * _nbytes((n1, kr, m), BF16)
                 + _nbytes((tm, d), F32))
    resident = (_nbytes((wh + m, d), BF16) + 2 * _nbytes((tm, m), BF16) + 2 * _nbytes((tm, m), F32)
                + 3 * _nbytes((tm, d), F32))
    return pl.pallas_call(
        functools.partial(_mix_ln_kernel, alpha=alpha),
        grid=(t // tm,),
        in_specs=[
            pl.BlockSpec((tm, d), lambda i: (i, 0)),
            const((1, d)), const((1, d)),
            pl.BlockSpec((tm, wh), lambda i: (i, 0)),
            pl.BlockSpec((None, n1, kr, m), lambda i: (0, 0, i, 0)),
            pl.BlockSpec((None, n1, kr, m), lambda i: (1, 0, i, 0)),
            const((gd, gd)), const((gd, gd)),
            const((wh + m, d)),
            const((1, d)), const((1, d)),
        ],
        out_specs=pl.BlockSpec((tm, d), lambda i: (i, 0)),
        out_shape=jax.ShapeDtypeStruct((t, d), F32),
        scratch_shapes=[pltpu.VMEM((tm, m), BF16), pltpu.VMEM((tm, m), BF16)],
        compiler_params=pltpu.CompilerParams(
            dimension_semantics=("parallel",),
            vmem_limit_bytes=_vmem_limit(pipelined, resident)),
        name="mix_ln",
    )(x, ge, be, o_hgrn, p, p, c3, s3, wo, g1, b1)


def _ffn_ln_kernel(h1_ref, wg_ref, wu_ref, wd_ref, g2_ref, b2_ref, o_ref, hb_scr, *, alpha):
    f = pl.program_id(1)

    @pl.when(f == 0)
    def _():
        hb_scr[...] = h1_ref[...].astype(BF16)
        o_ref[...] = jnp.zeros_like(o_ref)

    hb = hb_scr[...]
    gate = _dot(hb, wg_ref[...].astype(BF16))
    up = _dot(hb, wu_ref[...].astype(BF16))
    act = (gate * _sigmoid(gate) * up).astype(BF16)
    cw = FFN_OUT_SLAB
    for n in range(o_ref.shape[1] // cw):
        cols = slice(n * cw, (n + 1) * cw)
        o_ref[:, cols] += _dot(act, wd_ref[:, cols].astype(BF16))

    @pl.when(f == pl.num_programs(1) - 1)
    def _():
        for r in range(0, o_ref.shape[0], LN_ROWS):
            rows = slice(r, r + LN_ROWS)
            o_ref[rows, :] = _layernorm(alpha * h1_ref[rows, :] + o_ref[rows, :], g2_ref[...], b2_ref[...])


def _ffn_ln(h1, wg, wu, wd, g2, b2, *, alpha, tm=1024, tf=256):
    t, d = h1.shape
    dff = wg.shape[1]
    pipelined = 2 * _nbytes((tm, d), F32) + 3 * _nbytes((d, tf), F32)
    resident = _nbytes((tm, d), BF16) + 2 * _nbytes((tm, tf), F32) + 2 * _nbytes((tm, FFN_OUT_SLAB), F32)
    return pl.pallas_call(
        functools.partial(_ffn_ln_kernel, alpha=alpha),
        grid=(t // tm, dff // tf),
        in_specs=[
            pl.BlockSpec((tm, d), lambda i, f: (i, 0)),
            pl.BlockSpec((d, tf), lambda i, f: (0, f)),
            pl.BlockSpec((d, tf), lambda i, f: (0, f)),
            pl.BlockSpec((tf, d), lambda i, f: (f, 0)),
            pl.BlockSpec((1, d), lambda i, f: (0, 0)),
            pl.BlockSpec((1, d), lambda i, f: (0, 0)),
        ],
        out_specs=pl.BlockSpec((tm, d), lambda i, f: (i, 0)),
        out_shape=jax.ShapeDtypeStruct((t, d), F32),
        scratch_shapes=[pltpu.VMEM((tm, d), BF16)],
        compiler_params=pltpu.CompilerParams(
            dimension_semantics=("parallel", "arbitrary"),
            vmem_limit_bytes=_vmem_limit(pipelined, resident)),
        name="ffn_ln",
    )(h1, wg, wu, wd, g2, b2)


def kernel(x, ln_emb_g, ln_emb_b, w_in, lb_fwd_logits, lb_bwd_logits, g_norm, w_out, ln1_g, ln1_b,
           w_gate, w_up, w_down, ln2_g, ln2_b):
    depth = w_in.shape[0]
    assert depth == 1, "the embedding LayerNorm is fused into the single layer's projection"
    batch, seq, d = x.shape
    assert batch == 1
    alpha = (2.0 * depth) ** 0.25
    layer = 0
    row = lambda a: a.reshape(1, -1).astype(F32)

    x2 = x.reshape(seq, d)
    log2f, one_minus_f, qig, four = _ln_proj(
        x2, row(ln_emb_g), row(ln_emb_b), lb_fwd_logits.astype(F32), lb_bwd_logits.astype(F32),
        w_in[layer].astype(BF16), layer=layer)

    o_hgrn = _hgrn(row(g_norm[layer]), qig, log2f, one_minus_f)

    g1c, f2c, c3, s3 = _dft_constants(seq, FFT_N1)
    p = _fft_t2(_fft_t1(four, g1c), f2c)

    wo = w_out[layer].astype(BF16)
    h1 = _mix_ln(x2, row(ln_emb_g), row(ln_emb_b), o_hgrn, p, c3, s3, wo,
                 row(ln1_g[layer]), row(ln1_b[layer]), alpha=alpha)

    out = _ffn_ln(h1, w_gate[layer], w_up[layer], w_down[layer],
                  row(ln2_g[layer]), row(ln2_b[layer]), alpha=alpha)
    return out.reshape(batch, seq, d)
```

```python
import functools
import math

import numpy as np
import jax
import jax.numpy as jnp
from jax import lax
from jax.experimental import pallas as pl
from jax.experimental.pallas import tpu as pltpu

LN_EPS = 1e-5
RMS_EPS = 1e-6
HEAD_DIM = 128
FOURIER_GDIM = 256
FFT_N1 = 32
BF16_ROWS = 16
LN_ROWS = 256
FFN_OUT_SLAB = 512
HGRN_CHUNK = 128
LN_PROJ_STEPS = 4
V7X_VMEM_CAP = 60 * 1024 * 1024

BF16 = jnp.bfloat16
F32 = jnp.float32


def _vmem_limit(pipelined_bytes, resident_bytes):
    return int(min(V7X_VMEM_CAP, 2 * pipelined_bytes + resident_bytes + (4 << 20)))


def _nbytes(shape, dtype):
    return int(np.prod(shape)) * jnp.dtype(dtype).itemsize


def _dot(a, b):
    return jnp.dot(a, b, preferred_element_type=F32)


def _layernorm(x, g, b):
    mu = jnp.mean(x, axis=-1, keepdims=True)
    xc = x - mu
    var = jnp.mean(xc * xc, axis=-1, keepdims=True)
    return xc * lax.rsqrt(var + LN_EPS) * g + b


def _sigmoid(x):
    return 1.0 / (1.0 + jnp.exp(-x))


def _silu(x):
    return x * _sigmoid(x)


def _lower_bound(logits_ref, layer):
    logits = logits_ref[...]
    e = jnp.exp(logits - jnp.max(logits, axis=0, keepdims=True))
    return jnp.sum(e[:layer + 1], axis=0, keepdims=True) / jnp.sum(e, axis=0, keepdims=True)


def _ln_proj_kernel(x_ref, g_ref, b_ref, lbf_ref, lbb_ref, wzf_ref, wzb_ref, wq_ref, wi_ref, wg_ref, wf_ref,
                    lf_ref, kk_ref, q_ref, f_ref, h_scr, *, layer):
    @pl.when(pl.program_id(1) == 0)
    def _():
        for r in range(0, x_ref.shape[0], LN_ROWS):
            rows = slice(r, r + LN_ROWS)
            h_scr[rows, :] = _layernorm(x_ref[rows, :], g_ref[...], b_ref[...]).astype(BF16)

    h = h_scr[...]
    sw = wf_ref.shape[1]
    for k, (w_ref, lb_ref) in enumerate(((wzf_ref, lbf_ref), (wzb_ref, lbb_ref))):
        lb = _lower_bound(lb_ref, layer)
        f = lb + (1.0 - lb) * _sigmoid(_dot(h, w_ref[...]))
        lf_ref[:, k * sw:(k + 1) * sw] = jnp.log2(f)
        kk_ref[:, k * sw:(k + 1) * sw] = (1.0 - f).astype(BF16)
    q_ref[:, 0:sw] = _silu(_dot(h, wq_ref[...])).astype(BF16)
    q_ref[:, sw:2 * sw] = _dot(h, wi_ref[...]).astype(BF16)
    q_ref[:, 2 * sw:3 * sw] = _silu(_dot(h, wg_ref[...])).astype(BF16)
    f_ref[...] = _dot(h, wf_ref[...]).astype(BF16)


def _ln_proj(x, g, b, lbf_logits, lbb_logits, w, *, layer, tm=1024):
    t, d = x.shape
    nj = LN_PROJ_STEPS
    sw = w.shape[1] // (6 * nj)
    slots = lbf_logits.shape[0]
    wspec = lambda section: pl.BlockSpec((d, sw), lambda i, j: (0, section * nj + j))
    pipelined = (_nbytes((tm, d), F32) + _nbytes((d, 6 * sw), BF16)
                 + _nbytes((tm, 2 * sw), F32) + _nbytes((tm, 6 * sw), BF16))
    resident = _nbytes((tm, d), BF16) + _nbytes((tm, 6 * sw), F32)
    return pl.pallas_call(
        functools.partial(_ln_proj_kernel, layer=layer),
        grid=(t // tm, nj),
        in_specs=[
            pl.BlockSpec((tm, d), lambda i, j: (i, 0)),
            pl.BlockSpec((1, d), lambda i, j: (0, 0)),
            pl.BlockSpec((1, d), lambda i, j: (0, 0)),
            pl.BlockSpec((slots, sw), lambda i, j: (0, j)),
            pl.BlockSpec((slots, sw), lambda i, j: (0, j)),
            wspec(2), wspec(3), wspec(0), wspec(1), wspec(4), wspec(5),
        ],
        out_specs=[
            pl.BlockSpec((tm, 2 * sw), lambda i, j: (i, j)),
            pl.BlockSpec((tm, 2 * sw), lambda i, j: (i, j)),
            pl.BlockSpec((tm, 3 * sw), lambda i, j: (i, j)),
            pl.BlockSpec((tm, sw), lambda i, j: (i, j)),
        ],
        out_shape=[
            jax.ShapeDtypeStruct((t, nj * 2 * sw), F32),
            jax.ShapeDtypeStruct((t, nj * 2 * sw), BF16),
            jax.ShapeDtypeStruct((t, nj * 3 * sw), BF16),
            jax.ShapeDtypeStruct((t, nj * sw), BF16),
        ],
        scratch_shapes=[pltpu.VMEM((tm, d), BF16)],
        compiler_params=pltpu.CompilerParams(
            dimension_semantics=("parallel", "arbitrary"),
            vmem_limit_bytes=_vmem_limit(pipelined, resident)),
        name="ln_proj",
    )(x, g, b, lbf_logits, lbb_logits, w, w, w, w, w, w)


def _hgrn_kernel(gn_ref, q_ref, v_ref, g_ref, lf_ref, kk_ref, o_ref, st_ref, of_ref, *, nblk, blk_rows):
    c = HGRN_CHUNK
    d = HEAD_DIM
    half = c // 2
    nchunk = blk_rows // c
    heads = st_ref.shape[0]
    phase = pl.program_id(1)
    t = pl.program_id(2)

    @pl.when(t == 0)
    def _():
        st_ref[...] = jnp.zeros_like(st_ref)

    def scan_block(fwd):
        row = lax.broadcasted_iota(jnp.int32, (c, c), 0)
        col = lax.broadcasted_iota(jnp.int32, (c, c), 1)
        tri = (col <= row) if fwd else (col >= row)
        tri2 = jnp.concatenate([tri.astype(BF16)] * 2, axis=1)
        blk = t if fwd else nblk - 1 - t
        chunk_rows = [pl.ds((ci if fwd else nchunk - 1 - ci) * c, c) for ci in range(nchunk)]
        head_cols = [slice(h * d, (h + 1) * d) for h in range(heads)]
        trans_a = (((0,), (0,)), ((), ()))

        bs = []
        for rows in chunk_rows:
            lf = lf_ref[rows, :]
            hi = lf.astype(BF16)
            lo = (lf - hi.astype(F32)).astype(BF16)
            bs.append(_dot(tri2, jnp.concatenate([hi, lo], axis=0)))

        qds, dsts, scs, decs, mids = [], [], [], [], []
        for rows, b in zip(chunk_rows, bs):
            b_mid = b[half - 1:half, :] if fwd else b[half:half + 1, :]
            b_end = b[c - 1:c, :] if fwd else b[0:1, :]
            kk = kk_ref[rows, :].astype(F32)
            qd = (q_ref[rows, :].astype(F32) * jnp.exp2(b - b_mid)).astype(BF16)
            ki = kk * jnp.exp2(b_mid - b)
            ke = (kk * jnp.exp2(b_end - b)).astype(BF16)
            v = v_ref[rows, :]
            qds.append(qd)
            scs.append([_dot(qd[:, hs], ki[:, hs].T.astype(BF16)) for hs in head_cols])
            dsts.append([lax.dot_general(v[:, hs], ke[:, hs], trans_a, preferred_element_type=F32)
                         for hs in head_cols])
            decs.append(jnp.exp2(b_end))
            mids.append(jnp.exp2(b_mid))

        s_mids = [[] for _ in chunk_rows]
        for h, hs in enumerate(head_cols):
            st = st_ref[h]
            for ci in range(nchunk):
                s_mids[ci].append((st * mids[ci][:, hs]).T.astype(BF16))
                st = decs[ci][:, hs] * st + dsts[ci][h]
            st_ref[h] = st

        for ci, rows in enumerate(chunk_rows):
            v = v_ref[rows, :]
            outs = []
            for h, hs in enumerate(head_cols):
                scores = jnp.where(tri, scs[ci][h], 0.0).astype(BF16)
                o = _dot(scores, v[:, hs]) + _dot(qds[ci][:, hs], s_mids[ci][h])
                if not fwd:
                    o = o + of_ref[pl.ds(pl.multiple_of(blk * blk_rows, c) + rows.start, c), hs]
                    o = o * lax.rsqrt(jnp.mean(o * o, axis=-1, keepdims=True) + RMS_EPS) * gn_ref[...]
                outs.append(o)
            o = jnp.concatenate(outs, axis=1)
            if fwd:
                of_ref[pl.ds(pl.multiple_of(blk * blk_rows, c) + rows.start, c), :] = o
            else:
                o_ref[rows, :] = (o * g_ref[rows, :].astype(F32)).astype(BF16)

    pl.when(phase == 0)(functools.partial(scan_block, True))
    pl.when(phase == 1)(functools.partial(scan_block, False))


def _hgrn(g_norm, qig, log2f, one_minus_f, *, blk_rows=4096):
    t = log2f.shape[0]
    nj = LN_PROJ_STEPS
    sw = log2f.shape[1] // (2 * nj)
    heads = sw // HEAD_DIM
    nblk = t // blk_rows
    d = HEAD_DIM

    tblk = lambda p, i: i + p * (nblk - 1 - 2 * i)
    qspec = lambda sec: pl.BlockSpec((blk_rows, sw), lambda j, p, i: (tblk(p, i), 3 * j + sec))
    fspec = pl.BlockSpec((blk_rows, sw), lambda j, p, i: (tblk(p, i), 2 * j + p))

    pipelined = 5 * _nbytes((blk_rows, sw), BF16) + _nbytes((blk_rows, sw), F32)
    resident = _nbytes((t, sw), F32) + 128 * _nbytes((d, d), F32)
    return pl.pallas_call(
        functools.partial(_hgrn_kernel, nblk=nblk, blk_rows=blk_rows),
        grid=(nj, 2, nblk),
        in_specs=[
            pl.BlockSpec((1, d), lambda j, p, i: (0, 0)),
            qspec(0), qspec(1),
            pl.BlockSpec((blk_rows, sw), lambda j, p, i: (nblk - 1 - p * i, 3 * j + 2)),
            fspec, fspec,
        ],
        out_specs=pl.BlockSpec((blk_rows, sw), lambda j, p, i: (nblk - 1 - p * i, j)),
        out_shape=jax.ShapeDtypeStruct((t, nj * sw), BF16),
        scratch_shapes=[pltpu.VMEM((heads, d, d), F32), pltpu.VMEM((t, sw), F32)],
        compiler_params=pltpu.CompilerParams(
            dimension_semantics=("parallel", "arbitrary", "arbitrary"),
            vmem_limit_bytes=_vmem_limit(pipelined, resident)),
        name="hgrn",
    )(g_norm, qig, qig, qig, log2f, one_minus_f)


def _dft_constants(t, n1):
    n2 = t // n1
    k1 = np.arange(n1)[None, :, None]
    t1 = np.arange(n1)[None, None, :]
    t2 = np.arange(n2)[:, None, None]
    ang = 2.0 * np.pi * ((k1 * (n2 * t1 + t2)) % t) / t
    g = np.concatenate([np.cos(ang), -np.sin(ang)], axis=1) / math.sqrt(n1)
    k2 = np.arange(n2)[:, None]
    s2 = np.arange(n2)[None, :]
    ang2 = 2.0 * np.pi * ((k2 * s2) % n2) / n2
    c2, sn2 = np.cos(ang2) / math.sqrt(n2), np.sin(ang2) / math.sqrt(n2)
    f2 = np.block([[c2, sn2], [-sn2, c2]])
    m = np.arange(FOURIER_GDIM)
    ang3 = 2.0 * np.pi * ((m[:, None] * m[None, :]) % FOURIER_GDIM) / FOURIER_GDIM
    c3 = np.cos(ang3) / math.sqrt(FOURIER_GDIM)
    s3 = np.sin(ang3) / math.sqrt(FOURIER_GDIM)
    as_bf16 = lambda a: jnp.asarray(a, dtype=F32).astype(BF16)
    return as_bf16(g), as_bf16(f2), as_bf16(c3), as_bf16(s3)


def _fft_t1_kernel(u_ref, g_ref, w_ref, *, tb, n1):
    x = pltpu.einshape("tjm->(jt)m", u_ref[...])
    r = jnp.stack([_dot(g_ref[j], x[j * n1:(j + 1) * n1]).astype(BF16) for j in range(tb)])
    r = pltpu.einshape("jam->ajm", r)
    w_ref[0] = r[:n1]
    w_ref[1] = r[n1:]


def _fft_t1(four, g, *, tb=BF16_ROWS):
    t, m = four.shape
    n2, _, n1 = g.shape
    u = four.reshape(n1, n2, m)
    pipelined = (_nbytes((n1, tb, m), BF16) + _nbytes((tb, 2 * n1, n1), BF16)
                 + _nbytes((2, n1, tb, m), BF16))
    return pl.pallas_call(
        functools.partial(_fft_t1_kernel, tb=tb, n1=n1),
        grid=(n2 // tb,),
        in_specs=[
            pl.BlockSpec((n1, tb, m), lambda s: (0, s, 0)),
            pl.BlockSpec((tb, 2 * n1, n1), lambda s: (s, 0, 0)),
        ],
        out_specs=pl.BlockSpec((2, n1, tb, m), lambda s: (0, 0, s, 0)),
        out_shape=jax.ShapeDtypeStruct((2, n1, n2, m), BF16),
        compiler_params=pltpu.CompilerParams(
            dimension_semantics=("parallel",),
            vmem_limit_bytes=_vmem_limit(pipelined, 6 * _nbytes((tb * n1, m), F32))),
        name="fft_t1",
    )(u, g)


def _fft_t2_kernel(f_ref, w_ref, p_ref, *, kb, n2):
    for j in range(kb):
        rhs = jnp.concatenate([w_ref[0, j], w_ref[1, j]], axis=0)
        res = _dot(f_ref[...], rhs)
        p_ref[0, j] = res[:n2].astype(BF16)
        p_ref[1, j] = res[n2:].astype(BF16)


def _fft_t2(w4, f2, *, kb=4):
    _, n1, n2, m = w4.shape
    blk = (2, kb, n2, m)
    pipelined = 2 * _nbytes(blk, BF16) + _nbytes((2 * n2, 2 * n2), BF16)
    return pl.pallas_call(
        functools.partial(_fft_t2_kernel, kb=kb, n2=n2),
        grid=(n1 // kb,),
        in_specs=[
            pl.BlockSpec((2 * n2, 2 * n2), lambda s: (0, 0)),
            pl.BlockSpec(blk, lambda s: (0, s, 0, 0)),
        ],
        out_specs=pl.BlockSpec(blk, lambda s: (0, s, 0, 0)),
        out_shape=jax.ShapeDtypeStruct((2, n1, n2, m), BF16),
        compiler_params=pltpu.CompilerParams(
            dimension_semantics=("parallel",),
            vmem_limit_bytes=_vmem_limit(pipelined, 4 * _nbytes((2 * n2, m), F32))),
        name="fft_t2",
    )(f2, w4)


def _mix_ln_kernel(x_ref, ge_ref, be_ref, oh_ref, pr_ref, pi_ref, c3_ref, s3_ref, wo_ref,
                   g1_ref, b1_ref, h1_ref, pr_scr, pi_scr, *, alpha):
    gd = FOURIER_GDIM
    pr_scr[...] = pltpu.einshape("akm->(ka)m", pr_ref[...])
    pi_scr[...] = pltpu.einshape("akm->(ka)m", pi_ref[...])
    c3, s3 = c3_ref[...], s3_ref[...]
    parts = []
    for g in range(pr_scr.shape[1] // gd):
        sl = slice(g * gd, (g + 1) * gd)
        parts.append((_dot(pr_scr[:, sl], c3) + _dot(pi_scr[:, sl], s3)).astype(BF16))
    o_four = jnp.concatenate(parts, axis=1)
    wh = oh_ref.shape[1]
    mix = _dot(oh_ref[...], wo_ref[:wh, :]) + _dot(o_four, wo_ref[wh:, :])
    h0 = _layernorm(x_ref[...], ge_ref[...], be_ref[...])
    h1_ref[...] = _layernorm(alpha * h0 + mix, g1_ref[...], b1_ref[...])


def _mix_ln(x, ge, be, o_hgrn, p, c3, s3, wo, g1, b1, *, alpha, tm=512):
    t, d = x.shape
    wh = o_hgrn.shape[1]
    _, n1, n2, m = p.shape
    kr = tm // n1
    gd = FOURIER_GDIM
    const = lambda shape: pl.BlockSpec(shape, lambda i: tuple(0 for _ in shape),
                                       pipeline_mode=pl.Buffered(1))
    pipelined = (_nbytes((tm, d), F32) + _nbytes((tm, wh), BF16) + 2 * _nbytes((n1, kr, m), BF16)
                 + _nbytes((tm, d), F32))
    resident = (_nbytes((wh + m, d), BF16) + 2 * _nbytes((tm, m), BF16) + 2 * _nbytes((tm, m), F32)
                + 3 * _nbytes((tm, d), F32))
    return pl.pallas_call(
        functools.partial(_mix_ln_kernel, alpha=alpha),
        grid=(t // tm,),
        in_specs=[
            pl.BlockSpec((tm, d), lambda i: (i, 0)),
            const((1, d)), const((1, d)),
            pl.BlockSpec((tm, wh), lambda i: (i, 0)),
            pl.BlockSpec((None, n1, kr, m), lambda i: (0, 0, i, 0)),
            pl.BlockSpec((None, n1, kr, m), lambda i: (1, 0, i, 0)),
            const((gd, gd)), const((gd, gd)),
            const((wh + m, d)),
            const((1, d)), const((1, d)),
        ],
        out_specs=pl.BlockSpec((tm, d), lambda i: (i, 0)),
        out_shape=jax.ShapeDtypeStruct((t, d), F32),
        scratch_shapes=[pltpu.VMEM((tm, m), BF16), pltpu.VMEM((tm, m), BF16)],
        compiler_params=pltpu.CompilerParams(
            dimension_semantics=("parallel",),
            vmem_limit_bytes=_vmem_limit(pipelined, resident)),
        name="mix_ln",
    )(x, ge, be, o_hgrn, p, p, c3, s3, wo, g1, b1)


def _ffn_ln_kernel(h1_ref, wg_ref, wu_ref, wd_ref, g2_ref, b2_ref, o_ref, hb_scr, *, alpha):
    f = pl.program_id(1)

    @pl.when(f == 0)
    def _():
        hb_scr[...] = h1_ref[...].astype(BF16)
        o_ref[...] = jnp.zeros_like(o_ref)

    hb = hb_scr[...]
    gate = _dot(hb, wg_ref[...].astype(BF16))
    up = _dot(hb, wu_ref[...].astype(BF16))
    act = (gate * _sigmoid(gate) * up).astype(BF16)
    cw = FFN_OUT_SLAB
    for n in range(o_ref.shape[1] // cw):
        cols = slice(n * cw, (n + 1) * cw)
        o_ref[:, cols] += _dot(act, wd_ref[:, cols].astype(BF16))

    @pl.when(f == pl.num_programs(1) - 1)
    def _():
        for r in range(0, o_ref.shape[0], LN_ROWS):
            rows = slice(r, r + LN_ROWS)
            o_ref[rows, :] = _layernorm(alpha * h1_ref[rows, :] + o_ref[rows, :], g2_ref[...], b2_ref[...])


def _ffn_ln(h1, wg, wu, wd, g2, b2, *, alpha, tm=1024, tf=256):
    t, d = h1.shape
    dff = wg.shape[1]
    pipelined = 2 * _nbytes((tm, d), F32) + 3 * _nbytes((d, tf), F32)
    resident = _nbytes((tm, d), BF16) + 2 * _nbytes((tm, tf), F32) + 2 * _nbytes((tm, FFN_OUT_SLAB), F32)
    return pl.pallas_call(
        functools.partial(_ffn_ln_kernel, alpha=alpha),
        grid=(t // tm, dff // tf),
        in_specs=[
            pl.BlockSpec((tm, d), lambda i, f: (i, 0)),
            pl.BlockSpec((d, tf), lambda i, f: (0, f)),
            pl.BlockSpec((d, tf), lambda i, f: (0, f)),
            pl.BlockSpec((tf, d), lambda i, f: (f, 0)),
            pl.BlockSpec((1, d), lambda i, f: (0, 0)),
            pl.BlockSpec((1, d), lambda i, f: (0, 0)),
        ],
        out_specs=pl.BlockSpec((tm, d), lambda i, f: (i, 0)),
        out_shape=jax.ShapeDtypeStruct((t, d), F32),
        scratch_shapes=[pltpu.VMEM((tm, d), BF16)],
        compiler_params=pltpu.CompilerParams(
            dimension_semantics=("parallel", "arbitrary"),
            vmem_limit_bytes=_vmem_limit(pipelined, resident)),
        name="ffn_ln",
    )(h1, wg, wu, wd, g2, b2)


def kernel(x, ln_emb_g, ln_emb_b, w_in, lb_fwd_logits, lb_bwd_logits, g_norm, w_out, ln1_g, ln1_b,
           w_gate, w_up, w_down, ln2_g, ln2_b):
    depth = w_in.shape[0]
    assert depth == 1, "the embedding LayerNorm is fused into the single layer's projection"
    batch, seq, d = x.shape
    assert batch == 1
    alpha = (2.0 * depth) ** 0.25
    layer = 0
    row = lambda a: a.reshape(1, -1).astype(F32)

    x2 = x.reshape(seq, d)
    log2f, one_minus_f, qig, four = _ln_proj(
        x2, row(ln_emb_g), row(ln_emb_b), lb_fwd_logits.astype(F32), lb_bwd_logits.astype(F32),
        w_in[layer].astype(BF16), layer=layer)

    o_hgrn = _hgrn(row(g_norm[layer]), qig, log2f, one_minus_f)

    g1c, f2c, c3, s3 = _dft_constants(seq, FFT_N1)
    p = _fft_t2(_fft_t1(four, g1c), f2c)

    wo = w_out[layer].astype(BF16)
    h1 = _mix_ln(x2, row(ln_emb_g), row(ln_emb_b), o_hgrn, p, c3, s3, wo,
                 row(ln1_g[layer]), row(ln1_b[layer]), alpha=alpha)

    out = _ffn_ln(h1, w_gate[layer], w_up[layer], w_down[layer],
                  row(ln2_g[layer]), row(ln2_b[layer]), alpha=alpha)
    return out.reshape(batch, seq, d)
```

```python
import functools
import math

import numpy as np
import jax
import jax.numpy as jnp
from jax import lax
from jax.experimental import pallas as pl
from jax.experimental.pallas import tpu as pltpu

LN_EPS = 1e-5
RMS_EPS = 1e-6
HEAD_DIM = 128
FOURIER_GDIM = 256
FFT_N1 = 32
BF16_ROWS = 16
LN_ROWS = 256
FFN_OUT_SLAB = 512
HGRN_CHUNK = 128
LN_PROJ_STEPS = 4
V7X_VMEM_CAP = 63 * 1024 * 1024

BF16 = jnp.bfloat16
F32 = jnp.float32


def _vmem_limit(pipelined_bytes, resident_bytes):
    return int(min(V7X_VMEM_CAP, 2 * pipelined_bytes + resident_bytes + (4 << 20)))


def _nbytes(shape, dtype):
    return int(np.prod(shape)) * jnp.dtype(dtype).itemsize


def _dot(a, b):
    return jnp.dot(a, b, preferred_element_type=F32)


def _layernorm(x, g, b):
    mu = jnp.mean(x, axis=-1, keepdims=True)
    xc = x - mu
    var = jnp.mean(xc * xc, axis=-1, keepdims=True)
    return xc * lax.rsqrt(var + LN_EPS) * g + b


def _sigmoid(x):
    return 1.0 / (1.0 + jnp.exp(-x))


def _silu(x):
    return x * _sigmoid(x)


def _lower_bound(logits_ref, layer):
    logits = logits_ref[...]
    e = jnp.exp(logits - jnp.max(logits, axis=0, keepdims=True))
    return jnp.sum(e[:layer + 1], axis=0, keepdims=True) / jnp.sum(e, axis=0, keepdims=True)


def _ln_proj_kernel(x_ref, g_ref, b_ref, lbf_ref, lbb_ref, wzf_ref, wzb_ref, wq_ref, wi_ref, wg_ref, wf_ref,
                    lf_ref, kk_ref, q_ref, f_ref, h_scr, *, layer):
    @pl.when(pl.program_id(1) == 0)
    def _():
        for r in range(0, x_ref.shape[0], LN_ROWS):
            rows = slice(r, r + LN_ROWS)
            h_scr[rows, :] = _layernorm(x_ref[rows, :], g_ref[...], b_ref[...]).astype(BF16)

    h = h_scr[...]
    sw = wf_ref.shape[1]
    for k, (w_ref, lb_ref) in enumerate(((wzf_ref, lbf_ref), (wzb_ref, lbb_ref))):
        lb = _lower_bound(lb_ref, layer)
        f = lb + (1.0 - lb) * _sigmoid(_dot(h, w_ref[...]))
        lf_ref[:, k * sw:(k + 1) * sw] = jnp.log2(f)
        kk_ref[:, k * sw:(k + 1) * sw] = (1.0 - f).astype(BF16)
    q_ref[:, 0:sw] = _silu(_dot(h, wq_ref[...])).astype(BF16)
    q_ref[:, sw:2 * sw] = _dot(h, wi_ref[...]).astype(BF16)
    q_ref[:, 2 * sw:3 * sw] = _silu(_dot(h, wg_ref[...])).astype(BF16)
    f_ref[...] = _dot(h, wf_ref[...]).astype(BF16)


def _ln_proj(x, g, b, lbf_logits, lbb_logits, w, *, layer, tm=1024):
    t, d = x.shape
    nj = LN_PROJ_STEPS
    sw = w.shape[1] // (6 * nj)
    slots = lbf_logits.shape[0]
    wspec = lambda section: pl.BlockSpec((d, sw), lambda i, j: (0, section * nj + j))
    pipelined = (_nbytes((tm, d), F32) + _nbytes((d, 6 * sw), BF16)
                 + _nbytes((tm, 2 * sw), F32) + _nbytes((tm, 6 * sw), BF16))
    resident = _nbytes((tm, d), BF16) + _nbytes((tm, 6 * sw), F32)
    return pl.pallas_call(
        functools.partial(_ln_proj_kernel, layer=layer),
        grid=(t // tm, nj),
        in_specs=[
            pl.BlockSpec((tm, d), lambda i, j: (i, 0)),
            pl.BlockSpec((1, d), lambda i, j: (0, 0)),
            pl.BlockSpec((1, d), lambda i, j: (0, 0)),
            pl.BlockSpec((slots, sw), lambda i, j: (0, j)),
            pl.BlockSpec((slots, sw), lambda i, j: (0, j)),
            wspec(2), wspec(3), wspec(0), wspec(1), wspec(4), wspec(5),
        ],
        out_specs=[
            pl.BlockSpec((tm, 2 * sw), lambda i, j: (i, j)),
            pl.BlockSpec((tm, 2 * sw), lambda i, j: (i, j)),
            pl.BlockSpec((tm, 3 * sw), lambda i, j: (i, j)),
            pl.BlockSpec((tm, sw), lambda i, j: (i, j)),
        ],
        out_shape=[
            jax.ShapeDtypeStruct((t, nj * 2 * sw), F32),
            jax.ShapeDtypeStruct((t, nj * 2 * sw), BF16),
            jax.ShapeDtypeStruct((t, nj * 3 * sw), BF16),
            jax.ShapeDtypeStruct((t, nj * sw), BF16),
        ],
        scratch_shapes=[pltpu.VMEM((tm, d), BF16)],
        compiler_params=pltpu.CompilerParams(
            dimension_semantics=("parallel", "arbitrary"),
            vmem_limit_bytes=_vmem_limit(pipelined, resident)),
        name="ln_proj",
    )(x, g, b, lbf_logits, lbb_logits, w, w, w, w, w, w)


def _hgrn_kernel(gn_ref, q_ref, v_ref, g_ref, lf_ref, kk_ref, o_ref, st_ref, of_ref, *, nblk, blk_rows):
    c = HGRN_CHUNK
    d = HEAD_DIM
    half = c // 2
    nchunk = blk_rows // c
    heads = st_ref.shape[0]
    phase = pl.program_id(1)
    t = pl.program_id(2)

    @pl.when(t == 0)
    def _():
        st_ref[...] = jnp.zeros_like(st_ref)

    def scan_block(fwd):
        row = lax.broadcasted_iota(jnp.int32, (c, c), 0)
        col = lax.broadcasted_iota(jnp.int32, (c, c), 1)
        tri = (col <= row) if fwd else (col >= row)
        tri2 = jnp.concatenate([tri.astype(BF16)] * 2, axis=1)
        blk = t if fwd else nblk - 1 - t
        chunk_rows = [pl.ds((ci if fwd else nchunk - 1 - ci) * c, c) for ci in range(nchunk)]
        head_cols = [slice(h * d, (h + 1) * d) for h in range(heads)]
        trans_a = (((0,), (0,)), ((), ()))

        bs = []
        for rows in chunk_rows:
            lf = lf_ref[rows, :]
            hi = lf.astype(BF16)
            lo = (lf - hi.astype(F32)).astype(BF16)
            bs.append(_dot(tri2, jnp.concatenate([hi, lo], axis=0)))

        qds, dsts, scs, decs, mids = [], [], [], [], []
        for rows, b in zip(chunk_rows, bs):
            b_mid = b[half - 1:half, :] if fwd else b[half:half + 1, :]
            b_end = b[c - 1:c, :] if fwd else b[0:1, :]
            kk = kk_ref[rows, :].astype(F32)
            qd = (q_ref[rows, :].astype(F32) * jnp.exp2(b - b_mid)).astype(BF16)
            ki = kk * jnp.exp2(b_mid - b)
            ke = (kk * jnp.exp2(b_end - b)).astype(BF16)
            v = v_ref[rows, :]
            qds.append(qd)
            scs.append([_dot(qd[:, hs], ki[:, hs].T.astype(BF16)) for hs in head_cols])
            dsts.append([lax.dot_general(v[:, hs], ke[:, hs], trans_a, preferred_element_type=F32)
                         for hs in head_cols])
            decs.append(jnp.exp2(b_end))
            mids.append(jnp.exp2(b_mid))

        s_mids = [[] for _ in chunk_rows]
        for h, hs in enumerate(head_cols):
            st = st_ref[h]
            for ci in range(nchunk):
                s_mids[ci].append((st * mids[ci][:, hs]).T.astype(BF16))
                st = decs[ci][:, hs] * st + dsts[ci][h]
            st_ref[h] = st

        for ci, rows in enumerate(chunk_rows):
            v = v_ref[rows, :]
            outs = []
            for h, hs in enumerate(head_cols):
                scores = jnp.where(tri, scs[ci][h], 0.0).astype(BF16)
                o = _dot(scores, v[:, hs]) + _dot(qds[ci][:, hs], s_mids[ci][h])
                if not fwd:
                    o = o + of_ref[pl.ds(pl.multiple_of(blk * blk_rows, c) + rows.start, c), hs]
                    o = o * lax.rsqrt(jnp.mean(o * o, axis=-1, keepdims=True) + RMS_EPS) * gn_ref[...]
                outs.append(o)
            o = jnp.concatenate(outs, axis=1)
            if fwd:
                of_ref[pl.ds(pl.multiple_of(blk * blk_rows, c) + rows.start, c), :] = o
            else:
                o_ref[rows, :] = (o * g_ref[rows, :].astype(F32)).astype(BF16)

    pl.when(phase == 0)(functools.partial(scan_block, True))
    pl.when(phase == 1)(functools.partial(scan_block, False))


def _hgrn(g_norm, qig, log2f, one_minus_f, *, blk_rows=4096):
    t = log2f.shape[0]
    nj = LN_PROJ_STEPS
    sw = log2f.shape[1] // (2 * nj)
    heads = sw // HEAD_DIM
    nblk = t // blk_rows
    d = HEAD_DIM

    tblk = lambda p, i: i + p * (nblk - 1 - 2 * i)
    qspec = lambda sec: pl.BlockSpec((blk_rows, sw), lambda j, p, i: (tblk(p, i), 3 * j + sec))
    fspec = pl.BlockSpec((blk_rows, sw), lambda j, p, i: (tblk(p, i), 2 * j + p))

    pipelined = 5 * _nbytes((blk_rows, sw), BF16) + _nbytes((blk_rows, sw), F32)
    resident = _nbytes((t, sw), F32) + 128 * _nbytes((d, d), F32)
    return pl.pallas_call(
        functools.partial(_hgrn_kernel, nblk=nblk, blk_rows=blk_rows),
        grid=(nj, 2, nblk),
        in_specs=[
            pl.BlockSpec((1, d), lambda j, p, i: (0, 0)),
            qspec(0), qspec(1),
            pl.BlockSpec((blk_rows, sw), lambda j, p, i: (nblk - 1 - p * i, 3 * j + 2)),
            fspec, fspec,
        ],
        out_specs=pl.BlockSpec((blk_rows, sw), lambda j, p, i: (nblk - 1 - p * i, j)),
        out_shape=jax.ShapeDtypeStruct((t, nj * sw), BF16),
        scratch_shapes=[pltpu.VMEM((heads, d, d), F32), pltpu.VMEM((t, sw), F32)],
        compiler_params=pltpu.CompilerParams(
            dimension_semantics=("parallel", "arbitrary", "arbitrary"),
            vmem_limit_bytes=_vmem_limit(pipelined, resident)),
        name="hgrn",
    )(g_norm, qig, qig, qig, log2f, one_minus_f)


def _dft_constants(t, n1):
    n2 = t // n1
    k1 = np.arange(n1)[None, :, None]
    t1 = np.arange(n1)[None, None, :]
    t2 = np.arange(n2)[:, None, None]
    ang = 2.0 * np.pi * ((k1 * (n2 * t1 + t2)) % t) / t
    g = np.concatenate([np.cos(ang), -np.sin(ang)], axis=1) / math.sqrt(n1)
    k2 = np.arange(n2)[:, None]
    s2 = np.arange(n2)[None, :]
    ang2 = 2.0 * np.pi * ((k2 * s2) % n2) / n2
    c2, sn2 = np.cos(ang2) / math.sqrt(n2), np.sin(ang2) / math.sqrt(n2)
    f2 = np.block([[c2, sn2], [-sn2, c2]])
    m = np.arange(FOURIER_GDIM)
    ang3 = 2.0 * np.pi * ((m[:, None] * m[None, :]) % FOURIER_GDIM) / FOURIER_GDIM
    c3 = np.cos(ang3) / math.sqrt(FOURIER_GDIM)
    s3 = np.sin(ang3) / math.sqrt(FOURIER_GDIM)
    as_bf16 = lambda a: jnp.asarray(a, dtype=F32).astype(BF16)
    return as_bf16(g), as_bf16(f2), as_bf16(c3), as_bf16(s3)


def _fft_t1_kernel(u_ref, g_ref, w_ref, *, tb, n1):
    x = pltpu.einshape("tjm->(jt)m", u_ref[...])
    r = jnp.stack([_dot(g_ref[j], x[j * n1:(j + 1) * n1]).astype(BF16) for j in range(tb)])
    r = pltpu.einshape("jam->ajm", r)
    w_ref[0] = r[:n1]
    w_ref[1] = r[n1:]


def _fft_t1(four, g, *, tb=BF16_ROWS):
    t, m = four.shape
    n2, _, n1 = g.shape
    u = four.reshape(n1, n2, m)
    pipelined = (_nbytes((n1, tb, m), BF16) + _nbytes((tb, 2 * n1, n1), BF16)
                 + _nbytes((2, n1, tb, m), BF16))
    return pl.pallas_call(
        functools.partial(_fft_t1_kernel, tb=tb, n1=n1),
        grid=(n2 // tb,),
        in_specs=[
            pl.BlockSpec((n1, tb, m), lambda s: (0, s, 0)),
            pl.BlockSpec((tb, 2 * n1, n1), lambda s: (s, 0, 0)),
        ],
        out_specs=pl.BlockSpec((2, n1, tb, m), lambda s: (0, 0, s, 0)),
        out_shape=jax.ShapeDtypeStruct((2, n1, n2, m), BF16),
        compiler_params=pltpu.CompilerParams(
            dimension_semantics=("parallel",),
            vmem_limit_bytes=_vmem_limit(pipelined, 6 * _nbytes((tb * n1, m), F32))),
        name="fft_t1",
    )(u, g)


def _fft_t2_kernel(f_ref, w_ref, p_ref, *, kb, n2):
    for j in range(kb):
        rhs = jnp.concatenate([w_ref[0, j], w_ref[1, j]], axis=0)
        res = _dot(f_ref[...], rhs)
        p_ref[0, j] = res[:n2].astype(BF16)
        p_ref[1, j] = res[n2:].astype(BF16)


def _fft_t2(w4, f2, *, kb=4):
    _, n1, n2, m = w4.shape
    blk = (2, kb, n2, m)
    pipelined = 2 * _nbytes(blk, BF16) + _nbytes((2 * n2, 2 * n2), BF16)
    return pl.pallas_call(
        functools.partial(_fft_t2_kernel, kb=kb, n2=n2),
        grid=(n1 // kb,),
        in_specs=[
            pl.BlockSpec((2 * n2, 2 * n2), lambda s: (0, 0)),
            pl.BlockSpec(blk, lambda s: (0, s, 0, 0)),
        ],
        out_specs=pl.BlockSpec(blk, lambda s: (0, s, 0, 0)),
        out_shape=jax.ShapeDtypeStruct((2, n1, n2, m), BF16),
        compiler_params=pltpu.CompilerParams(
            dimension_semantics=("parallel",),
            vmem_limit_bytes=_vmem_limit(pipelined, 4 * _nbytes((2 * n2, m), F32))),
        name="fft_t2",
    )(f2, w4)


def _mix_ln_kernel(x_ref, ge_ref, be_ref, oh_ref, pr_ref, pi_ref, c3_ref, s3_ref, wo_ref,
                   g1_ref, b1_ref, h1_ref, pr_scr, pi_scr, *, alpha):
    gd = FOURIER_GDIM
    pr_scr[...] = pltpu.einshape("akm->(ka)m", pr_ref[...])
    pi_scr[...] = pltpu.einshape("akm->(ka)m", pi_ref[...])
    c3, s3 = c3_ref[...], s3_ref[...]
    parts = []
    for g in range(pr_scr.shape[1] // gd):
        sl = slice(g * gd, (g + 1) * gd)
        parts.append((_dot(pr_scr[:, sl], c3) + _dot(pi_scr[:, sl], s3)).astype(BF16))
    o_four = jnp.concatenate(parts, axis=1)
    wh = oh_ref.shape[1]
    mix = _dot(oh_ref[...], wo_ref[:wh, :]) + _dot(o_four, wo_ref[wh:, :])
    h0 = _layernorm(x_ref[...], ge_ref[...], be_ref[...])
    h1_ref[...] = _layernorm(alpha * h0 + mix, g1_ref[...], b1_ref[...])


def _mix_ln(x, ge, be, o_hgrn, p, c3, s3, wo, g1, b1, *, alpha, tm=512):
    t, d = x.shape
    wh = o_hgrn.shape[1]
    _, n1, n2, m = p.shape
    kr = tm // n1
    gd = FOURIER_GDIM
    const = lambda shape: pl.BlockSpec(shape, lambda i: tuple(0 for _ in shape),
                                       pipeline_mode=pl.Buffered(1))
    pipelined = (_nbytes((tm, d), F32) + _nbytes((tm, wh), BF16) + 2 * _nbytes((n1, kr, m), BF16)
                 + _nbytes((tm, d), F32))
    resident = (_nbytes((wh + m, d), BF16) + 2 * _nbytes((tm, m), BF16) + 2 * _nbytes((tm, m), F32)
                + 3 * _nbytes((tm, d), F32))
    return pl.pallas_call(
        functools.partial(_mix_ln_kernel, alpha=alpha),
        grid=(t // tm,),
        in_specs=[
            pl.BlockSpec((tm, d), lambda i: (i, 0)),
            const((1, d)), const((1, d)),
            pl.BlockSpec((tm, wh), lambda i: (i, 0)),
            pl.BlockSpec((None, n1, kr, m), lambda i: (0, 0, i, 0)),
            pl.BlockSpec((None, n1, kr, m), lambda i: (1, 0, i, 0)),
            const((gd, gd)), const((gd, gd)),
            const((wh + m, d)),
            const((1, d)), const((1, d)),
        ],
        out_specs=pl.BlockSpec((tm, d), lambda i: (i, 0)),
        out_shape=jax.ShapeDtypeStruct((t, d), F32),
        scratch_shapes=[pltpu.VMEM((tm, m), BF16), pltpu.VMEM((tm, m), BF16)],
        compiler_params=pltpu.CompilerParams(
            dimension_semantics=("parallel",),
            vmem_limit_bytes=_vmem_limit(pipelined, resident)),
        name="mix_ln",
    )(x, ge, be, o_hgrn, p, p, c3, s3, wo, g1, b1)


def _ffn_ln_kernel(h1_ref, wg_ref, wu_ref, wd_ref, g2_ref, b2_ref, o_ref, hb_scr, *, alpha):
    f = pl.program_id(1)

    @pl.when(f == 0)
    def _():
        hb_scr[...] = h1_ref[...].astype(BF16)
        o_ref[...] = jnp.zeros_like(o_ref)

    hb = hb_scr[...]
    gate = _dot(hb, wg_ref[...].astype(BF16))
    up = _dot(hb, wu_ref[...].astype(BF16))
    act = (gate * _sigmoid(gate) * up).astype(BF16)
    cw = FFN_OUT_SLAB
    for n in range(o_ref.shape[1] // cw):
        cols = slice(n * cw, (n + 1) * cw)
        o_ref[:, cols] += _dot(act, wd_ref[:, cols].astype(BF16))

    @pl.when(f == pl.num_programs(1) - 1)
    def _():
        for r in range(0, o_ref.shape[0], LN_ROWS):
            rows = slice(r, r + LN_ROWS)
            o_ref[rows, :] = _layernorm(alpha * h1_ref[rows, :] + o_ref[rows, :], g2_ref[...], b2_ref[...])


def _ffn_ln(h1, wg, wu, wd, g2, b2, *, alpha, tm=1024, tf=512):
    t, d = h1.shape
    dff = wg.shape[1]
    pipelined = _nbytes((tm, d), F32) + 3 * _nbytes((d, tf), F32)
    resident = (_nbytes((tm, d), F32) + _nbytes((tm, d), BF16) + 2 * _nbytes((tm, tf), F32)
                + 2 * _nbytes((tm, FFN_OUT_SLAB), F32))
    return pl.pallas_call(
        functools.partial(_ffn_ln_kernel, alpha=alpha),
        grid=(t // tm, dff // tf),
        in_specs=[
            pl.BlockSpec((tm, d), lambda i, f: (i, 0)),
            pl.BlockSpec((d, tf), lambda i, f: (0, f)),
            pl.BlockSpec((d, tf), lambda i, f: (0, f)),
            pl.BlockSpec((tf, d), lambda i, f: (f, 0)),
            pl.BlockSpec((1, d), lambda i, f: (0, 0)),
            pl.BlockSpec((1, d), lambda i, f: (0, 0)),
        ],
        out_specs=pl.BlockSpec((tm, d), lambda i, f: (i, 0), pipeline_mode=pl.Buffered(1)),
        out_shape=jax.ShapeDtypeStruct((t, d), F32),
        scratch_shapes=[pltpu.VMEM((tm, d), BF16)],
        compiler_params=pltpu.CompilerParams(
            dimension_semantics=("parallel", "arbitrary"),
            vmem_limit_bytes=_vmem_limit(pipelined, resident)),
        name="ffn_ln",
    )(h1, wg, wu, wd, g2, b2)


def kernel(x, ln_emb_g, ln_emb_b, w_in, lb_fwd_logits, lb_bwd_logits, g_norm, w_out, ln1_g, ln1_b,
           w_gate, w_up, w_down, ln2_g, ln2_b):
    depth = w_in.shape[0]
    assert depth == 1, "the embedding LayerNorm is fused into the single layer's projection"
    batch, seq, d = x.shape
    assert batch == 1
    alpha = (2.0 * depth) ** 0.25
    layer = 0
    row = lambda a: a.reshape(1, -1).astype(F32)

    x2 = x.reshape(seq, d)
    log2f, one_minus_f, qig, four = _ln_proj(
        x2, row(ln_emb_g), row(ln_emb_b), lb_fwd_logits.astype(F32), lb_bwd_logits.astype(F32),
        w_in[layer].astype(BF16), layer=layer)

    o_hgrn = _hgrn(row(g_norm[layer]), qig, log2f, one_minus_f)

    g1c, f2c, c3, s3 = _dft_constants(seq, FFT_N1)
    p = _fft_t2(_fft_t1(four, g1c), f2c)

    wo = w_out[layer].astype(BF16)
    h1 = _mix_ln(x2, row(ln_emb_g), row(ln_emb_b), o_hgrn, p, c3, s3, wo,
                 row(ln1_g[layer]), row(ln1_b[layer]), alpha=alpha)

    out = _ffn_ln(h1, w_gate[layer], w_up[layer], w_down[layer],
                  row(ln2_g[layer]), row(ln2_b[layer]), alpha=alpha)
    return out.reshape(batch, seq, d)
```

```python
import functools
import math

import numpy as np
import jax
import jax.numpy as jnp
from jax import lax
from jax.experimental import pallas as pl
from jax.experimental.pallas import tpu as pltpu

LN_EPS = 1e-5
RMS_EPS = 1e-6
HEAD_DIM = 128
FOURIER_GDIM = 256
FFT_N1 = 32
BF16_ROWS = 16
LN_ROWS = 256
FFN_OUT_SLAB = 512
HGRN_CHUNK = 128
LN_PROJ_STEPS = 4
V7X_VMEM_CAP = 63 * 1024 * 1024

BF16 = jnp.bfloat16
F32 = jnp.float32


def _vmem_limit(pipelined_bytes, resident_bytes):
    return int(min(V7X_VMEM_CAP, 2 * pipelined_bytes + resident_bytes + (4 << 20)))


def _nbytes(shape, dtype):
    return int(np.prod(shape)) * jnp.dtype(dtype).itemsize


def _dot(a, b):
    return jnp.dot(a, b, preferred_element_type=F32)


def _layernorm(x, g, b):
    mu = jnp.mean(x, axis=-1, keepdims=True)
    xc = x - mu
    var = jnp.mean(xc * xc, axis=-1, keepdims=True)
    return xc * lax.rsqrt(var + LN_EPS) * g + b


def _sigmoid(x):
    return 1.0 / (1.0 + jnp.exp(-x))


def _silu(x):
    return x * _sigmoid(x)


def _lower_bound(logits_ref, layer):
    logits = logits_ref[...]
    e = jnp.exp(logits - jnp.max(logits, axis=0, keepdims=True))
    return jnp.sum(e[:layer + 1], axis=0, keepdims=True) / jnp.sum(e, axis=0, keepdims=True)


def _ln_proj_kernel(x_ref, g_ref, b_ref, lbf_ref, lbb_ref, wzf_ref, wzb_ref, wq_ref, wi_ref, wg_ref, wf_ref,
                    lf_ref, kk_ref, q_ref, f_ref, h_scr, *, layer):
    @pl.when(pl.program_id(1) == 0)
    def _():
        for r in range(0, x_ref.shape[0], LN_ROWS):
            rows = slice(r, r + LN_ROWS)
            h_scr[rows, :] = _layernorm(x_ref[rows, :], g_ref[...], b_ref[...]).astype(BF16)

    h = h_scr[...]
    sw = wf_ref.shape[1]
    proj = lambda w_ref: _dot(h, w_ref[...].astype(BF16))
    for k, (w_ref, lb_ref) in enumerate(((wzf_ref, lbf_ref), (wzb_ref, lbb_ref))):
        lb = _lower_bound(lb_ref, layer)
        f = lb + (1.0 - lb) * _sigmoid(proj(w_ref))
        lf_ref[:, k * sw:(k + 1) * sw] = jnp.log2(f)
        kk_ref[:, k * sw:(k + 1) * sw] = (1.0 - f).astype(BF16)
    q_ref[:, 0:sw] = _silu(proj(wq_ref)).astype(BF16)
    q_ref[:, sw:2 * sw] = proj(wi_ref).astype(BF16)
    q_ref[:, 2 * sw:3 * sw] = _silu(proj(wg_ref)).astype(BF16)
    f_ref[...] = proj(wf_ref).astype(BF16)


def _ln_proj(x, g, b, lbf_logits, lbb_logits, w, *, layer, tm=1024):
    t, d = x.shape
    nj = LN_PROJ_STEPS
    sw = w.shape[1] // (6 * nj)
    slots = lbf_logits.shape[0]
    wspec = lambda section: pl.BlockSpec((d, sw), lambda i, j: (0, section * nj + j))
    pipelined = (_nbytes((tm, d), F32) + _nbytes((d, 6 * sw), F32)
                 + _nbytes((tm, 2 * sw), F32) + _nbytes((tm, 6 * sw), BF16))
    resident = _nbytes((tm, d), BF16) + _nbytes((tm, 6 * sw), F32)
    return pl.pallas_call(
        functools.partial(_ln_proj_kernel, layer=layer),
        grid=(t // tm, nj),
        in_specs=[
            pl.BlockSpec((tm, d), lambda i, j: (i, 0)),
            pl.BlockSpec((1, d), lambda i, j: (0, 0)),
            pl.BlockSpec((1, d), lambda i, j: (0, 0)),
            pl.BlockSpec((slots, sw), lambda i, j: (0, j)),
            pl.BlockSpec((slots, sw), lambda i, j: (0, j)),
            wspec(2), wspec(3), wspec(0), wspec(1), wspec(4), wspec(5),
        ],
        out_specs=[
            pl.BlockSpec((tm, 2 * sw), lambda i, j: (i, j)),
            pl.BlockSpec((tm, 2 * sw), lambda i, j: (i, j)),
            pl.BlockSpec((tm, 3 * sw), lambda i, j: (i, j)),
            pl.BlockSpec((tm, sw), lambda i, j: (i, j)),
        ],
        out_shape=[
            jax.ShapeDtypeStruct((t, nj * 2 * sw), F32),
            jax.ShapeDtypeStruct((t, nj * 2 * sw), BF16),
            jax.ShapeDtypeStruct((t, nj * 3 * sw), BF16),
            jax.ShapeDtypeStruct((t, nj * sw), BF16),
        ],
        scratch_shapes=[pltpu.VMEM((tm, d), BF16)],
        compiler_params=pltpu.CompilerParams(
            dimension_semantics=("parallel", "arbitrary"),
            vmem_limit_bytes=_vmem_limit(pipelined, resident)),
        name="ln_proj",
    )(x, g, b, lbf_logits, lbb_logits, w, w, w, w, w, w)


def _hgrn_kernel(gn_ref, q_ref, v_ref, g_ref, lf_ref, kk_ref, o_ref, st_ref, of_ref, *, nblk, blk_rows):
    c = HGRN_CHUNK
    d = HEAD_DIM
    half = c // 2
    nchunk = blk_rows // c
    heads = st_ref.shape[0]
    phase = pl.program_id(1)
    t = pl.program_id(2)

    @pl.when(t == 0)
    def _():
        st_ref[...] = jnp.zeros_like(st_ref)

    def scan_block(fwd):
        row = lax.broadcasted_iota(jnp.int32, (c, c), 0)
        col = lax.broadcasted_iota(jnp.int32, (c, c), 1)
        tri = (col <= row) if fwd else (col >= row)
        tri2 = jnp.concatenate([tri.astype(BF16)] * 2, axis=1)
        blk = t if fwd else nblk - 1 - t
        chunk_rows = [pl.ds((ci if fwd else nchunk - 1 - ci) * c, c) for ci in range(nchunk)]
        head_cols = [slice(h * d, (h + 1) * d) for h in range(heads)]
        trans_a = (((0,), (0,)), ((), ()))

        bs = []
        for rows in chunk_rows:
            lf = lf_ref[rows, :]
            hi = lf.astype(BF16)
            lo = (lf - hi.astype(F32)).astype(BF16)
            bs.append(_dot(tri2, jnp.concatenate([hi, lo], axis=0)))

        qds, dsts, scs, decs, mids = [], [], [], [], []
        for rows, b in zip(chunk_rows, bs):
            b_mid = b[half - 1:half, :] if fwd else b[half:half + 1, :]
            b_end = b[c - 1:c, :] if fwd else b[0:1, :]
            kk = kk_ref[rows, :].astype(F32)
            qd = (q_ref[rows, :].astype(F32) * jnp.exp2(b - b_mid)).astype(BF16)
            ki = kk * jnp.exp2(b_mid - b)
            ke = (kk * jnp.exp2(b_end - b)).astype(BF16)
            v = v_ref[rows, :]
            qds.append(qd)
            scs.append([_dot(qd[:, hs], ki[:, hs].T.astype(BF16)) for hs in head_cols])
            dsts.append([lax.dot_general(v[:, hs], ke[:, hs], trans_a, preferred_element_type=F32)
                         for hs in head_cols])
            decs.append(jnp.exp2(b_end))
            mids.append(jnp.exp2(b_mid))

        s_mids = [[] for _ in chunk_rows]
        for h, hs in enumerate(head_cols):
            st = st_ref[h]
            for ci in range(nchunk):
                s_mids[ci].append((st * mids[ci][:, hs]).T.astype(BF16))
                st = decs[ci][:, hs] * st + dsts[ci][h]
            st_ref[h] = st

        for ci, rows in enumerate(chunk_rows):
            v = v_ref[rows, :]
            outs = []
            for h, hs in enumerate(head_cols):
                scores = jnp.where(tri, scs[ci][h], 0.0).astype(BF16)
                o = _dot(scores, v[:, hs]) + _dot(qds[ci][:, hs], s_mids[ci][h])
                if not fwd:
                    o = o + of_ref[pl.ds(pl.multiple_of(blk * blk_rows, c) + rows.start, c), hs]
                    o = o * lax.rsqrt(jnp.mean(o * o, axis=-1, keepdims=True) + RMS_EPS) * gn_ref[...]
                outs.append(o)
            o = jnp.concatenate(outs, axis=1)
            if fwd:
                of_ref[pl.ds(pl.multiple_of(blk * blk_rows, c) + rows.start, c), :] = o
            else:
                o_ref[rows, :] = (o * g_ref[rows, :].astype(F32)).astype(BF16)

    pl.when(phase == 0)(functools.partial(scan_block, True))
    pl.when(phase == 1)(functools.partial(scan_block, False))


def _hgrn(g_norm, qig, log2f, one_minus_f, *, blk_rows=4096):
    t = log2f.shape[0]
    nj = LN_PROJ_STEPS
    sw = log2f.shape[1] // (2 * nj)
    heads = sw // HEAD_DIM
    nblk = t // blk_rows
    d = HEAD_DIM

    tblk = lambda p, i: i + p * (nblk - 1 - 2 * i)
    qspec = lambda sec: pl.BlockSpec((blk_rows, sw), lambda j, p, i: (tblk(p, i), 3 * j + sec))
    fspec = pl.BlockSpec((blk_rows, sw), lambda j, p, i: (tblk(p, i), 2 * j + p))

    pipelined = 5 * _nbytes((blk_rows, sw), BF16) + _nbytes((blk_rows, sw), F32)
    resident = _nbytes((t, sw), F32) + 128 * _nbytes((d, d), F32)
    return pl.pallas_call(
        functools.partial(_hgrn_kernel, nblk=nblk, blk_rows=blk_rows),
        grid=(nj, 2, nblk),
        in_specs=[
            pl.BlockSpec((1, d), lambda j, p, i: (0, 0)),
            qspec(0), qspec(1),
            pl.BlockSpec((blk_rows, sw), lambda j, p, i: (nblk - 1 - p * i, 3 * j + 2)),
            fspec, fspec,
        ],
        out_specs=pl.BlockSpec((blk_rows, sw), lambda j, p, i: (nblk - 1 - p * i, j)),
        out_shape=jax.ShapeDtypeStruct((t, nj * sw), BF16),
        scratch_shapes=[pltpu.VMEM((heads, d, d), F32), pltpu.VMEM((t, sw), F32)],
        compiler_params=pltpu.CompilerParams(
            dimension_semantics=("parallel", "arbitrary", "arbitrary"),
            vmem_limit_bytes=_vmem_limit(pipelined, resident)),
        name="hgrn",
    )(g_norm, qig, qig, qig, log2f, one_minus_f)


def _dft_constants(t, n1):
    n2 = t // n1
    k1 = np.arange(n1)[None, :, None]
    t1 = np.arange(n1)[None, None, :]
    t2 = np.arange(n2)[:, None, None]
    ang = 2.0 * np.pi * ((k1 * (n2 * t1 + t2)) % t) / t
    g = np.concatenate([np.cos(ang), -np.sin(ang)], axis=1) / math.sqrt(n1)
    k2 = np.arange(n2)[:, None]
    s2 = np.arange(n2)[None, :]
    ang2 = 2.0 * np.pi * ((k2 * s2) % n2) / n2
    c2, sn2 = np.cos(ang2) / math.sqrt(n2), np.sin(ang2) / math.sqrt(n2)
    f2 = np.block([[c2, sn2], [-sn2, c2]])
    m = np.arange(FOURIER_GDIM)
    ang3 = 2.0 * np.pi * ((m[:, None] * m[None, :]) % FOURIER_GDIM) / FOURIER_GDIM
    c3 = np.cos(ang3) / math.sqrt(FOURIER_GDIM)
    s3 = np.sin(ang3) / math.sqrt(FOURIER_GDIM)
    as_bf16 = lambda a: jnp.asarray(a, dtype=F32).astype(BF16)
    return as_bf16(g), as_bf16(f2), as_bf16(c3), as_bf16(s3)


def _fft_t1_kernel(u_ref, g_ref, w_ref, *, tb, n1):
    x = pltpu.einshape("tjm->(jt)m", u_ref[...])
    r = jnp.stack([_dot(g_ref[j], x[j * n1:(j + 1) * n1]).astype(BF16) for j in range(tb)])
    r = pltpu.einshape("jam->ajm", r)
    w_ref[0] = r[:n1]
    w_ref[1] = r[n1:]


def _fft_t1(four, g, *, tb=BF16_ROWS):
    t, m = four.shape
    n2, _, n1 = g.shape
    u = four.reshape(n1, n2, m)
    pipelined = (_nbytes((n1, tb, m), BF16) + _nbytes((tb, 2 * n1, n1), BF16)
                 + _nbytes((2, n1, tb, m), BF16))
    return pl.pallas_call(
        functools.partial(_fft_t1_kernel, tb=tb, n1=n1),
        grid=(n2 // tb,),
        in_specs=[
            pl.BlockSpec((n1, tb, m), lambda s: (0, s, 0)),
            pl.BlockSpec((tb, 2 * n1, n1), lambda s: (s, 0, 0)),
        ],
        out_specs=pl.BlockSpec((2, n1, tb, m), lambda s: (0, 0, s, 0)),
        out_shape=jax.ShapeDtypeStruct((2, n1, n2, m), BF16),
        compiler_params=pltpu.CompilerParams(
            dimension_semantics=("parallel",),
            vmem_limit_bytes=_vmem_limit(pipelined, 6 * _nbytes((tb * n1, m), F32))),
        name="fft_t1",
    )(u, g)


def _fft_t2_kernel(f_ref, w_ref, p_ref, *, kb, n2):
    for j in range(kb):
        rhs = jnp.concatenate([w_ref[0, j], w_ref[1, j]], axis=0)
        res = _dot(f_ref[...], rhs)
        p_ref[0, j] = res[:n2].astype(BF16)
        p_ref[1, j] = res[n2:].astype(BF16)


def _fft_t2(w4, f2, *, kb=4):
    _, n1, n2, m = w4.shape
    blk = (2, kb, n2, m)
    pipelined = 2 * _nbytes(blk, BF16) + _nbytes((2 * n2, 2 * n2), BF16)
    return pl.pallas_call(
        functools.partial(_fft_t2_kernel, kb=kb, n2=n2),
        grid=(n1 // kb,),
        in_specs=[
            pl.BlockSpec((2 * n2, 2 * n2), lambda s: (0, 0)),
            pl.BlockSpec(blk, lambda s: (0, s, 0, 0)),
        ],
        out_specs=pl.BlockSpec(blk, lambda s: (0, s, 0, 0)),
        out_shape=jax.ShapeDtypeStruct((2, n1, n2, m), BF16),
        compiler_params=pltpu.CompilerParams(
            dimension_semantics=("parallel",),
            vmem_limit_bytes=_vmem_limit(pipelined, 4 * _nbytes((2 * n2, m), F32))),
        name="fft_t2",
    )(f2, w4)


def _mix_ln_kernel(x_ref, ge_ref, be_ref, oh_ref, pr_ref, pi_ref, c3_ref, s3_ref, wo_ref,
                   g1_ref, b1_ref, h1_ref, pr_scr, pi_scr, *, alpha):
    gd = FOURIER_GDIM
    pr_scr[...] = pltpu.einshape("akm->(ka)m", pr_ref[...])
    pi_scr[...] = pltpu.einshape("akm->(ka)m", pi_ref[...])
    c3, s3 = c3_ref[...], s3_ref[...]
    parts = []
    for g in range(pr_scr.shape[1] // gd):
        sl = slice(g * gd, (g + 1) * gd)
        parts.append((_dot(pr_scr[:, sl], c3) + _dot(pi_scr[:, sl], s3)).astype(BF16))
    o_four = jnp.concatenate(parts, axis=1)
    wh = oh_ref.shape[1]
    mix = _dot(oh_ref[...], wo_ref[:wh, :]) + _dot(o_four, wo_ref[wh:, :])
    h0 = _layernorm(x_ref[...], ge_ref[...], be_ref[...])
    h1_ref[...] = _layernorm(alpha * h0 + mix, g1_ref[...], b1_ref[...])


def _mix_ln(x, ge, be, o_hgrn, p, c3, s3, wo, g1, b1, *, alpha, tm=512):
    t, d = x.shape
    wh = o_hgrn.shape[1]
    _, n1, n2, m = p.shape
    kr = tm // n1
    gd = FOURIER_GDIM
    const = lambda shape: pl.BlockSpec(shape, lambda i: tuple(0 for _ in shape),
                                       pipeline_mode=pl.Buffered(1))
    pipelined = (_nbytes((tm, d), F32) + _nbytes((tm, wh), BF16) + 2 * _nbytes((n1, kr, m), BF16)
                 + _nbytes((tm, d), F32))
    resident = (_nbytes((wh + m, d), BF16) + 2 * _nbytes((tm, m), BF16) + 2 * _nbytes((tm, m), F32)
                + 3 * _nbytes((tm, d), F32))
    return pl.pallas_call(
        functools.partial(_mix_ln_kernel, alpha=alpha),
        grid=(t // tm,),
        in_specs=[
            pl.BlockSpec((tm, d), lambda i: (i, 0)),
            const((1, d)), const((1, d)),
            pl.BlockSpec((tm, wh), lambda i: (i, 0)),
            pl.BlockSpec((None, n1, kr, m), lambda i: (0, 0, i, 0)),
            pl.BlockSpec((None, n1, kr, m), lambda i: (1, 0, i, 0)),
            const((gd, gd)), const((gd, gd)),
            const((wh + m, d)),
            const((1, d)), const((1, d)),
        ],
        out_specs=pl.BlockSpec((tm, d), lambda i: (i, 0)),
        out_shape=jax.ShapeDtypeStruct((t, d), F32),
        scratch_shapes=[pltpu.VMEM((tm, m), BF16), pltpu.VMEM((tm, m), BF16)],
        compiler_params=pltpu.CompilerParams(
            dimension_semantics=("parallel",),
            vmem_limit_bytes=_vmem_limit(pipelined, resident)),
        name="mix_ln",
    )(x, ge, be, o_hgrn, p, p, c3, s3, wo, g1, b1)


def _ffn_ln_kernel(h1_ref, wg_ref, wu_ref, wd_ref, g2_ref, b2_ref, o_ref, hb_scr, *, alpha):
    f = pl.program_id(1)

    @pl.when(f == 0)
    def _():
        hb_scr[...] = h1_ref[...].astype(BF16)
        o_ref[...] = jnp.zeros_like(o_ref)

    hb = hb_scr[...]
    gate = _dot(hb, wg_ref[...].astype(BF16))
    up = _dot(hb, wu_ref[...].astype(BF16))
    act = (gate * _sigmoid(gate) * up).astype(BF16)
    cw = FFN_OUT_SLAB
    for n in range(o_ref.shape[1] // cw):
        cols = slice(n * cw, (n + 1) * cw)
        o_ref[:, cols] += _dot(act, wd_ref[:, cols].astype(BF16))

    @pl.when(f == pl.num_programs(1) - 1)
    def _():
        for r in range(0, o_ref.shape[0], LN_ROWS):
            rows = slice(r, r + LN_ROWS)
            o_ref[rows, :] = _layernorm(alpha * h1_ref[rows, :] + o_ref[rows, :], g2_ref[...], b2_ref[...])


def _ffn_ln(h1, wg, wu, wd, g2, b2, *, alpha, tm=1024, tf=512):
    t, d = h1.shape
    dff = wg.shape[1]
    pipelined = _nbytes((tm, d), F32) + 3 * _nbytes((d, tf), F32)
    resident = (_nbytes((tm, d), F32) + _nbytes((tm, d), BF16) + 2 * _nbytes((tm, tf), F32)
                + 2 * _nbytes((tm, FFN_OUT_SLAB), F32))
    return pl.pallas_call(
        functools.partial(_ffn_ln_kernel, alpha=alpha),
        grid=(t // tm, dff // tf),
        in_specs=[
            pl.BlockSpec((tm, d), lambda i, f: (i, 0)),
            pl.BlockSpec((d, tf), lambda i, f: (0, f)),
            pl.BlockSpec((d, tf), lambda i, f: (0, f)),
            pl.BlockSpec((tf, d), lambda i, f: (f, 0)),
            pl.BlockSpec((1, d), lambda i, f: (0, 0)),
            pl.BlockSpec((1, d), lambda i, f: (0, 0)),
        ],
        out_specs=pl.BlockSpec((tm, d), lambda i, f: (i, 0), pipeline_mode=pl.Buffered(1)),
        out_shape=jax.ShapeDtypeStruct((t, d), F32),
        scratch_shapes=[pltpu.VMEM((tm, d), BF16)],
        compiler_params=pltpu.CompilerParams(
            dimension_semantics=("parallel", "arbitrary"),
            vmem_limit_bytes=_vmem_limit(pipelined, resident)),
        name="ffn_ln",
    )(h1, wg, wu, wd, g2, b2)


def kernel(x, ln_emb_g, ln_emb_b, w_in, lb_fwd_logits, lb_bwd_logits, g_norm, w_out, ln1_g, ln1_b,
           w_gate, w_up, w_down, ln2_g, ln2_b):
    depth = w_in.shape[0]
    assert depth == 1, "the embedding LayerNorm is fused into the single layer's projection"
    batch, seq, d = x.shape
    assert batch == 1
    alpha = (2.0 * depth) ** 0.25
    layer = 0
    row = lambda a: a.reshape(1, -1).astype(F32)

    x2 = x.reshape(seq, d)
    log2f, one_minus_f, qig, four = _ln_proj(
        x2, row(ln_emb_g), row(ln_emb_b), lb_fwd_logits.astype(F32), lb_bwd_logits.astype(F32),
        w_in[layer], layer=layer)

    o_hgrn = _hgrn(row(g_norm[layer]), qig, log2f, one_minus_f)

    g1c, f2c, c3, s3 = _dft_constants(seq, FFT_N1)
    p = _fft_t2(_fft_t1(four, g1c), f2c)

    wo = w_out[layer].astype(BF16)
    h1 = _mix_ln(x2, row(ln_emb_g), row(ln_emb_b), o_hgrn, p, c3, s3, wo,
                 row(ln1_g[layer]), row(ln1_b[layer]), alpha=alpha)

    out = _ffn_ln(h1, w_gate[layer], w_up[layer], w_down[layer],
                  row(ln2_g[layer]), row(ln2_b[layer]), alpha=alpha)
    return out.reshape(batch, seq, d)
```

```python
import functools
import math

import numpy as np
import jax
import jax.numpy as jnp
from jax import lax
from jax.experimental import pallas as pl
from jax.experimental.pallas import tpu as pltpu

LN_EPS = 1e-5
RMS_EPS = 1e-6
HEAD_DIM = 128
FOURIER_GDIM = 256
FFT_N1 = 32
BF16_ROWS = 16
LN_ROWS = 256
FFN_OUT_SLAB = 512
HGRN_CHUNK = 128
LN_PROJ_STEPS = 4
V7X_VMEM_CAP = 63 * 1024 * 1024

BF16 = jnp.bfloat16
F32 = jnp.float32


def _vmem_limit(pipelined_bytes, resident_bytes):
    return int(min(V7X_VMEM_CAP, 2 * pipelined_bytes + resident_bytes + (4 << 20)))


def _nbytes(shape, dtype):
    return int(np.prod(shape)) * jnp.dtype(dtype).itemsize


def _dot(a, b):
    return jnp.dot(a, b, preferred_element_type=F32)


def _layernorm(x, g, b):
    mu = jnp.mean(x, axis=-1, keepdims=True)
    xc = x - mu
    var = jnp.mean(xc * xc, axis=-1, keepdims=True)
    return xc * lax.rsqrt(var + LN_EPS) * g + b


def _sigmoid(x):
    return 1.0 / (1.0 + jnp.exp(-x))


def _silu(x):
    return x * _sigmoid(x)


def _lower_bound(logits_ref, layer):
    logits = logits_ref[...]
    e = jnp.exp(logits - jnp.max(logits, axis=0, keepdims=True))
    return jnp.sum(e[:layer + 1], axis=0, keepdims=True) / jnp.sum(e, axis=0, keepdims=True)


def _ln_proj_kernel(x_ref, g_ref, b_ref, lbf_ref, lbb_ref, wzf_ref, wzb_ref, wq_ref, wi_ref, wg_ref, wf_ref,
                    lf_ref, kk_ref, q_ref, f_ref, h_even, h_odd, *, layer, ntiles):
    r = pl.program_id(0)
    j = pl.program_id(1)
    qrows = x_ref.shape[0]

    def normalise_quarter(h_dst):
        rows = pl.ds(pl.multiple_of(j * qrows, qrows), qrows)
        y = _layernorm(x_ref[...], g_ref[...], b_ref[...]).astype(BF16)
        h_dst[rows, :] = y
        bits = pltpu.bitcast(y, jnp.uint32)
        acc = bits[0:8, :]
        for s in range(8, bits.shape[0], 8):
            acc = acc | bits[s:s + 8, :]
        zero = ((acc >> 16) >> 16).astype(F32)
        return zero[0:1, 0:wf_ref.shape[1]]

    def project_slab(h_src, zero=None):
        h = h_src[...]
        sw = wf_ref.shape[1]
        proj = lambda w_ref: _dot(h, w_ref[...].astype(BF16))
        for k, (w_ref, lb_ref) in enumerate(((wzf_ref, lbf_ref), (wzb_ref, lbb_ref))):
            lb = _lower_bound(lb_ref, layer)
            if zero is not None and k == 0:
                lb = lb + zero
            f = lb + (1.0 - lb) * _sigmoid(proj(w_ref))
            lf_ref[:, k * sw:(k + 1) * sw] = jnp.log2(f)
            kk_ref[:, k * sw:(k + 1) * sw] = (1.0 - f).astype(BF16)
        q_ref[:, 0:sw] = _silu(proj(wq_ref)).astype(BF16)
        q_ref[:, sw:2 * sw] = proj(wi_ref).astype(BF16)
        q_ref[:, 2 * sw:3 * sw] = _silu(proj(wg_ref)).astype(BF16)
        f_ref[...] = proj(wf_ref).astype(BF16)

    @pl.when(r == 0)
    def _():
        normalise_quarter(h_even)

    middle = jnp.logical_and(r > 0, r < ntiles)

    @pl.when(jnp.logical_and(middle, r % 2 == 1))
    def _():
        project_slab(h_even, normalise_quarter(h_odd))

    @pl.when(jnp.logical_and(middle, r % 2 == 0))
    def _():
        project_slab(h_odd, normalise_quarter(h_even))

    pl.when(r == ntiles)(functools.partial(project_slab, h_odd if ntiles % 2 == 0 else h_even))


def _ln_proj(x, g, b, lbf_logits, lbb_logits, w, *, layer, tm=1024):
    t, d = x.shape
    nj = LN_PROJ_STEPS
    sw = w.shape[1] // (6 * nj)
    slots = lbf_logits.shape[0]
    ntiles = t // tm
    qrows = tm // nj
    col = lambda r, j: jnp.where(r == 0, 0, j)
    wspec = lambda section: pl.BlockSpec((d, sw), lambda r, j: (0, section * nj + col(r, j)))
    ospec = lambda width: pl.BlockSpec((tm, width), lambda r, j: (jnp.maximum(r - 1, 0), col(r, j)))
    pipelined = (_nbytes((qrows, d), F32) + _nbytes((d, 6 * sw), F32)
                 + _nbytes((tm, 2 * sw), F32) + _nbytes((tm, 6 * sw), BF16))
    resident = 2 * _nbytes((tm, d), BF16) + _nbytes((tm, 6 * sw), F32)
    return pl.pallas_call(
        functools.partial(_ln_proj_kernel, layer=layer, ntiles=ntiles),
        grid=(ntiles + 1, nj),
        in_specs=[
            pl.BlockSpec((qrows, d), lambda r, j: (jnp.minimum(r, ntiles - 1) * nj + j, 0)),
            pl.BlockSpec((1, d), lambda r, j: (0, 0)),
            pl.BlockSpec((1, d), lambda r, j: (0, 0)),
            pl.BlockSpec((slots, sw), lambda r, j: (0, j)),
            pl.BlockSpec((slots, sw), lambda r, j: (0, j)),
            wspec(2), wspec(3), wspec(0), wspec(1), wspec(4), wspec(5),
        ],
        out_specs=[ospec(2 * sw), ospec(2 * sw), ospec(3 * sw), ospec(sw)],
        out_shape=[
            jax.ShapeDtypeStruct((t, nj * 2 * sw), F32),
            jax.ShapeDtypeStruct((t, nj * 2 * sw), BF16),
            jax.ShapeDtypeStruct((t, nj * 3 * sw), BF16),
            jax.ShapeDtypeStruct((t, nj * sw), BF16),
        ],
        scratch_shapes=[pltpu.VMEM((tm, d), BF16), pltpu.VMEM((tm, d), BF16)],
        compiler_params=pltpu.CompilerParams(
            dimension_semantics=("arbitrary", "arbitrary"),
            vmem_limit_bytes=_vmem_limit(pipelined, resident)),
        name="ln_proj",
    )(x, g, b, lbf_logits, lbb_logits, w, w, w, w, w, w)


def _hgrn_kernel(gn_ref, q_ref, v_ref, g_ref, lf_ref, kk_ref, o_ref, st_ref, of_ref, *, nblk, blk_rows):
    c = HGRN_CHUNK
    d = HEAD_DIM
    half = c // 2
    nchunk = blk_rows // c
    heads = st_ref.shape[0]
    phase = pl.program_id(1)
    t = pl.program_id(2)

    @pl.when(t == 0)
    def _():
        st_ref[...] = jnp.zeros_like(st_ref)

    def scan_block(fwd):
        row = lax.broadcasted_iota(jnp.int32, (c, c), 0)
        col = lax.broadcasted_iota(jnp.int32, (c, c), 1)
        tri = (col <= row) if fwd else (col >= row)
        tri2 = jnp.concatenate([tri.astype(BF16)] * 2, axis=1)
        blk = t if fwd else nblk - 1 - t
        chunk_rows = [pl.ds((ci if fwd else nchunk - 1 - ci) * c, c) for ci in range(nchunk)]
        head_cols = [slice(h * d, (h + 1) * d) for h in range(heads)]
        trans_a = (((0,), (0,)), ((), ()))

        bs = []
        for rows in chunk_rows:
            lf = lf_ref[rows, :]
            hi = lf.astype(BF16)
            lo = (lf - hi.astype(F32)).astype(BF16)
            bs.append(_dot(tri2, jnp.concatenate([hi, lo], axis=0)))

        qds, dsts, scs, decs, mids = [], [], [], [], []
        for rows, b in zip(chunk_rows, bs):
            b_mid = b[half - 1:half, :] if fwd else b[half:half + 1, :]
            b_end = b[c - 1:c, :] if fwd else b[0:1, :]
            kk = kk_ref[rows, :].astype(F32)
            qd = (q_ref[rows, :].astype(F32) * jnp.exp2(b - b_mid)).astype(BF16)
            ki = kk * jnp.exp2(b_mid - b)
            ke = (kk * jnp.exp2(b_end - b)).astype(BF16)
            v = v_ref[rows, :]
            qds.append(qd)
            scs.append([_dot(qd[:, hs], ki[:, hs].T.astype(BF16)) for hs in head_cols])
            dsts.append([lax.dot_general(v[:, hs], ke[:, hs], trans_a, preferred_element_type=F32)
                         for hs in head_cols])
            decs.append(jnp.exp2(b_end))
            mids.append(jnp.exp2(b_mid))

        s_mids = [[] for _ in chunk_rows]
        for h, hs in enumerate(head_cols):
            st = st_ref[h]
            for ci in range(nchunk):
                s_mids[ci].append((st * mids[ci][:, hs]).T.astype(BF16))
                st = decs[ci][:, hs] * st + dsts[ci][h]
            st_ref[h] = st

        for ci, rows in enumerate(chunk_rows):
            v = v_ref[rows, :]
            outs = []
            for h, hs in enumerate(head_cols):
                scores = jnp.where(tri, scs[ci][h], 0.0).astype(BF16)
                o = _dot(scores, v[:, hs]) + _dot(qds[ci][:, hs], s_mids[ci][h])
                if not fwd:
                    o = o + of_ref[pl.ds(pl.multiple_of(blk * blk_rows, c) + rows.start, c), hs]
                    o = o * lax.rsqrt(jnp.mean(o * o, axis=-1, keepdims=True) + RMS_EPS) * gn_ref[...]
                outs.append(o)
            o = jnp.concatenate(outs, axis=1)
            if fwd:
                of_ref[pl.ds(pl.multiple_of(blk * blk_rows, c) + rows.start, c), :] = o
            else:
                o_ref[rows, :] = (o * g_ref[rows, :].astype(F32)).astype(BF16)

    pl.when(phase == 0)(functools.partial(scan_block, True))
    pl.when(phase == 1)(functools.partial(scan_block, False))


def _hgrn(g_norm, qig, log2f, one_minus_f, *, blk_rows=4096):
    t = log2f.shape[0]
    nj = LN_PROJ_STEPS
    sw = log2f.shape[1] // (2 * nj)
    heads = sw // HEAD_DIM
    nblk = t // blk_rows
    d = HEAD_DIM

    tblk = lambda p, i: i + p * (nblk - 1 - 2 * i)
    qspec = lambda sec: pl.BlockSpec((blk_rows, sw), lambda j, p, i: (tblk(p, i), 3 * j + sec))
    fspec = pl.BlockSpec((blk_rows, sw), lambda j, p, i: (tblk(p, i), 2 * j + p))

    pipelined = 5 * _nbytes((blk_rows, sw), BF16) + _nbytes((blk_rows, sw), F32)
    resident = _nbytes((t, sw), F32) + 128 * _nbytes((d, d), F32)
    return pl.pallas_call(
        functools.partial(_hgrn_kernel, nblk=nblk, blk_rows=blk_rows),
        grid=(nj, 2, nblk),
        in_specs=[
            pl.BlockSpec((1, d), lambda j, p, i: (0, 0)),
            qspec(0), qspec(1),
            pl.BlockSpec((blk_rows, sw), lambda j, p, i: (nblk - 1 - p * i, 3 * j + 2)),
            fspec, fspec,
        ],
        out_specs=pl.BlockSpec((blk_rows, sw), lambda j, p, i: (nblk - 1 - p * i, j)),
        out_shape=jax.ShapeDtypeStruct((t, nj * sw), BF16),
        scratch_shapes=[pltpu.VMEM((heads, d, d), F32), pltpu.VMEM((t, sw), F32)],
        compiler_params=pltpu.CompilerParams(
            dimension_semantics=("parallel", "arbitrary", "arbitrary"),
            vmem_limit_bytes=_vmem_limit(pipelined, resident)),
        name="hgrn",
    )(g_norm, qig, qig, qig, log2f, one_minus_f)


def _dft_constants(t, n1):
    n2 = t // n1
    k1 = np.arange(n1)[None, :, None]
    t1 = np.arange(n1)[None, None, :]
    t2 = np.arange(n2)[:, None, None]
    ang = 2.0 * np.pi * ((k1 * (n2 * t1 + t2)) % t) / t
    g = np.concatenate([np.cos(ang), -np.sin(ang)], axis=1) / math.sqrt(n1)
    k2 = np.arange(n2)[:, None]
    s2 = np.arange(n2)[None, :]
    ang2 = 2.0 * np.pi * ((k2 * s2) % n2) / n2
    c2, sn2 = np.cos(ang2) / math.sqrt(n2), np.sin(ang2) / math.sqrt(n2)
    f2 = np.block([[c2, sn2], [-sn2, c2]])
    m = np.arange(FOURIER_GDIM)
    ang3 = 2.0 * np.pi * ((m[:, None] * m[None, :]) % FOURIER_GDIM) / FOURIER_GDIM
    c3 = np.cos(ang3) / math.sqrt(FOURIER_GDIM)
    s3 = np.sin(ang3) / math.sqrt(FOURIER_GDIM)
    as_bf16 = lambda a: jnp.asarray(a, dtype=F32).astype(BF16)
    return as_bf16(g), as_bf16(f2), as_bf16(c3), as_bf16(s3)


def _fft_t1_kernel(u_ref, g_ref, w_ref, *, tb, n1):
    x = pltpu.einshape("tjm->(jt)m", u_ref[...])
    r = jnp.stack([_dot(g_ref[j], x[j * n1:(j + 1) * n1]).astype(BF16) for j in range(tb)])
    r = pltpu.einshape("jam->ajm", r)
    w_ref[0] = r[:n1]
    w_ref[1] = r[n1:]


def _fft_t1(four, g, *, tb=BF16_ROWS):
    t, m = four.shape
    n2, _, n1 = g.shape
    u = four.reshape(n1, n2, m)
    pipelined = (_nbytes((n1, tb, m), BF16) + _nbytes((tb, 2 * n1, n1), BF16)
                 + _nbytes((2, n1, tb, m), BF16))
    return pl.pallas_call(
        functools.partial(_fft_t1_kernel, tb=tb, n1=n1),
        grid=(n2 // tb,),
        in_specs=[
            pl.BlockSpec((n1, tb, m), lambda s: (0, s, 0)),
            pl.BlockSpec((tb, 2 * n1, n1), lambda s: (s, 0, 0)),
        ],
        out_specs=pl.BlockSpec((2, n1, tb, m), lambda s: (0, 0, s, 0)),
        out_shape=jax.ShapeDtypeStruct((2, n1, n2, m), BF16),
        compiler_params=pltpu.CompilerParams(
            dimension_semantics=("parallel",),
            vmem_limit_bytes=_vmem_limit(pipelined, 6 * _nbytes((tb * n1, m), F32))),
        name="fft_t1",
    )(u, g)


def _fft_t2_kernel(f_ref, w_ref, p_ref, *, kb, n2):
    for j in range(kb):
        rhs = jnp.concatenate([w_ref[0, j], w_ref[1, j]], axis=0)
        res = _dot(f_ref[...], rhs)
        p_ref[0, j] = res[:n2].astype(BF16)
        p_ref[1, j] = res[n2:].astype(BF16)


def _fft_t2(w4, f2, *, kb=4):
    _, n1, n2, m = w4.shape
    blk = (2, kb, n2, m)
    pipelined = 2 * _nbytes(blk, BF16) + _nbytes((2 * n2, 2 * n2), BF16)
    return pl.pallas_call(
        functools.partial(_fft_t2_kernel, kb=kb, n2=n2),
        grid=(n1 // kb,),
        in_specs=[
            pl.BlockSpec((2 * n2, 2 * n2), lambda s: (0, 0)),
            pl.BlockSpec(blk, lambda s: (0, s, 0, 0)),
        ],
        out_specs=pl.BlockSpec(blk, lambda s: (0, s, 0, 0)),
        out_shape=jax.ShapeDtypeStruct((2, n1, n2, m), BF16),
        compiler_params=pltpu.CompilerParams(
            dimension_semantics=("parallel",),
            vmem_limit_bytes=_vmem_limit(pipelined, 4 * _nbytes((2 * n2, m), F32))),
        name="fft_t2",
    )(f2, w4)


def _mix_ln_kernel(x_ref, ge_ref, be_ref, oh_ref, pr_ref, pi_ref, c3_ref, s3_ref, wo_ref,
                   g1_ref, b1_ref, h1_ref, pr_scr, pi_scr, *, alpha):
    gd = FOURIER_GDIM
    pr_scr[...] = pltpu.einshape("akm->(ka)m", pr_ref[...])
    pi_scr[...] = pltpu.einshape("akm->(ka)m", pi_ref[...])
    c3, s3 = c3_ref[...], s3_ref[...]
    parts = []
    for g in range(pr_scr.shape[1] // gd):
        sl = slice(g * gd, (g + 1) * gd)
        parts.append((_dot(pr_scr[:, sl], c3) + _dot(pi_scr[:, sl], s3)).astype(BF16))
    o_four = jnp.concatenate(parts, axis=1)
    wh = oh_ref.shape[1]
    mix = _dot(oh_ref[...], wo_ref[:wh, :]) + _dot(o_four, wo_ref[wh:, :])
    h0 = _layernorm(x_ref[...], ge_ref[...], be_ref[...])
    h1_ref[...] = _layernorm(alpha * h0 + mix, g1_ref[...], b1_ref[...])


def _mix_ln(x, ge, be, o_hgrn, p, c3, s3, wo, g1, b1, *, alpha, tm=512):
    t, d = x.shape
    wh = o_hgrn.shape[1]
    _, n1, n2, m = p.shape
    kr = tm // n1
    gd = FOURIER_GDIM
    const = lambda shape: pl.BlockSpec(shape, lambda i: tuple(0 for _ in shape),
                                       pipeline_mode=pl.Buffered(1))
    pipelined = (_nbytes((tm, d), F32) + _nbytes((tm, wh), BF16) + 2 * _nbytes((n1, kr, m), BF16)
                 + _nbytes((tm, d), F32))
    resident = (_nbytes((wh + m, d), BF16) + 2 * _nbytes((tm, m), BF16) + 2 * _nbytes((tm, m), F32)
                + 3 * _nbytes((tm, d), F32))
    return pl.pallas_call(
        functools.partial(_mix_ln_kernel, alpha=alpha),
        grid=(t // tm,),
        in_specs=[
            pl.BlockSpec((tm, d), lambda i: (i, 0)),
            const((1, d)), const((1, d)),
            pl.BlockSpec((tm, wh), lambda i: (i, 0)),
            pl.BlockSpec((None, n1, kr, m), lambda i: (0, 0, i, 0)),
            pl.BlockSpec((None, n1, kr, m), lambda i: (1, 0, i, 0)),
            const((gd, gd)), const((gd, gd)),
            const((wh + m, d)),
            const((1, d)), const((1, d)),
        ],
        out_specs=pl.BlockSpec((tm, d), lambda i: (i, 0)),
        out_shape=jax.ShapeDtypeStruct((t, d), F32),
        scratch_shapes=[pltpu.VMEM((tm, m), BF16), pltpu.VMEM((tm, m), BF16)],
        compiler_params=pltpu.CompilerParams(
            dimension_semantics=("parallel",),
            vmem_limit_bytes=_vmem_limit(pipelined, resident)),
        name="mix_ln",
    )(x, ge, be, o_hgrn, p, p, c3, s3, wo, g1, b1)


def _ffn_ln_kernel(h1_ref, wg_ref, wu_ref, wd_ref, g2_ref, b2_ref, o_ref, hb_scr, *, alpha):
    f = pl.program_id(1)

    @pl.when(f == 0)
    def _():
        hb_scr[...] = h1_ref[...].astype(BF16)
        o_ref[...] = jnp.zeros_like(o_ref)

    hb = hb_scr[...]
    gate = _dot(hb, wg_ref[...].astype(BF16))
    up = _dot(hb, wu_ref[...].astype(BF16))
    act = (gate * _sigmoid(gate) * up).astype(BF16)
    cw = FFN_OUT_SLAB
    for n in range(o_ref.shape[1] // cw):
        cols = slice(n * cw, (n + 1) * cw)
        o_ref[:, cols] += _dot(act, wd_ref[:, cols].astype(BF16))

    @pl.when(f == pl.num_programs(1) - 1)
    def _():
        for r in range(0, o_ref.shape[0], LN_ROWS):
            rows = slice(r, r + LN_ROWS)
            o_ref[rows, :] = _layernorm(alpha * h1_ref[rows, :] + o_ref[rows, :], g2_ref[...], b2_ref[...])


def _ffn_ln(h1, wg, wu, wd, g2, b2, *, alpha, tm=1024, tf=512):
    t, d = h1.shape
    dff = wg.shape[1]
    pipelined = _nbytes((tm, d), F32) + 3 * _nbytes((d, tf), F32)
    resident = (_nbytes((tm, d), F32) + _nbytes((tm, d), BF16) + 2 * _nbytes((tm, tf), F32)
                + 2 * _nbytes((tm, FFN_OUT_SLAB), F32))
    return pl.pallas_call(
        functools.partial(_ffn_ln_kernel, alpha=alpha),
        grid=(t // tm, dff // tf),
        in_specs=[
            pl.BlockSpec((tm, d), lambda i, f: (i, 0)),
            pl.BlockSpec((d, tf), lambda i, f: (0, f)),
            pl.BlockSpec((d, tf), lambda i, f: (0, f)),
            pl.BlockSpec((tf, d), lambda i, f: (f, 0)),
            pl.BlockSpec((1, d), lambda i, f: (0, 0)),
            pl.BlockSpec((1, d), lambda i, f: (0, 0)),
        ],
        out_specs=pl.BlockSpec((tm, d), lambda i, f: (i, 0), pipeline_mode=pl.Buffered(1)),
        out_shape=jax.ShapeDtypeStruct((t, d), F32),
        scratch_shapes=[pltpu.VMEM((tm, d), BF16)],
        compiler_params=pltpu.CompilerParams(
            dimension_semantics=("parallel", "arbitrary"),
            vmem_limit_bytes=_vmem_limit(pipelined, resident)),
        name="ffn_ln",
    )(h1, wg, wu, wd, g2, b2)


def kernel(x, ln_emb_g, ln_emb_b, w_in, lb_fwd_logits, lb_bwd_logits, g_norm, w_out, ln1_g, ln1_b,
           w_gate, w_up, w_down, ln2_g, ln2_b):
    depth = w_in.shape[0]
    assert depth == 1, "the embedding LayerNorm is fused into the single layer's projection"
    batch, seq, d = x.shape
    assert batch == 1
    alpha = (2.0 * depth) ** 0.25
    layer = 0
    row = lambda a: a.reshape(1, -1).astype(F32)

    x2 = x.reshape(seq, d)
    log2f, one_minus_f, qig, four = _ln_proj(
        x2, row(ln_emb_g), row(ln_emb_b), lb_fwd_logits.astype(F32), lb_bwd_logits.astype(F32),
        w_in[layer], layer=layer)

    o_hgrn = _hgrn(row(g_norm[layer]), qig, log2f, one_minus_f)

    g1c, f2c, c3, s3 = _dft_constants(seq, FFT_N1)
    p = _fft_t2(_fft_t1(four, g1c), f2c)

    wo = w_out[layer].astype(BF16)
    h1 = _mix_ln(x2, row(ln_emb_g), row(ln_emb_b), o_hgrn, p, c3, s3, wo,
                 row(ln1_g[layer]), row(ln1_b[layer]), alpha=alpha)

    out = _ffn_ln(h1, w_gate[layer], w_up[layer], w_down[layer],
                  row(ln2_g[layer]), row(ln2_b[layer]), alpha=alpha)
    return out.reshape(batch, seq, d)
```

```python
import functools
import math

import numpy as np
import jax
import jax.numpy as jnp
from jax import lax
from jax.experimental import pallas as pl
from jax.experimental.pallas import tpu as pltpu

LN_EPS = 1e-5
RMS_EPS = 1e-6
HEAD_DIM = 128
FOURIER_GDIM = 256
FFT_N1 = 32
BF16_ROWS = 16
LN_ROWS = 256
FFN_OUT_SLAB = 512
HGRN_CHUNK = 128
LN_PROJ_STEPS = 4
V7X_VMEM_CAP = 63 * 1024 * 1024

BF16 = jnp.bfloat16
F32 = jnp.float32


def _vmem_limit(pipelined_bytes, resident_bytes):
    return int(min(V7X_VMEM_CAP, 2 * pipelined_bytes + resident_bytes + (4 << 20)))


def _nbytes(shape, dtype):
    return int(np.prod(shape)) * jnp.dtype(dtype).itemsize


def _dot(a, b):
    return jnp.dot(a, b, preferred_element_type=F32)


def _layernorm(x, g, b):
    mu = jnp.mean(x, axis=-1, keepdims=True)
    xc = x - mu
    var = jnp.mean(xc * xc, axis=-1, keepdims=True)
    return xc * lax.rsqrt(var + LN_EPS) * g + b


def _sigmoid(x):
    return 1.0 / (1.0 + jnp.exp(-x))


def _silu(x):
    return x * _sigmoid(x)


def _lower_bound(logits_ref, layer):
    logits = logits_ref[...]
    e = jnp.exp(logits - jnp.max(logits, axis=0, keepdims=True))
    return jnp.sum(e[:layer + 1], axis=0, keepdims=True) / jnp.sum(e, axis=0, keepdims=True)


def _ln_proj_kernel(x_ref, g_ref, b_ref, lbf_ref, lbb_ref, wzf_ref, wzb_ref, wq_ref, wi_ref, wg_ref, wf_ref,
                    lf_ref, kk_ref, q_ref, f_ref, h_even, h_odd, *, layer, ntiles):
    r = pl.program_id(0)
    j = pl.program_id(1)
    qrows = x_ref.shape[0]

    def normalise_quarter(h_dst):
        rows = pl.ds(pl.multiple_of(j * qrows, qrows), qrows)
        y = _layernorm(x_ref[...], g_ref[...], b_ref[...]).astype(BF16)
        h_dst[rows, :] = y
        bits = pltpu.bitcast(y, jnp.uint32)
        acc = bits[0:8, :]
        for s in range(8, bits.shape[0], 8):
            acc = acc | bits[s:s + 8, :]
        zero = ((acc >> 16) >> 16).astype(F32)
        return zero[0:1, 0:wf_ref.shape[1]]

    def project_slab(h_src, zero=None):
        h = h_src[...]
        sw = wf_ref.shape[1]
        proj = lambda w_ref: _dot(h, w_ref[...].astype(BF16))
        for k, (w_ref, lb_ref) in enumerate(((wzf_ref, lbf_ref), (wzb_ref, lbb_ref))):
            lb = _lower_bound(lb_ref, layer)
            if zero is not None and k == 0:
                lb = lb + zero
            f = lb + (1.0 - lb) * _sigmoid(proj(w_ref))
            lf_ref[:, k * sw:(k + 1) * sw] = jnp.log2(f)
            kk_ref[:, k * sw:(k + 1) * sw] = (1.0 - f).astype(BF16)
        q_ref[:, 0:sw] = _silu(proj(wq_ref)).astype(BF16)
        q_ref[:, sw:2 * sw] = proj(wi_ref).astype(BF16)
        q_ref[:, 2 * sw:3 * sw] = _silu(proj(wg_ref)).astype(BF16)
        f_ref[...] = proj(wf_ref).astype(BF16)

    @pl.when(r == 0)
    def _():
        normalise_quarter(h_even)

    middle = jnp.logical_and(r > 0, r < ntiles)

    @pl.when(jnp.logical_and(middle, r % 2 == 1))
    def _():
        project_slab(h_even, normalise_quarter(h_odd))

    @pl.when(jnp.logical_and(middle, r % 2 == 0))
    def _():
        project_slab(h_odd, normalise_quarter(h_even))

    pl.when(r == ntiles)(functools.partial(project_slab, h_odd if ntiles % 2 == 0 else h_even))


def _ln_proj(x, g, b, lbf_logits, lbb_logits, w, *, layer, tm=1024):
    t, d = x.shape
    nj = LN_PROJ_STEPS
    sw = w.shape[1] // (6 * nj)
    slots = lbf_logits.shape[0]
    ntiles = t // tm
    qrows = tm // nj
    col = lambda r, j: jnp.where(r == 0, 0, j)
    wspec = lambda section: pl.BlockSpec((d, sw), lambda r, j: (0, section * nj + col(r, j)))
    ospec = lambda width: pl.BlockSpec((tm, width), lambda r, j: (jnp.maximum(r - 1, 0), col(r, j)))
    pipelined = (_nbytes((qrows, d), F32) + _nbytes((d, 6 * sw), F32)
                 + _nbytes((tm, 2 * sw), F32) + _nbytes((tm, 6 * sw), BF16))
    resident = 2 * _nbytes((tm, d), BF16) + _nbytes((tm, 6 * sw), F32)
    return pl.pallas_call(
        functools.partial(_ln_proj_kernel, layer=layer, ntiles=ntiles),
        grid=(ntiles + 1, nj),
        in_specs=[
            pl.BlockSpec((qrows, d), lambda r, j: (jnp.minimum(r, ntiles - 1) * nj + j, 0)),
            pl.BlockSpec((1, d), lambda r, j: (0, 0)),
            pl.BlockSpec((1, d), lambda r, j: (0, 0)),
            pl.BlockSpec((slots, sw), lambda r, j: (0, j)),
            pl.BlockSpec((slots, sw), lambda r, j: (0, j)),
            wspec(2), wspec(3), wspec(0), wspec(1), wspec(4), wspec(5),
        ],
        out_specs=[ospec(2 * sw), ospec(2 * sw), ospec(3 * sw), ospec(sw)],
        out_shape=[
            jax.ShapeDtypeStruct((t, nj * 2 * sw), F32),
            jax.ShapeDtypeStruct((t, nj * 2 * sw), BF16),
            jax.ShapeDtypeStruct((t, nj * 3 * sw), BF16),
            jax.ShapeDtypeStruct((t, nj * sw), BF16),
        ],
        scratch_shapes=[pltpu.VMEM((tm, d), BF16), pltpu.VMEM((tm, d), BF16)],
        compiler_params=pltpu.CompilerParams(
            dimension_semantics=("arbitrary", "arbitrary"),
            vmem_limit_bytes=_vmem_limit(pipelined, resident)),
        name="ln_proj",
    )(x, g, b, lbf_logits, lbb_logits, w, w, w, w, w, w)


def _hgrn_kernel(gn_ref, q_ref, v_ref, g_ref, lf_ref, kk_ref, f2_ref, w_ref, o_ref, p_ref, st_ref, of_ref,
                 *, nblk, blk_rows):
    c = HGRN_CHUNK
    d = HEAD_DIM
    half = c // 2
    nchunk = blk_rows // c
    heads = st_ref.shape[0]
    phase = pl.program_id(1)
    t = pl.program_id(2)

    @pl.when(t == 0)
    def _():
        st_ref[...] = jnp.zeros_like(st_ref)

    def scan_block(fwd):
        row = lax.broadcasted_iota(jnp.int32, (c, c), 0)
        col = lax.broadcasted_iota(jnp.int32, (c, c), 1)
        tri = (col <= row) if fwd else (col >= row)
        tri2 = jnp.concatenate([tri.astype(BF16)] * 2, axis=1)
        blk = t if fwd else nblk - 1 - t
        chunk_rows = [pl.ds((ci if fwd else nchunk - 1 - ci) * c, c) for ci in range(nchunk)]
        head_cols = [slice(h * d, (h + 1) * d) for h in range(heads)]
        trans_a = (((0,), (0,)), ((), ()))
        n2 = w_ref.shape[2]

        def dft_slice(k):
            res = _dot(f2_ref[...], jnp.concatenate([w_ref[0, k], w_ref[1, k]], axis=0))
            p_ref[0, k] = res[:n2].astype(BF16)
            p_ref[1, k] = res[n2:].astype(BF16)

        bs = []
        for rows in chunk_rows:
            lf = lf_ref[rows, :]
            hi = lf.astype(BF16)
            lo = (lf - hi.astype(F32)).astype(BF16)
            bs.append(_dot(tri2, jnp.concatenate([hi, lo], axis=0)))
        for k in range(0, w_ref.shape[1], 2):
            dft_slice(k)

        qds, dsts, scs, decs, mids = [], [], [], [], []
        for rows, b in zip(chunk_rows, bs):
            b_mid = b[half - 1:half, :] if fwd else b[half:half + 1, :]
            b_end = b[c - 1:c, :] if fwd else b[0:1, :]
            kk = kk_ref[rows, :].astype(F32)
            qd = (q_ref[rows, :].astype(F32) * jnp.exp2(b - b_mid)).astype(BF16)
            ki = kk * jnp.exp2(b_mid - b)
            ke = (kk * jnp.exp2(b_end - b)).astype(BF16)
            v = v_ref[rows, :]
            qds.append(qd)
            scs.append([_dot(qd[:, hs], ki[:, hs].T.astype(BF16)) for hs in head_cols])
            dsts.append([lax.dot_general(v[:, hs], ke[:, hs], trans_a, preferred_element_type=F32)
                         for hs in head_cols])
            decs.append(jnp.exp2(b_end))
            mids.append(jnp.exp2(b_mid))
        for k in range(1, w_ref.shape[1], 2):
            dft_slice(k)

        s_mids = [[] for _ in chunk_rows]
        for h, hs in enumerate(head_cols):
            st = st_ref[h]
            for ci in range(nchunk):
                s_mids[ci].append((st * mids[ci][:, hs]).T.astype(BF16))
                st = decs[ci][:, hs] * st + dsts[ci][h]
            st_ref[h] = st

        for ci, rows in enumerate(chunk_rows):
            v = v_ref[rows, :]
            outs = []
            for h, hs in enumerate(head_cols):
                scores = jnp.where(tri, scs[ci][h], 0.0).astype(BF16)
                o = _dot(scores, v[:, hs]) + _dot(qds[ci][:, hs], s_mids[ci][h])
                if not fwd:
                    o = o + of_ref[pl.ds(pl.multiple_of(blk * blk_rows, c) + rows.start, c), hs]
                    o = o * lax.rsqrt(jnp.mean(o * o, axis=-1, keepdims=True) + RMS_EPS) * gn_ref[...]
                outs.append(o)
            o = jnp.concatenate(outs, axis=1)
            if fwd:
                of_ref[pl.ds(pl.multiple_of(blk * blk_rows, c) + rows.start, c), :] = o
            else:
                o_ref[rows, :] = (o * g_ref[rows, :].astype(F32)).astype(BF16)

    pl.when(phase == 0)(functools.partial(scan_block, True))
    pl.when(phase == 1)(functools.partial(scan_block, False))


def _hgrn_fft2(g_norm, qig, log2f, one_minus_f, w4, f2, *, blk_rows=4096):
    t = log2f.shape[0]
    nj = LN_PROJ_STEPS
    sw = log2f.shape[1] // (2 * nj)
    heads = sw // HEAD_DIM
    nblk = t // blk_rows
    d = HEAD_DIM
    _, n1, n2, m = w4.shape
    kb = n1 // (nj * 2 * nblk)
    assert kb * nj * 2 * nblk == n1

    tblk = lambda p, i: i + p * (nblk - 1 - 2 * i)
    step = lambda j, p, i: (j * 2 + p) * nblk + i
    qspec = lambda sec: pl.BlockSpec((blk_rows, sw), lambda j, p, i: (tblk(p, i), 3 * j + sec))
    fspec = pl.BlockSpec((blk_rows, sw), lambda j, p, i: (tblk(p, i), 2 * j + p))
    dft_blk = (2, kb, n2, m)
    dft_spec = pl.BlockSpec(dft_blk, lambda j, p, i: (0, step(j, p, i), 0, 0))

    pipelined = (5 * _nbytes((blk_rows, sw), BF16) + _nbytes((blk_rows, sw), F32)
                 + 2 * _nbytes(dft_blk, BF16) + _nbytes((2 * n2, 2 * n2), BF16))
    resident = _nbytes((t, sw), F32) + 256 * _nbytes((d, d), F32) + 2 * _nbytes((2 * n2, m), F32)
    return pl.pallas_call(
        functools.partial(_hgrn_kernel, nblk=nblk, blk_rows=blk_rows),
        grid=(nj, 2, nblk),
        in_specs=[
            pl.BlockSpec((1, d), lambda j, p, i: (0, 0)),
            qspec(0), qspec(1),
            pl.BlockSpec((blk_rows, sw), lambda j, p, i: (nblk - 1 - p * i, 3 * j + 2)),
            fspec, fspec,
            pl.BlockSpec((2 * n2, 2 * n2), lambda j, p, i: (0, 0)),
            dft_spec,
        ],
        out_specs=[pl.BlockSpec((blk_rows, sw), lambda j, p, i: (nblk - 1 - p * i, j)), dft_spec],
        out_shape=[jax.ShapeDtypeStruct((t, nj * sw), BF16),
                   jax.ShapeDtypeStruct((2, n1, n2, m), BF16)],
        scratch_shapes=[pltpu.VMEM((heads, d, d), F32), pltpu.VMEM((t, sw), F32)],
        compiler_params=pltpu.CompilerParams(
            dimension_semantics=("arbitrary", "arbitrary", "arbitrary"),
            vmem_limit_bytes=_vmem_limit(pipelined, resident)),
        name="hgrn_fft2",
    )(g_norm, qig, qig, qig, log2f, one_minus_f, f2, w4)


def _dft_constants(t, n1):
    n2 = t // n1
    k1 = np.arange(n1)[None, :, None]
    t1 = np.arange(n1)[None, None, :]
    t2 = np.arange(n2)[:, None, None]
    ang = 2.0 * np.pi * ((k1 * (n2 * t1 + t2)) % t) / t
    g = np.concatenate([np.cos(ang), -np.sin(ang)], axis=1) / math.sqrt(n1)
    k2 = np.arange(n2)[:, None]
    s2 = np.arange(n2)[None, :]
    ang2 = 2.0 * np.pi * ((k2 * s2) % n2) / n2
    c2, sn2 = np.cos(ang2) / math.sqrt(n2), np.sin(ang2) / math.sqrt(n2)
    f2 = np.block([[c2, sn2], [-sn2, c2]])
    m = np.arange(FOURIER_GDIM)
    ang3 = 2.0 * np.pi * ((m[:, None] * m[None, :]) % FOURIER_GDIM) / FOURIER_GDIM
    c3 = np.cos(ang3) / math.sqrt(FOURIER_GDIM)
    s3 = np.sin(ang3) / math.sqrt(FOURIER_GDIM)
    as_bf16 = lambda a: jnp.asarray(a, dtype=F32).astype(BF16)
    return as_bf16(g), as_bf16(f2), as_bf16(c3), as_bf16(s3)


def _fft_t1_kernel(u_ref, g_ref, w_ref, *, tb, n1):
    x = pltpu.einshape("tjm->(jt)m", u_ref[...])
    r = jnp.stack([_dot(g_ref[j], x[j * n1:(j + 1) * n1]).astype(BF16) for j in range(tb)])
    r = pltpu.einshape("jam->ajm", r)
    w_ref[0] = r[:n1]
    w_ref[1] = r[n1:]


def _fft_t1(four, g, *, tb=BF16_ROWS):
    t, m = four.shape
    n2, _, n1 = g.shape
    u = four.reshape(n1, n2, m)
    pipelined = (_nbytes((n1, tb, m), BF16) + _nbytes((tb, 2 * n1, n1), BF16)
                 + _nbytes((2, n1, tb, m), BF16))
    return pl.pallas_call(
        functools.partial(_fft_t1_kernel, tb=tb, n1=n1),
        grid=(n2 // tb,),
        in_specs=[
            pl.BlockSpec((n1, tb, m), lambda s: (0, s, 0)),
            pl.BlockSpec((tb, 2 * n1, n1), lambda s: (s, 0, 0)),
        ],
        out_specs=pl.BlockSpec((2, n1, tb, m), lambda s: (0, 0, s, 0)),
        out_shape=jax.ShapeDtypeStruct((2, n1, n2, m), BF16),
        compiler_params=pltpu.CompilerParams(
            dimension_semantics=("parallel",),
            vmem_limit_bytes=_vmem_limit(pipelined, 6 * _nbytes((tb * n1, m), F32))),
        name="fft_t1",
    )(u, g)


def _mix_ln_kernel(x_ref, ge_ref, be_ref, oh_ref, pr_ref, pi_ref, c3_ref, s3_ref, wo_ref,
                   g1_ref, b1_ref, h1_ref, pr_scr, pi_scr, *, alpha):
    gd = FOURIER_GDIM
    pr_scr[...] = pltpu.einshape("akm->(ka)m", pr_ref[...])
    pi_scr[...] = pltpu.einshape("akm->(ka)m", pi_ref[...])
    c3, s3 = c3_ref[...], s3_ref[...]
    parts = []
    for g in range(pr_scr.shape[1] // gd):
        sl = slice(g * gd, (g + 1) * gd)
        parts.append((_dot(pr_scr[:, sl], c3) + _dot(pi_scr[:, sl], s3)).astype(BF16))
    o_four = jnp.concatenate(parts, axis=1)
    wh = oh_ref.shape[1]
    mix = _dot(oh_ref[...], wo_ref[:wh, :]) + _dot(o_four, wo_ref[wh:, :])
    h0 = _layernorm(x_ref[...], ge_ref[...], be_ref[...])
    h1_ref[...] = _layernorm(alpha * h0 + mix, g1_ref[...], b1_ref[...])


def _mix_ln(x, ge, be, o_hgrn, p, c3, s3, wo, g1, b1, *, alpha, tm=512):
    t, d = x.shape
    wh = o_hgrn.shape[1]
    _, n1, n2, m = p.shape
    kr = tm // n1
    gd = FOURIER_GDIM
    const = lambda shape: pl.BlockSpec(shape, lambda i: tuple(0 for _ in shape),
                                       pipeline_mode=pl.Buffered(1))
    pipelined = (_nbytes((tm, d), F32) + _nbytes((tm, wh), BF16) + 2 * _nbytes((n1, kr, m), BF16)
                 + _nbytes((tm, d), F32))
    resident = (_nbytes((wh + m, d), BF16) + 2 * _nbytes((tm, m), BF16) + 2 * _nbytes((tm, m), F32)
                + 3 * _nbytes((tm, d), F32))
    return pl.pallas_call(
        functools.partial(_mix_ln_kernel, alpha=alpha),
        grid=(t // tm,),
        in_specs=[
            pl.BlockSpec((tm, d), lambda i: (i, 0)),
            const((1, d)), const((1, d)),
            pl.BlockSpec((tm, wh), lambda i: (i, 0)),
            pl.BlockSpec((None, n1, kr, m), lambda i: (0, 0, i, 0)),
            pl.BlockSpec((None, n1, kr, m), lambda i: (1, 0, i, 0)),
            const((gd, gd)), const((gd, gd)),
            const((wh + m, d)),
            const((1, d)), const((1, d)),
        ],
        out_specs=pl.BlockSpec((tm, d), lambda i: (i, 0)),
        out_shape=jax.ShapeDtypeStruct((t, d), F32),
        scratch_shapes=[pltpu.VMEM((tm, m), BF16), pltpu.VMEM((tm, m), BF16)],
        compiler_params=pltpu.CompilerParams(
            dimension_semantics=("parallel",),
            vmem_limit_bytes=_vmem_limit(pipelined, resident)),
        name="mix_ln",
    )(x, ge, be, o_hgrn, p, p, c3, s3, wo, g1, b1)


def _ffn_ln_kernel(h1_ref, wg_ref, wu_ref, wd_ref, g2_ref, b2_ref, o_ref, hb_scr, *, alpha):
    f = pl.program_id(1)

    @pl.when(f == 0)
    def _():
        hb_scr[...] = h1_ref[...].astype(BF16)
        o_ref[...] = jnp.zeros_like(o_ref)

    hb = hb_scr[...]
    gate = _dot(hb, wg_ref[...].astype(BF16))
    up = _dot(hb, wu_ref[...].astype(BF16))
    act = (gate * _sigmoid(gate) * up).astype(BF16)
    cw = FFN_OUT_SLAB
    for n in range(o_ref.shape[1] // cw):
        cols = slice(n * cw, (n + 1) * cw)
        o_ref[:, cols] += _dot(act, wd_ref[:, cols].astype(BF16))

    @pl.when(f == pl.num_programs(1) - 1)
    def _():
        for r in range(0, o_ref.shape[0], LN_ROWS):
            rows = slice(r, r + LN_ROWS)
            o_ref[rows, :] = _layernorm(alpha * h1_ref[rows, :] + o_ref[rows, :], g2_ref[...], b2_ref[...])


def _ffn_ln(h1, wg, wu, wd, g2, b2, *, alpha, tm=1024, tf=512):
    t, d = h1.shape
    dff = wg.shape[1]
    pipelined = _nbytes((tm, d), F32) + 3 * _nbytes((d, tf), F32)
    resident = (_nbytes((tm, d), F32) + _nbytes((tm, d), BF16) + 2 * _nbytes((tm, tf), F32)
                + 2 * _nbytes((tm, FFN_OUT_SLAB), F32))
    return pl.pallas_call(
        functools.partial(_ffn_ln_kernel, alpha=alpha),
        grid=(t // tm, dff // tf),
        in_specs=[
            pl.BlockSpec((tm, d), lambda i, f: (i, 0)),
            pl.BlockSpec((d, tf), lambda i, f: (0, f)),
            pl.BlockSpec((d, tf), lambda i, f: (0, f)),
            pl.BlockSpec((tf, d), lambda i, f: (f, 0)),
            pl.BlockSpec((1, d), lambda i, f: (0, 0)),
            pl.BlockSpec((1, d), lambda i, f: (0, 0)),
        ],
        out_specs=pl.BlockSpec((tm, d), lambda i, f: (i, 0), pipeline_mode=pl.Buffered(1)),
        out_shape=jax.ShapeDtypeStruct((t, d), F32),
        scratch_shapes=[pltpu.VMEM((tm, d), BF16)],
        compiler_params=pltpu.CompilerParams(
            dimension_semantics=("parallel", "arbitrary"),
            vmem_limit_bytes=_vmem_limit(pipelined, resident)),
        name="ffn_ln",
    )(h1, wg, wu, wd, g2, b2)


def kernel(x, ln_emb_g, ln_emb_b, w_in, lb_fwd_logits, lb_bwd_logits, g_norm, w_out, ln1_g, ln1_b,
           w_gate, w_up, w_down, ln2_g, ln2_b):
    depth = w_in.shape[0]
    assert depth == 1, "the embedding LayerNorm is fused into the single layer's projection"
    batch, seq, d = x.shape
    assert batch == 1
    alpha = (2.0 * depth) ** 0.25
    layer = 0
    row = lambda a: a.reshape(1, -1).astype(F32)

    x2 = x.reshape(seq, d)
    log2f, one_minus_f, qig, four = _ln_proj(
        x2, row(ln_emb_g), row(ln_emb_b), lb_fwd_logits.astype(F32), lb_bwd_logits.astype(F32),
        w_in[layer], layer=layer)

    g1c, f2c, c3, s3 = _dft_constants(seq, FFT_N1)
    o_hgrn, p = _hgrn_fft2(row(g_norm[layer]), qig, log2f, one_minus_f, _fft_t1(four, g1c), f2c)

    wo = w_out[layer].astype(BF16)
    h1 = _mix_ln(x2, row(ln_emb_g), row(ln_emb_b), o_hgrn, p, c3, s3, wo,
                 row(ln1_g[layer]), row(ln1_b[layer]), alpha=alpha)

    out = _ffn_ln(h1, w_gate[layer], w_up[layer], w_down[layer],
                  row(ln2_g[layer]), row(ln2_b[layer]), alpha=alpha)
    return out.reshape(batch, seq, d)
```

```python
import functools
import math

import numpy as np
import jax
import jax.numpy as jnp
from jax import lax
from jax.experimental import pallas as pl
from jax.experimental.pallas import tpu as pltpu

LN_EPS = 1e-5
RMS_EPS = 1e-6
HEAD_DIM = 128
FOURIER_GDIM = 256
FFT_N1 = 32
BF16_ROWS = 16
LN_ROWS = 256
FFN_OUT_SLAB = 512
HGRN_CHUNK = 128
LN_PROJ_STEPS = 4
V7X_VMEM_CAP = 63 * 1024 * 1024
COMPILER_SCRATCH_BYTES = 4 * 1024 * 1024
F32_ROWS = 8

BF16 = jnp.bfloat16
F32 = jnp.float32


def _vmem_limit(pipelined_bytes, resident_bytes):
    return int(min(V7X_VMEM_CAP, 2 * pipelined_bytes + resident_bytes + COMPILER_SCRATCH_BYTES))


def _nbytes(shape, dtype):
    return int(np.prod(shape)) * jnp.dtype(dtype).itemsize


def _dot(a, b):
    return jnp.dot(a, b, preferred_element_type=F32)


def _layernorm(x, g, b):
    mu = jnp.mean(x, axis=-1, keepdims=True)
    xc = x - mu
    var = jnp.mean(xc * xc, axis=-1, keepdims=True)
    return xc * lax.rsqrt(var + LN_EPS) * g + b


def _sigmoid(x):
    return 1.0 / (1.0 + jnp.exp(-x))


def _silu(x):
    return x * _sigmoid(x)


def _swap_leading(x, *, merge):
    a, b, m = x.shape
    y = jnp.swapaxes(x, 0, 1)
    return y.reshape(b * a, m) if merge else y


def _lower_bound(logits_ref, layer):
    logits = logits_ref[...]
    e = jnp.exp(logits - jnp.max(logits, axis=0, keepdims=True))
    return jnp.sum(e[:layer + 1], axis=0, keepdims=True) / jnp.sum(e, axis=0, keepdims=True)


def _ln_proj_kernel(x_ref, g_ref, b_ref, lbf_ref, lbb_ref, wzf_ref, wzb_ref, wq_ref, wi_ref, wg_ref, wf_ref,
                    lf_ref, kk_ref, q_ref, f_ref, h_even, h_odd, *, layer, ntiles):
    r = pl.program_id(0)
    j = pl.program_id(1)
    qrows = x_ref.shape[0]

    def normalise_quarter(h_dst):
        rows = pl.ds(pl.multiple_of(j * qrows, qrows), qrows)
        y = _layernorm(x_ref[...], g_ref[...], b_ref[...]).astype(BF16)
        h_dst[rows, :] = y
        bits = pltpu.bitcast(y, jnp.uint32)
        acc = bits[0:F32_ROWS, :]
        for s in range(F32_ROWS, bits.shape[0], F32_ROWS):
            acc = acc | bits[s:s + F32_ROWS, :]
        zero = ((acc >> 16) >> 16).astype(F32)
        return zero[0:1, 0:wf_ref.shape[1]]

    def project_slab(h_src, zero=None):
        h = h_src[...]
        sw = wf_ref.shape[1]
        proj = lambda w_ref: _dot(h, w_ref[...].astype(BF16))
        for k, (w_ref, lb_ref) in enumerate(((wzf_ref, lbf_ref), (wzb_ref, lbb_ref))):
            lb = _lower_bound(lb_ref, layer)
            if zero is not None and k == 0:
                lb = lb + zero
            f = lb + (1.0 - lb) * _sigmoid(proj(w_ref))
            lf_ref[:, k * sw:(k + 1) * sw] = jnp.log2(f)
            kk_ref[:, k * sw:(k + 1) * sw] = (1.0 - f).astype(BF16)
        q_ref[:, 0:sw] = _silu(proj(wq_ref)).astype(BF16)
        q_ref[:, sw:2 * sw] = proj(wi_ref).astype(BF16)
        q_ref[:, 2 * sw:3 * sw] = _silu(proj(wg_ref)).astype(BF16)
        f_ref[...] = proj(wf_ref).astype(BF16)

    @pl.when(r == 0)
    def _():
        normalise_quarter(h_even)

    middle = jnp.logical_and(r > 0, r < ntiles)

    @pl.when(jnp.logical_and(middle, r % 2 == 1))
    def _():
        project_slab(h_even, normalise_quarter(h_odd))

    @pl.when(jnp.logical_and(middle, r % 2 == 0))
    def _():
        project_slab(h_odd, normalise_quarter(h_even))

    pl.when(r == ntiles)(functools.partial(project_slab, h_odd if ntiles % 2 == 0 else h_even))


def _ln_proj(x, g, b, lbf_logits, lbb_logits, w, *, layer, tm=1024):
    t, d = x.shape
    nj = LN_PROJ_STEPS
    sw = w.shape[1] // (6 * nj)
    slots = lbf_logits.shape[0]
    ntiles = t // tm
    qrows = tm // nj
    col = lambda r, j: jnp.where(r == 0, 0, j)
    wspec = lambda section: pl.BlockSpec((d, sw), lambda r, j: (0, section * nj + col(r, j)))
    ospec = lambda width: pl.BlockSpec((tm, width), lambda r, j: (jnp.maximum(r - 1, 0), col(r, j)))
    pipelined = (_nbytes((qrows, d), F32) + _nbytes((d, 6 * sw), F32)
                 + _nbytes((tm, 2 * sw), F32) + _nbytes((tm, 6 * sw), BF16))
    resident = 2 * _nbytes((tm, d), BF16) + _nbytes((tm, 6 * sw), F32)
    return pl.pallas_call(
        functools.partial(_ln_proj_kernel, layer=layer, ntiles=ntiles),
        grid=(ntiles + 1, nj),
        in_specs=[
            pl.BlockSpec((qrows, d), lambda r, j: (jnp.minimum(r, ntiles - 1) * nj + j, 0)),
            pl.BlockSpec((1, d), lambda r, j: (0, 0)),
            pl.BlockSpec((1, d), lambda r, j: (0, 0)),
            pl.BlockSpec((slots, sw), lambda r, j: (0, j)),
            pl.BlockSpec((slots, sw), lambda r, j: (0, j)),
            wspec(2), wspec(3), wspec(0), wspec(1), wspec(4), wspec(5),
        ],
        out_specs=[ospec(2 * sw), ospec(2 * sw), ospec(3 * sw), ospec(sw)],
        out_shape=[
            jax.ShapeDtypeStruct((t, nj * 2 * sw), F32),
            jax.ShapeDtypeStruct((t, nj * 2 * sw), BF16),
            jax.ShapeDtypeStruct((t, nj * 3 * sw), BF16),
            jax.ShapeDtypeStruct((t, nj * sw), BF16),
        ],
        scratch_shapes=[pltpu.VMEM((tm, d), BF16), pltpu.VMEM((tm, d), BF16)],
        compiler_params=pltpu.CompilerParams(
            dimension_semantics=("arbitrary", "arbitrary"),
            vmem_limit_bytes=_vmem_limit(pipelined, resident)),
        name="ln_proj",
    )(x, g, b, lbf_logits, lbb_logits, w, w, w, w, w, w)


def _hgrn_kernel(gn_ref, q_ref, v_ref, g_ref, lf_ref, kk_ref, f2_ref, w_ref, o_ref, p_ref, st_ref, of_ref,
                 *, nblk, blk_rows):
    c = HGRN_CHUNK
    d = HEAD_DIM
    half = c // 2
    nchunk = blk_rows // c
    heads = st_ref.shape[0]
    phase = pl.program_id(1)
    t = pl.program_id(2)

    @pl.when(t == 0)
    def _():
        st_ref[...] = jnp.zeros_like(st_ref)

    def scan_block(fwd):
        row = lax.broadcasted_iota(jnp.int32, (c, c), 0)
        col = lax.broadcasted_iota(jnp.int32, (c, c), 1)
        tri = (col <= row) if fwd else (col >= row)
        tri2 = jnp.concatenate([tri.astype(BF16)] * 2, axis=1)
        blk = t if fwd else nblk - 1 - t
        chunk_rows = [pl.ds((ci if fwd else nchunk - 1 - ci) * c, c) for ci in range(nchunk)]
        head_cols = [slice(h * d, (h + 1) * d) for h in range(heads)]
        trans_a = (((0,), (0,)), ((), ()))
        n2 = w_ref.shape[2]

        def dft_slice(k):
            res = _dot(f2_ref[...], jnp.concatenate([w_ref[0, k], w_ref[1, k]], axis=0))
            p_ref[0, k] = res[:n2].astype(BF16)
            p_ref[1, k] = res[n2:].astype(BF16)

        bs = []
        for rows in chunk_rows:
            lf = lf_ref[rows, :]
            hi = lf.astype(BF16)
            lo = (lf - hi.astype(F32)).astype(BF16)
            bs.append(_dot(tri2, jnp.concatenate([hi, lo], axis=0)))
        for k in range(0, w_ref.shape[1], 2):
            dft_slice(k)

        qds, dsts, scs, decs, mids = [], [], [], [], []
        for rows, b in zip(chunk_rows, bs):
            b_mid = b[half - 1:half, :] if fwd else b[half:half + 1, :]
            b_end = b[c - 1:c, :] if fwd else b[0:1, :]
            kk = kk_ref[rows, :].astype(F32)
            qd = (q_ref[rows, :].astype(F32) * jnp.exp2(b - b_mid)).astype(BF16)
            ki = kk * jnp.exp2(b_mid - b)
            ke = (kk * jnp.exp2(b_end - b)).astype(BF16)
            v = v_ref[rows, :]
            qds.append(qd)
            scs.append([_dot(qd[:, hs], ki[:, hs].T.astype(BF16)) for hs in head_cols])
            dsts.append([lax.dot_general(v[:, hs], ke[:, hs], trans_a, preferred_element_type=F32)
                         for hs in head_cols])
            decs.append(jnp.exp2(b_end))
            mids.append(jnp.exp2(b_mid))
        for k in range(1, w_ref.shape[1], 2):
            dft_slice(k)

        s_mids = [[] for _ in chunk_rows]
        for h, hs in enumerate(head_cols):
            st = st_ref[h]
            for ci in range(nchunk):
                s_mids[ci].append((st * mids[ci][:, hs]).T.astype(BF16))
                st = decs[ci][:, hs] * st + dsts[ci][h]
            st_ref[h] = st

        for ci, rows in enumerate(chunk_rows):
            v = v_ref[rows, :]
            outs = []
            for h, hs in enumerate(head_cols):
                scores = jnp.where(tri, scs[ci][h], 0.0).astype(BF16)
                o = _dot(scores, v[:, hs]) + _dot(qds[ci][:, hs], s_mids[ci][h])
                if not fwd:
                    o = o + of_ref[pl.ds(pl.multiple_of(blk * blk_rows, c) + rows.start, c), hs]
                    o = o * lax.rsqrt(jnp.mean(o * o, axis=-1, keepdims=True) + RMS_EPS) * gn_ref[...]
                outs.append(o)
            o = jnp.concatenate(outs, axis=1)
            if fwd:
                of_ref[pl.ds(pl.multiple_of(blk * blk_rows, c) + rows.start, c), :] = o
            else:
                o_ref[rows, :] = (o * g_ref[rows, :].astype(F32)).astype(BF16)

    pl.when(phase == 0)(functools.partial(scan_block, True))
    pl.when(phase == 1)(functools.partial(scan_block, False))


def _hgrn_fft2(g_norm, qig, log2f, one_minus_f, w4, f2, *, blk_rows=4096):
    t = log2f.shape[0]
    nj = LN_PROJ_STEPS
    sw = log2f.shape[1] // (2 * nj)
    heads = sw // HEAD_DIM
    nblk = t // blk_rows
    d = HEAD_DIM
    _, n1, n2, m = w4.shape
    kb = n1 // (nj * 2 * nblk)
    assert kb * nj * 2 * nblk == n1

    tblk = lambda p, i: i + p * (nblk - 1 - 2 * i)
    step = lambda j, p, i: (j * 2 + p) * nblk + i
    qspec = lambda sec: pl.BlockSpec((blk_rows, sw), lambda j, p, i: (tblk(p, i), 3 * j + sec))
    fspec = pl.BlockSpec((blk_rows, sw), lambda j, p, i: (tblk(p, i), 2 * j + p))
    dft_blk = (2, kb, n2, m)
    dft_spec = pl.BlockSpec(dft_blk, lambda j, p, i: (0, step(j, p, i), 0, 0))

    pipelined = (5 * _nbytes((blk_rows, sw), BF16) + _nbytes((blk_rows, sw), F32)
                 + 2 * _nbytes(dft_blk, BF16) + _nbytes((2 * n2, 2 * n2), BF16))
    resident = _nbytes((t, sw), F32) + 256 * _nbytes((d, d), F32) + 2 * _nbytes((2 * n2, m), F32)
    return pl.pallas_call(
        functools.partial(_hgrn_kernel, nblk=nblk, blk_rows=blk_rows),
        grid=(nj, 2, nblk),
        in_specs=[
            pl.BlockSpec((1, d), lambda j, p, i: (0, 0)),
            qspec(0), qspec(1),
            pl.BlockSpec((blk_rows, sw), lambda j, p, i: (nblk - 1 - p * i, 3 * j + 2)),
            fspec, fspec,
            pl.BlockSpec((2 * n2, 2 * n2), lambda j, p, i: (0, 0)),
            dft_spec,
        ],
        out_specs=[pl.BlockSpec((blk_rows, sw), lambda j, p, i: (nblk - 1 - p * i, j)), dft_spec],
        out_shape=[jax.ShapeDtypeStruct((t, nj * sw), BF16),
                   jax.ShapeDtypeStruct((2, n1, n2, m), BF16)],
        scratch_shapes=[pltpu.VMEM((heads, d, d), F32), pltpu.VMEM((t, sw), F32)],
        compiler_params=pltpu.CompilerParams(
            dimension_semantics=("arbitrary", "arbitrary", "arbitrary"),
            vmem_limit_bytes=_vmem_limit(pipelined, resident)),
        name="hgrn_fft2",
    )(g_norm, qig, qig, qig, log2f, one_minus_f, f2, w4)


def _dft_constants(t, n1):
    n2 = t // n1
    k1 = np.arange(n1)[None, :, None]
    t1 = np.arange(n1)[None, None, :]
    t2 = np.arange(n2)[:, None, None]
    ang = 2.0 * np.pi * ((k1 * (n2 * t1 + t2)) % t) / t
    g = np.concatenate([np.cos(ang), -np.sin(ang)], axis=1) / math.sqrt(n1)
    k2 = np.arange(n2)[:, None]
    s2 = np.arange(n2)[None, :]
    ang2 = 2.0 * np.pi * ((k2 * s2) % n2) / n2
    c2, sn2 = np.cos(ang2) / math.sqrt(n2), np.sin(ang2) / math.sqrt(n2)
    f2 = np.block([[c2, sn2], [-sn2, c2]])
    m = np.arange(FOURIER_GDIM)
    ang3 = 2.0 * np.pi * ((m[:, None] * m[None, :]) % FOURIER_GDIM) / FOURIER_GDIM
    c3 = np.cos(ang3) / math.sqrt(FOURIER_GDIM)
    s3 = np.sin(ang3) / math.sqrt(FOURIER_GDIM)
    as_bf16 = lambda a: jnp.asarray(a, dtype=F32).astype(BF16)
    return as_bf16(g), as_bf16(f2), as_bf16(c3), as_bf16(s3)


def _fft_t1_kernel(u_ref, g_ref, w_ref, *, tb, n1):
    x = _swap_leading(u_ref[...], merge=True)
    r = jnp.stack([_dot(g_ref[j], x[j * n1:(j + 1) * n1]).astype(BF16) for j in range(tb)])
    r = _swap_leading(r, merge=False)
    w_ref[0] = r[:n1]
    w_ref[1] = r[n1:]


def _fft_t1(four, g, *, tb=BF16_ROWS):
    t, m = four.shape
    n2, _, n1 = g.shape
    u = four.reshape(n1, n2, m)
    pipelined = (_nbytes((n1, tb, m), BF16) + _nbytes((tb, 2 * n1, n1), BF16)
                 + _nbytes((2, n1, tb, m), BF16))
    return pl.pallas_call(
        functools.partial(_fft_t1_kernel, tb=tb, n1=n1),
        grid=(n2 // tb,),
        in_specs=[
            pl.BlockSpec((n1, tb, m), lambda s: (0, s, 0)),
            pl.BlockSpec((tb, 2 * n1, n1), lambda s: (s, 0, 0)),
        ],
        out_specs=pl.BlockSpec((2, n1, tb, m), lambda s: (0, 0, s, 0)),
        out_shape=jax.ShapeDtypeStruct((2, n1, n2, m), BF16),
        compiler_params=pltpu.CompilerParams(
            dimension_semantics=("parallel",),
            vmem_limit_bytes=_vmem_limit(pipelined, 6 * _nbytes((tb * n1, m), F32))),
        name="fft_t1",
    )(u, g)


def _mix_ln_kernel(x_ref, ge_ref, be_ref, oh_ref, pr_ref, pi_ref, c3_ref, s3_ref, wo_ref,
                   g1_ref, b1_ref, h1_ref, pr_scr, pi_scr, *, alpha):
    gd = FOURIER_GDIM
    pr_scr[...] = _swap_leading(pr_ref[...], merge=True)
    pi_scr[...] = _swap_leading(pi_ref[...], merge=True)
    c3, s3 = c3_ref[...], s3_ref[...]
    parts = []
    for g in range(pr_scr.shape[1] // gd):
        sl = slice(g * gd, (g + 1) * gd)
        parts.append((_dot(pr_scr[:, sl], c3) + _dot(pi_scr[:, sl], s3)).astype(BF16))
    o_four = jnp.concatenate(parts, axis=1)
    wh = oh_ref.shape[1]
    mix = _dot(oh_ref[...], wo_ref[:wh, :]) + _dot(o_four, wo_ref[wh:, :])
    h0 = _layernorm(x_ref[...], ge_ref[...], be_ref[...])
    h1_ref[...] = _layernorm(alpha * h0 + mix, g1_ref[...], b1_ref[...])


def _mix_ln(x, ge, be, o_hgrn, p, c3, s3, wo, g1, b1, *, alpha, tm=512):
    t, d = x.shape
    wh = o_hgrn.shape[1]
    _, n1, n2, m = p.shape
    kr = tm // n1
    gd = FOURIER_GDIM
    const = lambda shape: pl.BlockSpec(shape, lambda i: tuple(0 for _ in shape),
                                       pipeline_mode=pl.Buffered(1))
    pipelined = (_nbytes((tm, d), F32) + _nbytes((tm, wh), BF16) + 2 * _nbytes((n1, kr, m), BF16)
                 + _nbytes((tm, d), F32))
    resident = (_nbytes((wh + m, d), BF16) + 2 * _nbytes((tm, m), BF16) + 2 * _nbytes((tm, m), F32)
                + 3 * _nbytes((tm, d), F32))
    return pl.pallas_call(
        functools.partial(_mix_ln_kernel, alpha=alpha),
        grid=(t // tm,),
        in_specs=[
            pl.BlockSpec((tm, d), lambda i: (i, 0)),
            const((1, d)), const((1, d)),
            pl.BlockSpec((tm, wh), lambda i: (i, 0)),
            pl.BlockSpec((None, n1, kr, m), lambda i: (0, 0, i, 0)),
            pl.BlockSpec((None, n1, kr, m), lambda i: (1, 0, i, 0)),
            const((gd, gd)), const((gd, gd)),
            const((wh + m, d)),
            const((1, d)), const((1, d)),
        ],
        out_specs=pl.BlockSpec((tm, d), lambda i: (i, 0)),
        out_shape=jax.ShapeDtypeStruct((t, d), F32),
        scratch_shapes=[pltpu.VMEM((tm, m), BF16), pltpu.VMEM((tm, m), BF16)],
        compiler_params=pltpu.CompilerParams(
            dimension_semantics=("parallel",),
            vmem_limit_bytes=_vmem_limit(pipelined, resident)),
        name="mix_ln",
    )(x, ge, be, o_hgrn, p, p, c3, s3, wo, g1, b1)


def _ffn_ln_kernel(h1_ref, wg_ref, wu_ref, wd_ref, g2_ref, b2_ref, o_ref, hb_scr, *, alpha):
    f = pl.program_id(1)

    def ffn_step(first):
        if first:
            hb = h1_ref[...].astype(BF16)
            hb_scr[...] = hb
        else:
            hb = hb_scr[...]
        gate = _dot(hb, wg_ref[...].astype(BF16))
        up = _dot(hb, wu_ref[...].astype(BF16))
        act = (gate * _sigmoid(gate) * up).astype(BF16)
        cw = FFN_OUT_SLAB
        for n in range(o_ref.shape[1] // cw):
            cols = slice(n * cw, (n + 1) * cw)
            part = _dot(act, wd_ref[:, cols].astype(BF16))
            if first:
                o_ref[:, cols] = alpha * h1_ref[:, cols] + part
            else:
                o_ref[:, cols] += part

    pl.when(f == 0)(functools.partial(ffn_step, True))
    pl.when(f > 0)(functools.partial(ffn_step, False))

    @pl.when(f == pl.num_programs(1) - 1)
    def _():
        for r in range(0, o_ref.shape[0], LN_ROWS):
            rows = slice(r, r + LN_ROWS)
            o_ref[rows, :] = _layernorm(o_ref[rows, :], g2_ref[...], b2_ref[...])


def _ffn_ln(h1, wg, wu, wd, g2, b2, *, alpha, tm=1024, tf=512):
    t, d = h1.shape
    dff = wg.shape[1]
    pipelined = _nbytes((tm, d), F32) + 3 * _nbytes((d, tf), F32)
    resident = (_nbytes((tm, d), F32) + _nbytes((tm, d), BF16) + 2 * _nbytes((tm, tf), F32)
                + 2 * _nbytes((tm, FFN_OUT_SLAB), F32))
    return pl.pallas_call(
        functools.partial(_ffn_ln_kernel, alpha=alpha),
        grid=(t // tm, dff // tf),
        in_specs=[
            pl.BlockSpec((tm, d), lambda i, f: (i, 0)),
            pl.BlockSpec((d, tf), lambda i, f: (0, f)),
            pl.BlockSpec((d, tf), lambda i, f: (0, f)),
            pl.BlockSpec((tf, d), lambda i, f: (f, 0)),
            pl.BlockSpec((1, d), lambda i, f: (0, 0)),
            pl.BlockSpec((1, d), lambda i, f: (0, 0)),
        ],
        out_specs=pl.BlockSpec((tm, d), lambda i, f: (i, 0), pipeline_mode=pl.Buffered(1)),
        out_shape=jax.ShapeDtypeStruct((t, d), F32),
        scratch_shapes=[pltpu.VMEM((tm, d), BF16)],
        compiler_params=pltpu.CompilerParams(
            dimension_semantics=("parallel", "arbitrary"),
            vmem_limit_bytes=_vmem_limit(pipelined, resident)),
        name="ffn_ln",
    )(h1, wg, wu, wd, g2, b2)


def kernel(x, ln_emb_g, ln_emb_b, w_in, lb_fwd_logits, lb_bwd_logits, g_norm, w_out, ln1_g, ln1_b,
           w_gate, w_up, w_down, ln2_g, ln2_b):
    depth = w_in.shape[0]
    assert depth == 1, "the embedding LayerNorm is fused into the single layer's projection"
    batch, seq, d = x.shape
    assert batch == 1
    alpha = (2.0 * depth) ** 0.25
    layer = 0
    row = lambda a: a.reshape(1, -1).astype(F32)

    x2 = x.reshape(seq, d)
    log2f, one_minus_f, qig, four = _ln_proj(
        x2, row(ln_emb_g), row(ln_emb_b), lb_fwd_logits.astype(F32), lb_bwd_logits.astype(F32),
        w_in[layer], layer=layer)

    g1c, f2c, c3, s3 = _dft_constants(seq, FFT_N1)
    o_hgrn, p = _hgrn_fft2(row(g_norm[layer]), qig, log2f, one_minus_f, _fft_t1(four, g1c), f2c)

    wo = w_out[layer].astype(BF16)
    h1 = _mix_ln(x2, row(ln_emb_g), row(ln_emb_b), o_hgrn, p, c3, s3, wo,
                 row(ln1_g[layer]), row(ln1_b[layer]), alpha=alpha)

    out = _ffn_ln(h1, w_gate[layer], w_up[layer], w_down[layer],
                  row(ln2_g[layer]), row(ln2_b[layer]), alpha=alpha)
    return out.reshape(batch, seq, d)
```

```python
import functools
import math

import numpy as np
import jax
import jax.numpy as jnp
from jax import lax
from jax.experimental import pallas as pl
from jax.experimental.pallas import tpu as pltpu

LN_EPS = 1e-5
RMS_EPS = 1e-6
HEAD_DIM = 128
FOURIER_GDIM = 256
FFT_N1 = 32
BF16_ROWS = 16
LN_ROWS = 256
FFN_OUT_SLAB = 512
HGRN_CHUNK = 128
LN_PROJ_STEPS = 4
V7X_VMEM_CAP = 63 * 1024 * 1024
COMPILER_SCRATCH_BYTES = 4 * 1024 * 1024
F32_ROWS = 8

BF16 = jnp.bfloat16
F32 = jnp.float32


def _vmem_limit(pipelined_bytes, resident_bytes):
    return int(min(V7X_VMEM_CAP, 2 * pipelined_bytes + resident_bytes + COMPILER_SCRATCH_BYTES))


def _nbytes(shape, dtype):
    return int(np.prod(shape)) * jnp.dtype(dtype).itemsize


def _dot(a, b):
    return jnp.dot(a, b, preferred_element_type=F32)


def _layernorm(x, g, b):
    mu = jnp.mean(x, axis=-1, keepdims=True)
    xc = x - mu
    var = jnp.mean(xc * xc, axis=-1, keepdims=True)
    return xc * lax.rsqrt(var + LN_EPS) * g + b


def _sigmoid(x):
    return 1.0 / (1.0 + jnp.exp(-x))


def _silu(x):
    return x * _sigmoid(x)


def _swap_leading(x, *, merge):
    a, b, m = x.shape
    y = jnp.swapaxes(x, 0, 1)
    return y.reshape(b * a, m) if merge else y


def _lower_bound(logits_ref, layer):
    logits = logits_ref[...]
    e = jnp.exp(logits - jnp.max(logits, axis=0, keepdims=True))
    return jnp.sum(e[:layer + 1], axis=0, keepdims=True) / jnp.sum(e, axis=0, keepdims=True)


def _ln_proj_kernel(x_ref, g_ref, b_ref, lbf_ref, lbb_ref, wzf_ref, wzb_ref, wq_ref, wi_ref, wg_ref, wf_ref,
                    lf_ref, kk_ref, q_ref, f_ref, h_even, h_odd, *, layer, ntiles):
    r = pl.program_id(0)
    j = pl.program_id(1)
    qrows = x_ref.shape[0]

    def normalise_quarter(h_dst):
        rows = pl.ds(pl.multiple_of(j * qrows, qrows), qrows)
        y = _layernorm(x_ref[...], g_ref[...], b_ref[...]).astype(BF16)
        h_dst[rows, :] = y
        bits = pltpu.bitcast(y, jnp.uint32)
        acc = bits[0:F32_ROWS, :]
        for s in range(F32_ROWS, bits.shape[0], F32_ROWS):
            acc = acc | bits[s:s + F32_ROWS, :]
        zero = ((acc >> 16) >> 16).astype(F32)
        return zero[0:1, 0:wf_ref.shape[1]]

    def project_slab(h_src, zero=None):
        h = h_src[...]
        sw = wf_ref.shape[1]
        proj = lambda w_ref: _dot(h, w_ref[...].astype(BF16))
        for k, (w_ref, lb_ref) in enumerate(((wzf_ref, lbf_ref), (wzb_ref, lbb_ref))):
            lb = _lower_bound(lb_ref, layer)
            if zero is not None and k == 0:
                lb = lb + zero
            f = lb + (1.0 - lb) * _sigmoid(proj(w_ref))
            lf_ref[:, k * sw:(k + 1) * sw] = jnp.log2(f)
            kk_ref[:, k * sw:(k + 1) * sw] = (1.0 - f).astype(BF16)
        q_ref[:, 0:sw] = _silu(proj(wq_ref)).astype(BF16)
        q_ref[:, sw:2 * sw] = proj(wi_ref).astype(BF16)
        q_ref[:, 2 * sw:3 * sw] = _silu(proj(wg_ref)).astype(BF16)
        f_ref[...] = proj(wf_ref).astype(BF16)

    @pl.when(r == 0)
    def _():
        normalise_quarter(h_even)

    middle = jnp.logical_and(r > 0, r < ntiles)

    @pl.when(jnp.logical_and(middle, r % 2 == 1))
    def _():
        project_slab(h_even, normalise_quarter(h_odd))

    @pl.when(jnp.logical_and(middle, r % 2 == 0))
    def _():
        project_slab(h_odd, normalise_quarter(h_even))

    pl.when(r == ntiles)(functools.partial(project_slab, h_odd if ntiles % 2 == 0 else h_even))


def _ln_proj(x, g, b, lbf_logits, lbb_logits, w, *, layer, tm=1024):
    t, d = x.shape
    nj = LN_PROJ_STEPS
    sw = w.shape[1] // (6 * nj)
    slots = lbf_logits.shape[0]
    ntiles = t // tm
    qrows = tm // nj
    col = lambda r, j: jnp.where(r == 0, 0, j)
    wspec = lambda section: pl.BlockSpec((d, sw), lambda r, j: (0, section * nj + col(r, j)))
    ospec = lambda width: pl.BlockSpec((tm, width), lambda r, j: (jnp.maximum(r - 1, 0), col(r, j)))
    pipelined = (_nbytes((qrows, d), F32) + _nbytes((d, 6 * sw), F32)
                 + _nbytes((tm, 2 * sw), F32) + _nbytes((tm, 6 * sw), BF16))
    resident = 2 * _nbytes((tm, d), BF16) + _nbytes((tm, 6 * sw), F32)
    return pl.pallas_call(
        functools.partial(_ln_proj_kernel, layer=layer, ntiles=ntiles),
        grid=(ntiles + 1, nj),
        in_specs=[
            pl.BlockSpec((qrows, d), lambda r, j: (jnp.minimum(r, ntiles - 1) * nj + j, 0)),
            pl.BlockSpec((1, d), lambda r, j: (0, 0)),
            pl.BlockSpec((1, d), lambda r, j: (0, 0)),
            pl.BlockSpec((slots, sw), lambda r, j: (0, j)),
            pl.BlockSpec((slots, sw), lambda r, j: (0, j)),
            wspec(2), wspec(3), wspec(0), wspec(1), wspec(4), wspec(5),
        ],
        out_specs=[ospec(2 * sw), ospec(2 * sw), ospec(3 * sw), ospec(sw)],
        out_shape=[
            jax.ShapeDtypeStruct((t, nj * 2 * sw), F32),
            jax.ShapeDtypeStruct((t, nj * 2 * sw), BF16),
            jax.ShapeDtypeStruct((t, nj * 3 * sw), BF16),
            jax.ShapeDtypeStruct((t, nj * sw), BF16),
        ],
        scratch_shapes=[pltpu.VMEM((tm, d), BF16), pltpu.VMEM((tm, d), BF16)],
        compiler_params=pltpu.CompilerParams(
            dimension_semantics=("arbitrary", "arbitrary"),
            vmem_limit_bytes=_vmem_limit(pipelined, resident)),
        name="ln_proj",
    )(x, g, b, lbf_logits, lbb_logits, w, w, w, w, w, w)


def _hgrn_kernel(gn_ref, q_ref, v_ref, g_ref, lf_ref, kk_ref, f2_ref, w_ref, o_ref, p_ref, st_ref, of_ref,
                 *, nblk, blk_rows):
    c = HGRN_CHUNK
    d = HEAD_DIM
    half = c // 2
    nchunk = blk_rows // c
    heads = st_ref.shape[0]
    phase = pl.program_id(1)
    t = pl.program_id(2)

    @pl.when(t == 0)
    def _():
        st_ref[...] = jnp.zeros_like(st_ref)

    def scan_block(fwd):
        row = lax.broadcasted_iota(jnp.int32, (c, c), 0)
        col = lax.broadcasted_iota(jnp.int32, (c, c), 1)
        tri = (col <= row) if fwd else (col >= row)
        tri2 = jnp.concatenate([tri.astype(BF16)] * 2, axis=1)
        blk = t if fwd else nblk - 1 - t
        chunk_rows = [pl.ds((ci if fwd else nchunk - 1 - ci) * c, c) for ci in range(nchunk)]
        head_cols = [slice(h * d, (h + 1) * d) for h in range(heads)]
        trans_a = (((0,), (0,)), ((), ()))
        n2 = w_ref.shape[2]

        def dft_slice(k):
            res = _dot(f2_ref[...], jnp.concatenate([w_ref[0, k], w_ref[1, k]], axis=0))
            p_ref[0, k] = res[:n2].astype(BF16)
            p_ref[1, k] = res[n2:].astype(BF16)

        bs = []
        for rows in chunk_rows:
            lf = lf_ref[rows, :]
            hi = lf.astype(BF16)
            lo = (lf - hi.astype(F32)).astype(BF16)
            bs.append(_dot(tri2, jnp.concatenate([hi, lo], axis=0)))
        for k in range(0, w_ref.shape[1], 2):
            dft_slice(k)

        qds, dsts, scs, decs, mids = [], [], [], [], []
        for rows, b in zip(chunk_rows, bs):
            b_mid = b[half - 1:half, :] if fwd else b[half:half + 1, :]
            b_end = b[c - 1:c, :] if fwd else b[0:1, :]
            kk = kk_ref[rows, :].astype(F32)
            qd = (q_ref[rows, :].astype(F32) * jnp.exp2(b - b_mid)).astype(BF16)
            ki = kk * jnp.exp2(b_mid - b)
            ke = (kk * jnp.exp2(b_end - b)).astype(BF16)
            v = v_ref[rows, :]
            qds.append(qd)
            scs.append([_dot(qd[:, hs], ki[:, hs].T.astype(BF16)) for hs in head_cols])
            dsts.append([lax.dot_general(v[:, hs], ke[:, hs], trans_a, preferred_element_type=F32)
                         for hs in head_cols])
            decs.append(jnp.exp2(b_end))
            mids.append(jnp.exp2(b_mid))
        for k in range(1, w_ref.shape[1], 2):
            dft_slice(k)

        s_mids = [[] for _ in chunk_rows]
        for h, hs in enumerate(head_cols):
            st = st_ref[h]
            for ci in range(nchunk):
                s_mids[ci].append((st * mids[ci][:, hs]).T.astype(BF16))
                st = decs[ci][:, hs] * st + dsts[ci][h]
            st_ref[h] = st

        for ci, rows in enumerate(chunk_rows):
            v = v_ref[rows, :]
            outs = []
            for h, hs in enumerate(head_cols):
                scores = jnp.where(tri, scs[ci][h], 0.0).astype(BF16)
                o = _dot(scores, v[:, hs]) + _dot(qds[ci][:, hs], s_mids[ci][h])
                if not fwd:
                    o = o + of_ref[pl.ds(pl.multiple_of(blk * blk_rows, c) + rows.start, c), hs]
                    o = o * lax.rsqrt(jnp.mean(o * o, axis=-1, keepdims=True) + RMS_EPS) * gn_ref[...]
                outs.append(o)
            o = jnp.concatenate(outs, axis=1)
            if fwd:
                of_ref[pl.ds(pl.multiple_of(blk * blk_rows, c) + rows.start, c), :] = o
            else:
                o_ref[rows, :] = (o * g_ref[rows, :].astype(F32)).astype(BF16)

    pl.when(phase == 0)(functools.partial(scan_block, True))
    pl.when(phase == 1)(functools.partial(scan_block, False))


def _hgrn_fft2(g_norm, qig, log2f, one_minus_f, w4, f2, *, blk_rows=2048):
    t = log2f.shape[0]
    nj = LN_PROJ_STEPS
    sw = log2f.shape[1] // (2 * nj)
    heads = sw // HEAD_DIM
    nblk = t // blk_rows
    d = HEAD_DIM
    _, n1, n2, m = w4.shape
    kb = n1 // (nj * 2 * nblk)
    assert kb * nj * 2 * nblk == n1

    tblk = lambda p, i: i + p * (nblk - 1 - 2 * i)
    step = lambda j, p, i: (j * 2 + p) * nblk + i
    qspec = lambda sec: pl.BlockSpec((blk_rows, sw), lambda j, p, i: (tblk(p, i), 3 * j + sec))
    fspec = pl.BlockSpec((blk_rows, sw), lambda j, p, i: (tblk(p, i), 2 * j + p))
    dft_blk = (2, kb, n2, m)
    dft_spec = pl.BlockSpec(dft_blk, lambda j, p, i: (0, step(j, p, i), 0, 0))

    pipelined = (5 * _nbytes((blk_rows, sw), BF16) + _nbytes((blk_rows, sw), F32)
                 + 2 * _nbytes(dft_blk, BF16) + _nbytes((2 * n2, 2 * n2), BF16))
    resident = _nbytes((t, sw), F32) + 256 * _nbytes((d, d), F32) + 2 * _nbytes((2 * n2, m), F32)
    return pl.pallas_call(
        functools.partial(_hgrn_kernel, nblk=nblk, blk_rows=blk_rows),
        grid=(nj, 2, nblk),
        in_specs=[
            pl.BlockSpec((1, d), lambda j, p, i: (0, 0)),
            qspec(0), qspec(1),
            pl.BlockSpec((blk_rows, sw), lambda j, p, i: (nblk - 1 - p * i, 3 * j + 2)),
            fspec, fspec,
            pl.BlockSpec((2 * n2, 2 * n2), lambda j, p, i: (0, 0)),
            dft_spec,
        ],
        out_specs=[pl.BlockSpec((blk_rows, sw), lambda j, p, i: (nblk - 1 - p * i, j)), dft_spec],
        out_shape=[jax.ShapeDtypeStruct((t, nj * sw), BF16),
                   jax.ShapeDtypeStruct((2, n1, n2, m), BF16)],
        scratch_shapes=[pltpu.VMEM((heads, d, d), F32), pltpu.VMEM((t, sw), F32)],
        compiler_params=pltpu.CompilerParams(
            dimension_semantics=("arbitrary", "arbitrary", "arbitrary"),
            vmem_limit_bytes=_vmem_limit(pipelined, resident)),
        name="hgrn_fft2",
    )(g_norm, qig, qig, qig, log2f, one_minus_f, f2, w4)


def _dft_constants(t, n1):
    n2 = t // n1
    k1 = np.arange(n1)[None, :, None]
    t1 = np.arange(n1)[None, None, :]
    t2 = np.arange(n2)[:, None, None]
    ang = 2.0 * np.pi * ((k1 * (n2 * t1 + t2)) % t) / t
    g = np.concatenate([np.cos(ang), -np.sin(ang)], axis=1) / math.sqrt(n1)
    k2 = np.arange(n2)[:, None]
    s2 = np.arange(n2)[None, :]
    ang2 = 2.0 * np.pi * ((k2 * s2) % n2) / n2
    c2, sn2 = np.cos(ang2) / math.sqrt(n2), np.sin(ang2) / math.sqrt(n2)
    f2 = np.block([[c2, sn2], [-sn2, c2]])
    m = np.arange(FOURIER_GDIM)
    ang3 = 2.0 * np.pi * ((m[:, None] * m[None, :]) % FOURIER_GDIM) / FOURIER_GDIM
    c3 = np.cos(ang3) / math.sqrt(FOURIER_GDIM)
    s3 = np.sin(ang3) / math.sqrt(FOURIER_GDIM)
    as_bf16 = lambda a: jnp.asarray(a, dtype=F32).astype(BF16)
    return as_bf16(g), as_bf16(f2), as_bf16(c3), as_bf16(s3)


def _fft_t1_kernel(u_ref, g_ref, w_ref, *, tb, n1):
    x = _swap_leading(u_ref[...], merge=True)
    r = jnp.stack([_dot(g_ref[j], x[j * n1:(j + 1) * n1]).astype(BF16) for j in range(tb)])
    r = _swap_leading(r, merge=False)
    w_ref[0] = r[:n1]
    w_ref[1] = r[n1:]


def _fft_t1(four, g, *, tb=BF16_ROWS):
    t, m = four.shape
    n2, _, n1 = g.shape
    u = four.reshape(n1, n2, m)
    pipelined = (_nbytes((n1, tb, m), BF16) + _nbytes((tb, 2 * n1, n1), BF16)
                 + _nbytes((2, n1, tb, m), BF16))
    return pl.pallas_call(
        functools.partial(_fft_t1_kernel, tb=tb, n1=n1),
        grid=(n2 // tb,),
        in_specs=[
            pl.BlockSpec((n1, tb, m), lambda s: (0, s, 0)),
            pl.BlockSpec((tb, 2 * n1, n1), lambda s: (s, 0, 0)),
        ],
        out_specs=pl.BlockSpec((2, n1, tb, m), lambda s: (0, 0, s, 0)),
        out_shape=jax.ShapeDtypeStruct((2, n1, n2, m), BF16),
        compiler_params=pltpu.CompilerParams(
            dimension_semantics=("parallel",),
            vmem_limit_bytes=_vmem_limit(pipelined, 6 * _nbytes((tb * n1, m), F32))),
        name="fft_t1",
    )(u, g)


def _mix_ln_kernel(x_ref, ge_ref, be_ref, oh_ref, pr_ref, pi_ref, c3_ref, s3_ref, wo_ref,
                   g1_ref, b1_ref, h1_ref, pr_scr, pi_scr, *, alpha):
    gd = FOURIER_GDIM
    pr_scr[...] = _swap_leading(pr_ref[...], merge=True)
    pi_scr[...] = _swap_leading(pi_ref[...], merge=True)
    c3, s3 = c3_ref[...], s3_ref[...]
    parts = []
    for g in range(pr_scr.shape[1] // gd):
        sl = slice(g * gd, (g + 1) * gd)
        parts.append((_dot(pr_scr[:, sl], c3) + _dot(pi_scr[:, sl], s3)).astype(BF16))
    o_four = jnp.concatenate(parts, axis=1)
    wh = oh_ref.shape[1]
    mix = _dot(oh_ref[...], wo_ref[:wh, :]) + _dot(o_four, wo_ref[wh:, :])
    h0 = _layernorm(x_ref[...], ge_ref[...], be_ref[...])
    h1_ref[...] = _layernorm(alpha * h0 + mix, g1_ref[...], b1_ref[...])


def _mix_ln(x, ge, be, o_hgrn, p, c3, s3, wo, g1, b1, *, alpha, tm=512):
    t, d = x.shape
    wh = o_hgrn.shape[1]
    _, n1, n2, m = p.shape
    kr = tm // n1
    gd = FOURIER_GDIM
    const = lambda shape: pl.BlockSpec(shape, lambda i: tuple(0 for _ in shape),
                                       pipeline_mode=pl.Buffered(1))
    pipelined = (_nbytes((tm, d), F32) + _nbytes((tm, wh), BF16) + 2 * _nbytes((n1, kr, m), BF16)
                 + _nbytes((tm, d), F32))
    resident = (_nbytes((wh + m, d), BF16) + 2 * _nbytes((tm, m), BF16) + 2 * _nbytes((tm, m), F32)
                + 3 * _nbytes((tm, d), F32))
    return pl.pallas_call(
        functools.partial(_mix_ln_kernel, alpha=alpha),
        grid=(t // tm,),
        in_specs=[
            pl.BlockSpec((tm, d), lambda i: (i, 0)),
            const((1, d)), const((1, d)),
            pl.BlockSpec((tm, wh), lambda i: (i, 0)),
            pl.BlockSpec((None, n1, kr, m), lambda i: (0, 0, i, 0)),
            pl.BlockSpec((None, n1, kr, m), lambda i: (1, 0, i, 0)),
            const((gd, gd)), const((gd, gd)),
            const((wh + m, d)),
            const((1, d)), const((1, d)),
        ],
        out_specs=pl.BlockSpec((tm, d), lambda i: (i, 0)),
        out_shape=jax.ShapeDtypeStruct((t, d), F32),
        scratch_shapes=[pltpu.VMEM((tm, m), BF16), pltpu.VMEM((tm, m), BF16)],
        compiler_params=pltpu.CompilerParams(
            dimension_semantics=("parallel",),
            vmem_limit_bytes=_vmem_limit(pipelined, resident)),
        name="mix_ln",
    )(x, ge, be, o_hgrn, p, p, c3, s3, wo, g1, b1)


def _ffn_ln_kernel(h1_ref, wg_ref, wu_ref, wd_ref, g2_ref, b2_ref, o_ref, hb_scr, *, alpha):
    f = pl.program_id(1)

    def ffn_step(first):
        if first:
            hb = h1_ref[...].astype(BF16)
            hb_scr[...] = hb
        else:
            hb = hb_scr[...]
        gate = _dot(hb, wg_ref[...].astype(BF16))
        up = _dot(hb, wu_ref[...].astype(BF16))
        act = (gate * _sigmoid(gate) * up).astype(BF16)
        cw = FFN_OUT_SLAB
        for n in range(o_ref.shape[1] // cw):
            cols = slice(n * cw, (n + 1) * cw)
            part = _dot(act, wd_ref[:, cols].astype(BF16))
            if first:
                o_ref[:, cols] = alpha * h1_ref[:, cols] + part
            else:
                o_ref[:, cols] += part

    pl.when(f == 0)(functools.partial(ffn_step, True))
    pl.when(f > 0)(functools.partial(ffn_step, False))

    @pl.when(f == pl.num_programs(1) - 1)
    def _():
        for r in range(0, o_ref.shape[0], LN_ROWS):
            rows = slice(r, r + LN_ROWS)
            o_ref[rows, :] = _layernorm(o_ref[rows, :], g2_ref[...], b2_ref[...])


def _ffn_ln(h1, wg, wu, wd, g2, b2, *, alpha, tm=1024, tf=512):
    t, d = h1.shape
    dff = wg.shape[1]
    pipelined = _nbytes((tm, d), F32) + 3 * _nbytes((d, tf), F32)
    resident = (_nbytes((tm, d), F32) + _nbytes((tm, d), BF16) + 2 * _nbytes((tm, tf), F32)
                + 2 * _nbytes((tm, FFN_OUT_SLAB), F32))
    return pl.pallas_call(
        functools.partial(_ffn_ln_kernel, alpha=alpha),
        grid=(t // tm, dff // tf),
        in_specs=[
            pl.BlockSpec((tm, d), lambda i, f: (i, 0)),
            pl.BlockSpec((d, tf), lambda i, f: (0, f)),
            pl.BlockSpec((d, tf), lambda i, f: (0, f)),
            pl.BlockSpec((tf, d), lambda i, f: (f, 0)),
            pl.BlockSpec((1, d), lambda i, f: (0, 0)),
            pl.BlockSpec((1, d), lambda i, f: (0, 0)),
        ],
        out_specs=pl.BlockSpec((tm, d), lambda i, f: (i, 0), pipeline_mode=pl.Buffered(1)),
        out_shape=jax.ShapeDtypeStruct((t, d), F32),
        scratch_shapes=[pltpu.VMEM((tm, d), BF16)],
        compiler_params=pltpu.CompilerParams(
            dimension_semantics=("parallel", "arbitrary"),
            vmem_limit_bytes=_vmem_limit(pipelined, resident)),
        name="ffn_ln",
    )(h1, wg, wu, wd, g2, b2)


def kernel(x, ln_emb_g, ln_emb_b, w_in, lb_fwd_logits, lb_bwd_logits, g_norm, w_out, ln1_g, ln1_b,
           w_gate, w_up, w_down, ln2_g, ln2_b):
    depth = w_in.shape[0]
    assert depth == 1, "the embedding LayerNorm is fused into the single layer's projection"
    batch, seq, d = x.shape
    assert batch == 1
    alpha = (2.0 * depth) ** 0.25
    layer = 0
    row = lambda a: a.reshape(1, -1).astype(F32)

    x2 = x.reshape(seq, d)
    log2f, one_minus_f, qig, four = _ln_proj(
        x2, row(ln_emb_g), row(ln_emb_b), lb_fwd_logits.astype(F32), lb_bwd_logits.astype(F32),
        w_in[layer], layer=layer)

    g1c, f2c, c3, s3 = _dft_constants(seq, FFT_N1)
    o_hgrn, p = _hgrn_fft2(row(g_norm[layer]), qig, log2f, one_minus_f, _fft_t1(four, g1c), f2c)

    wo = w_out[layer].astype(BF16)
    h1 = _mix_ln(x2, row(ln_emb_g), row(ln_emb_b), o_hgrn, p, c3, s3, wo,
                 row(ln1_g[layer]), row(ln1_b[layer]), alpha=alpha)

    out = _ffn_ln(h1, w_gate[layer], w_up[layer], w_down[layer],
                  row(ln2_g[layer]), row(ln2_b[layer]), alpha=alpha)
    return out.reshape(batch, seq, d)
```

```python
import functools
import math

import numpy as np
import jax
import jax.numpy as jnp
from jax import lax
from jax.experimental import pallas as pl
from jax.experimental.pallas import tpu as pltpu

LN_EPS = 1e-5
RMS_EPS = 1e-6
HEAD_DIM = 128
FOURIER_GDIM = 256
FFT_N1 = 32
BF16_ROWS = 16
LN_ROWS = 256
FFN_OUT_SLAB = 512
HGRN_CHUNK = 128
LN_PROJ_STEPS = 4
V7X_VMEM_CAP = 63 * 1024 * 1024
COMPILER_SCRATCH_BYTES = 4 * 1024 * 1024
F32_ROWS = 8

BF16 = jnp.bfloat16
F32 = jnp.float32


def _vmem_limit(pipelined_bytes, resident_bytes):
    return int(min(V7X_VMEM_CAP, 2 * pipelined_bytes + resident_bytes + COMPILER_SCRATCH_BYTES))


def _nbytes(shape, dtype):
    return int(np.prod(shape)) * jnp.dtype(dtype).itemsize


def _dot(a, b):
    return jnp.dot(a, b, preferred_element_type=F32)


def _layernorm(x, g, b):
    mu = jnp.mean(x, axis=-1, keepdims=True)
    xc = x - mu
    var = jnp.mean(xc * xc, axis=-1, keepdims=True)
    return xc * lax.rsqrt(var + LN_EPS) * g + b


def _sigmoid(x):
    return 1.0 / (1.0 + jnp.exp(-x))


def _silu(x):
    return x * _sigmoid(x)


def _swap_leading(x, *, merge):
    a, b, m = x.shape
    y = jnp.swapaxes(x, 0, 1)
    return y.reshape(b * a, m) if merge else y


def _lower_bound(logits_ref, layer):
    logits = logits_ref[...]
    e = jnp.exp(logits - jnp.max(logits, axis=0, keepdims=True))
    return jnp.sum(e[:layer + 1], axis=0, keepdims=True) / jnp.sum(e, axis=0, keepdims=True)


def _ln_proj_kernel(x_ref, g_ref, b_ref, lbf_ref, lbb_ref, wzf_ref, wzb_ref, wq_ref, wi_ref, wg_ref, wf_ref,
                    lf_ref, kk_ref, q_ref, f_ref, h_even, h_odd, *, layer, ntiles):
    r = pl.program_id(0)
    j = pl.program_id(1)
    qrows = x_ref.shape[0]

    def normalise_quarter(h_dst):
        rows = pl.ds(pl.multiple_of(j * qrows, qrows), qrows)
        y = _layernorm(x_ref[...], g_ref[...], b_ref[...]).astype(BF16)
        h_dst[rows, :] = y
        bits = pltpu.bitcast(y, jnp.uint32)
        acc = bits[0:F32_ROWS, :]
        for s in range(F32_ROWS, bits.shape[0], F32_ROWS):
            acc = acc | bits[s:s + F32_ROWS, :]
        zero = ((acc >> 16) >> 16).astype(F32)
        return zero[0:1, 0:wf_ref.shape[1]]

    def project_slab(h_src, zero=None):
        h = h_src[...]
        sw = wf_ref.shape[1]
        proj = lambda w_ref: _dot(h, w_ref[...].astype(BF16))
        for k, (w_ref, lb_ref) in enumerate(((wzf_ref, lbf_ref), (wzb_ref, lbb_ref))):
            lb = _lower_bound(lb_ref, layer)
            if zero is not None and k == 0:
                lb = lb + zero
            f = lb + (1.0 - lb) * _sigmoid(proj(w_ref))
            lf_ref[:, k * sw:(k + 1) * sw] = jnp.log2(f)
            kk_ref[:, k * sw:(k + 1) * sw] = (1.0 - f).astype(BF16)
        q_ref[:, 0:sw] = _silu(proj(wq_ref)).astype(BF16)
        q_ref[:, sw:2 * sw] = proj(wi_ref).astype(BF16)
        q_ref[:, 2 * sw:3 * sw] = _silu(proj(wg_ref)).astype(BF16)
        f_ref[...] = proj(wf_ref).astype(BF16)

    @pl.when(r == 0)
    def _():
        normalise_quarter(h_even)

    middle = jnp.logical_and(r > 0, r < ntiles)

    @pl.when(jnp.logical_and(middle, r % 2 == 1))
    def _():
        project_slab(h_even, normalise_quarter(h_odd))

    @pl.when(jnp.logical_and(middle, r % 2 == 0))
    def _():
        project_slab(h_odd, normalise_quarter(h_even))

    pl.when(r == ntiles)(functools.partial(project_slab, h_odd if ntiles % 2 == 0 else h_even))


def _ln_proj(x, g, b, lbf_logits, lbb_logits, w, *, layer, tm=1024):
    t, d = x.shape
    nj = LN_PROJ_STEPS
    sw = w.shape[1] // (6 * nj)
    slots = lbf_logits.shape[0]
    ntiles = t // tm
    qrows = tm // nj
    col = lambda r, j: jnp.where(r == 0, 0, j)
    wspec = lambda section: pl.BlockSpec((d, sw), lambda r, j: (0, section * nj + col(r, j)))
    ospec = lambda width: pl.BlockSpec((tm, width), lambda r, j: (jnp.maximum(r - 1, 0), col(r, j)))
    pipelined = (_nbytes((qrows, d), F32) + _nbytes((d, 6 * sw), F32)
                 + _nbytes((tm, 2 * sw), F32) + _nbytes((tm, 6 * sw), BF16))
    resident = 2 * _nbytes((tm, d), BF16) + _nbytes((tm, 6 * sw), F32)
    return pl.pallas_call(
        functools.partial(_ln_proj_kernel, layer=layer, ntiles=ntiles),
        grid=(ntiles + 1, nj),
        in_specs=[
            pl.BlockSpec((qrows, d), lambda r, j: (jnp.minimum(r, ntiles - 1) * nj + j, 0)),
            pl.BlockSpec((1, d), lambda r, j: (0, 0)),
            pl.BlockSpec((1, d), lambda r, j: (0, 0)),
            pl.BlockSpec((slots, sw), lambda r, j: (0, j)),
            pl.BlockSpec((slots, sw), lambda r, j: (0, j)),
            wspec(2), wspec(3), wspec(0), wspec(1), wspec(4), wspec(5),
        ],
        out_specs=[ospec(2 * sw), ospec(2 * sw), ospec(3 * sw), ospec(sw)],
        out_shape=[
            jax.ShapeDtypeStruct((t, nj * 2 * sw), F32),
            jax.ShapeDtypeStruct((t, nj * 2 * sw), BF16),
            jax.ShapeDtypeStruct((t, nj * 3 * sw), BF16),
            jax.ShapeDtypeStruct((t, nj * sw), BF16),
        ],
        scratch_shapes=[pltpu.VMEM((tm, d), BF16), pltpu.VMEM((tm, d), BF16)],
        compiler_params=pltpu.CompilerParams(
            dimension_semantics=("arbitrary", "arbitrary"),
            vmem_limit_bytes=_vmem_limit(pipelined, resident)),
        name="ln_proj",
    )(x, g, b, lbf_logits, lbb_logits, w, w, w, w, w, w)


def _hgrn_kernel(gn_ref, q_ref, v_ref, g_ref, lf_ref, kk_ref, f2_ref, w_ref, o_ref, p_ref, st_ref, of_ref,
                 *, nblk, blk_rows):
    c = HGRN_CHUNK
    d = HEAD_DIM
    half = c // 2
    nchunk = blk_rows // c
    heads = st_ref.shape[0]
    phase = pl.program_id(1)
    t = pl.program_id(2)

    @pl.when(t == 0)
    def _():
        st_ref[...] = jnp.zeros_like(st_ref)

    def scan_block(fwd):
        row = lax.broadcasted_iota(jnp.int32, (c, c), 0)
        col = lax.broadcasted_iota(jnp.int32, (c, c), 1)
        tri = (col <= row) if fwd else (col >= row)
        tri2 = jnp.concatenate([tri.astype(BF16)] * 2, axis=1)
        blk = t if fwd else nblk - 1 - t
        chunk_rows = [pl.ds((ci if fwd else nchunk - 1 - ci) * c, c) for ci in range(nchunk)]
        head_cols = [slice(h * d, (h + 1) * d) for h in range(heads)]
        trans_a = (((0,), (0,)), ((), ()))
        n2 = w_ref.shape[2]

        def dft_slice(k):
            res = _dot(f2_ref[...], jnp.concatenate([w_ref[0, k], w_ref[1, k]], axis=0))
            p_ref[0, k] = res[:n2].astype(BF16)
            p_ref[1, k] = res[n2:].astype(BF16)

        bs = []
        for rows in chunk_rows:
            lf = lf_ref[rows, :]
            hi = lf.astype(BF16)
            lo = (lf - hi.astype(F32)).astype(BF16)
            bs.append(_dot(tri2, jnp.concatenate([hi, lo], axis=0)))
        for k in range(0, w_ref.shape[1], 2):
            dft_slice(k)

        qds, dsts, scs, decs, mids = [], [], [], [], []
        for rows, b in zip(chunk_rows, bs):
            b_mid = b[half - 1:half, :] if fwd else b[half:half + 1, :]
            b_end = b[c - 1:c, :] if fwd else b[0:1, :]
            kk = kk_ref[rows, :].astype(F32)
            qd = (q_ref[rows, :].astype(F32) * jnp.exp2(b - b_mid)).astype(BF16)
            ki = kk * jnp.exp2(b_mid - b)
            ke = (ki * jnp.exp2(b_end - b_mid)).astype(BF16)
            v = v_ref[rows, :]
            qds.append(qd)
            scs.append([_dot(qd[:, hs], ki[:, hs].T.astype(BF16)) for hs in head_cols])
            dsts.append([lax.dot_general(v[:, hs], ke[:, hs], trans_a, preferred_element_type=F32)
                         for hs in head_cols])
            decs.append(jnp.exp2(b_end))
            mids.append(jnp.exp2(b_mid))
        for k in range(1, w_ref.shape[1], 2):
            dft_slice(k)

        s_mids = [[] for _ in chunk_rows]
        for h, hs in enumerate(head_cols):
            st = st_ref[h]
            for ci in range(nchunk):
                s_mids[ci].append((st * mids[ci][:, hs]).T.astype(BF16))
                st = decs[ci][:, hs] * st + dsts[ci][h]
            st_ref[h] = st

        for ci, rows in enumerate(chunk_rows):
            v = v_ref[rows, :]
            outs = []
            for h, hs in enumerate(head_cols):
                scores = jnp.where(tri, scs[ci][h], 0.0).astype(BF16)
                o = _dot(scores, v[:, hs]) + _dot(qds[ci][:, hs], s_mids[ci][h])
                if not fwd:
                    o = o + of_ref[pl.ds(pl.multiple_of(blk * blk_rows, c) + rows.start, c), hs]
                    o = o * lax.rsqrt(jnp.mean(o * o, axis=-1, keepdims=True) + RMS_EPS) * gn_ref[...]
                outs.append(o)
            o = jnp.concatenate(outs, axis=1)
            if fwd:
                of_ref[pl.ds(pl.multiple_of(blk * blk_rows, c) + rows.start, c), :] = o
            else:
                o_ref[rows, :] = (o * g_ref[rows, :].astype(F32)).astype(BF16)

    pl.when(phase == 0)(functools.partial(scan_block, True))
    pl.when(phase == 1)(functools.partial(scan_block, False))


def _hgrn_fft2(g_norm, qig, log2f, one_minus_f, w4, f2, *, blk_rows=2048):
    t = log2f.shape[0]
    nj = LN_PROJ_STEPS
    sw = log2f.shape[1] // (2 * nj)
    heads = sw // HEAD_DIM
    nblk = t // blk_rows
    d = HEAD_DIM
    _, n1, n2, m = w4.shape
    kb = n1 // (nj * 2 * nblk)
    assert kb * nj * 2 * nblk == n1

    tblk = lambda p, i: i + p * (nblk - 1 - 2 * i)
    step = lambda j, p, i: (j * 2 + p) * nblk + i
    qspec = lambda sec: pl.BlockSpec((blk_rows, sw), lambda j, p, i: (tblk(p, i), 3 * j + sec))
    fspec = pl.BlockSpec((blk_rows, sw), lambda j, p, i: (tblk(p, i), 2 * j + p))
    dft_blk = (2, kb, n2, m)
    dft_spec = pl.BlockSpec(dft_blk, lambda j, p, i: (0, step(j, p, i), 0, 0))

    pipelined = (5 * _nbytes((blk_rows, sw), BF16) + _nbytes((blk_rows, sw), F32)
                 + 2 * _nbytes(dft_blk, BF16) + _nbytes((2 * n2, 2 * n2), BF16))
    resident = _nbytes((t, sw), F32) + 256 * _nbytes((d, d), F32) + 2 * _nbytes((2 * n2, m), F32)
    return pl.pallas_call(
        functools.partial(_hgrn_kernel, nblk=nblk, blk_rows=blk_rows),
        grid=(nj, 2, nblk),
        in_specs=[
            pl.BlockSpec((1, d), lambda j, p, i: (0, 0)),
            qspec(0), qspec(1),
            pl.BlockSpec((blk_rows, sw), lambda j, p, i: (nblk - 1 - p * i, 3 * j + 2)),
            fspec, fspec,
            pl.BlockSpec((2 * n2, 2 * n2), lambda j, p, i: (0, 0)),
            dft_spec,
        ],
        out_specs=[pl.BlockSpec((blk_rows, sw), lambda j, p, i: (nblk - 1 - p * i, j)), dft_spec],
        out_shape=[jax.ShapeDtypeStruct((t, nj * sw), BF16),
                   jax.ShapeDtypeStruct((2, n1, n2, m), BF16)],
        scratch_shapes=[pltpu.VMEM((heads, d, d), F32), pltpu.VMEM((t, sw), F32)],
        compiler_params=pltpu.CompilerParams(
            dimension_semantics=("arbitrary", "arbitrary", "arbitrary"),
            vmem_limit_bytes=_vmem_limit(pipelined, resident)),
        name="hgrn_fft2",
    )(g_norm, qig, qig, qig, log2f, one_minus_f, f2, w4)


def _dft_constants(t, n1):
    n2 = t // n1
    k1 = np.arange(n1)[None, :, None]
    t1 = np.arange(n1)[None, None, :]
    t2 = np.arange(n2)[:, None, None]
    ang = 2.0 * np.pi * ((k1 * (n2 * t1 + t2)) % t) / t
    g = np.concatenate([np.cos(ang), -np.sin(ang)], axis=1) / math.sqrt(n1)
    k2 = np.arange(n2)[:, None]
    s2 = np.arange(n2)[None, :]
    ang2 = 2.0 * np.pi * ((k2 * s2) % n2) / n2
    c2, sn2 = np.cos(ang2) / math.sqrt(n2), np.sin(ang2) / math.sqrt(n2)
    f2 = np.block([[c2, sn2], [-sn2, c2]])
    m = np.arange(FOURIER_GDIM)
    ang3 = 2.0 * np.pi * ((m[:, None] * m[None, :]) % FOURIER_GDIM) / FOURIER_GDIM
    c3 = np.cos(ang3) / math.sqrt(FOURIER_GDIM)
    s3 = np.sin(ang3) / math.sqrt(FOURIER_GDIM)
    as_bf16 = lambda a: jnp.asarray(a, dtype=F32).astype(BF16)
    return as_bf16(g), as_bf16(f2), as_bf16(c3), as_bf16(s3)


def _fft_t1_kernel(u_ref, g_ref, w_ref, *, tb, n1):
    x = _swap_leading(u_ref[...], merge=True)
    r = jnp.stack([_dot(g_ref[j], x[j * n1:(j + 1) * n1]).astype(BF16) for j in range(tb)])
    r = _swap_leading(r, merge=False)
    w_ref[0] = r[:n1]
    w_ref[1] = r[n1:]


def _fft_t1(four, g, *, tb=2 * BF16_ROWS):
    t, m = four.shape
    n2, _, n1 = g.shape
    u = four.reshape(n1, n2, m)
    pipelined = (_nbytes((n1, tb, m), BF16) + _nbytes((tb, 2 * n1, n1), BF16)
                 + _nbytes((2, n1, tb, m), BF16))
    return pl.pallas_call(
        functools.partial(_fft_t1_kernel, tb=tb, n1=n1),
        grid=(n2 // tb,),
        in_specs=[
            pl.BlockSpec((n1, tb, m), lambda s: (0, s, 0)),
            pl.BlockSpec((tb, 2 * n1, n1), lambda s: (s, 0, 0)),
        ],
        out_specs=pl.BlockSpec((2, n1, tb, m), lambda s: (0, 0, s, 0)),
        out_shape=jax.ShapeDtypeStruct((2, n1, n2, m), BF16),
        compiler_params=pltpu.CompilerParams(
            dimension_semantics=("parallel",),
            vmem_limit_bytes=_vmem_limit(pipelined, 6 * _nbytes((tb * n1, m), F32))),
        name="fft_t1",
    )(u, g)


def _mix_ln_kernel(x_ref, ge_ref, be_ref, oh_ref, pr_ref, pi_ref, c3_ref, s3_ref, wo_ref,
                   g1_ref, b1_ref, h1_ref, pr_scr, pi_scr, wo_scr, *, alpha):
    gd = FOURIER_GDIM

    @pl.when(pl.program_id(0) == 0)
    def _():
        for r in range(0, wo_ref.shape[0], LN_ROWS):
            wo_scr[r:r + LN_ROWS, :] = wo_ref[r:r + LN_ROWS, :].astype(BF16)

    pr_scr[...] = _swap_leading(pr_ref[...], merge=True)
    pi_scr[...] = _swap_leading(pi_ref[...], merge=True)
    c3, s3 = c3_ref[...], s3_ref[...]
    parts = []
    for g in range(pr_scr.shape[1] // gd):
        sl = slice(g * gd, (g + 1) * gd)
        parts.append((_dot(pr_scr[:, sl], c3) + _dot(pi_scr[:, sl], s3)).astype(BF16))
    o_four = jnp.concatenate(parts, axis=1)
    wh = oh_ref.shape[1]
    mix = _dot(oh_ref[...], wo_scr[:wh, :]) + _dot(o_four, wo_scr[wh:, :])
    h0 = _layernorm(x_ref[...], ge_ref[...], be_ref[...])
    h1_ref[...] = _layernorm(alpha * h0 + mix, g1_ref[...], b1_ref[...])


def _mix_ln(x, ge, be, o_hgrn, p, c3, s3, wo, g1, b1, *, alpha, tm=512):
    t, d = x.shape
    wh = o_hgrn.shape[1]
    _, n1, n2, m = p.shape
    kr = tm // n1
    gd = FOURIER_GDIM
    const = lambda shape: pl.BlockSpec(shape, lambda i: tuple(0 for _ in shape),
                                       pipeline_mode=pl.Buffered(1))
    pipelined = (_nbytes((tm, d), F32) + _nbytes((tm, wh), BF16) + 2 * _nbytes((n1, kr, m), BF16)
                 + _nbytes((tm, d), F32))
    resident = (_nbytes((wh + m, d), BF16) + _nbytes((wh + m, d), F32)
                + 2 * _nbytes((tm, m), BF16) + 2 * _nbytes((tm, m), F32)
                + 3 * _nbytes((tm, d), F32))
    return pl.pallas_call(
        functools.partial(_mix_ln_kernel, alpha=alpha),
        grid=(t // tm,),
        in_specs=[
            pl.BlockSpec((tm, d), lambda i: (i, 0)),
            const((1, d)), const((1, d)),
            pl.BlockSpec((tm, wh), lambda i: (i, 0)),
            pl.BlockSpec((None, n1, kr, m), lambda i: (0, 0, i, 0)),
            pl.BlockSpec((None, n1, kr, m), lambda i: (1, 0, i, 0)),
            const((gd, gd)), const((gd, gd)),
            const((wh + m, d)),
            const((1, d)), const((1, d)),
        ],
        out_specs=pl.BlockSpec((tm, d), lambda i: (i, 0)),
        out_shape=jax.ShapeDtypeStruct((t, d), F32),
        scratch_shapes=[pltpu.VMEM((tm, m), BF16), pltpu.VMEM((tm, m), BF16), pltpu.VMEM((wh + m, d), BF16)],
        compiler_params=pltpu.CompilerParams(
            dimension_semantics=("arbitrary",),
            vmem_limit_bytes=_vmem_limit(pipelined, resident)),
        name="mix_ln",
    )(x, ge, be, o_hgrn, p, p, c3, s3, wo, g1, b1)


def _ffn_ln_kernel(h1_ref, wg_ref, wu_ref, wd_ref, g2_ref, b2_ref, o_ref, hb_scr, *, alpha):
    f = pl.program_id(1)

    def ffn_step(first):
        if first:
            hb = h1_ref[...].astype(BF16)
            hb_scr[...] = hb
        else:
            hb = hb_scr[...]
        gate = _dot(hb, wg_ref[...].astype(BF16))
        up = _dot(hb, wu_ref[...].astype(BF16))
        act = (gate * _sigmoid(gate) * up).astype(BF16)
        cw = FFN_OUT_SLAB
        for n in range(o_ref.shape[1] // cw):
            cols = slice(n * cw, (n + 1) * cw)
            part = _dot(act, wd_ref[:, cols].astype(BF16))
            if first:
                o_ref[:, cols] = alpha * h1_ref[:, cols] + part
            else:
                o_ref[:, cols] += part

    pl.when(f == 0)(functools.partial(ffn_step, True))
    pl.when(f > 0)(functools.partial(ffn_step, False))

    @pl.when(f == pl.num_programs(1) - 1)
    def _():
        for r in range(0, o_ref.shape[0], LN_ROWS):
            rows = slice(r, r + LN_ROWS)
            o_ref[rows, :] = _layernorm(o_ref[rows, :], g2_ref[...], b2_ref[...])


def _ffn_ln(h1, wg, wu, wd, g2, b2, *, alpha, tm=1024, tf=512):
    t, d = h1.shape
    dff = wg.shape[1]
    pipelined = _nbytes((tm, d), F32) + 3 * _nbytes((d, tf), F32)
    resident = (_nbytes((tm, d), F32) + _nbytes((tm, d), BF16) + 2 * _nbytes((tm, tf), F32)
                + 2 * _nbytes((tm, FFN_OUT_SLAB), F32))
    return pl.pallas_call(
        functools.partial(_ffn_ln_kernel, alpha=alpha),
        grid=(t // tm, dff // tf),
        in_specs=[
            pl.BlockSpec((tm, d), lambda i, f: (i, 0)),
            pl.BlockSpec((d, tf), lambda i, f: (0, f)),
            pl.BlockSpec((d, tf), lambda i, f: (0, f)),
            pl.BlockSpec((tf, d), lambda i, f: (f, 0)),
            pl.BlockSpec((1, d), lambda i, f: (0, 0)),
            pl.BlockSpec((1, d), lambda i, f: (0, 0)),
        ],
        out_specs=pl.BlockSpec((tm, d), lambda i, f: (i, 0), pipeline_mode=pl.Buffered(1)),
        out_shape=jax.ShapeDtypeStruct((t, d), F32),
        scratch_shapes=[pltpu.VMEM((tm, d), BF16)],
        compiler_params=pltpu.CompilerParams(
            dimension_semantics=("parallel", "arbitrary"),
            vmem_limit_bytes=_vmem_limit(pipelined, resident)),
        name="ffn_ln",
    )(h1, wg, wu, wd, g2, b2)


def kernel(x, ln_emb_g, ln_emb_b, w_in, lb_fwd_logits, lb_bwd_logits, g_norm, w_out, ln1_g, ln1_b,
           w_gate, w_up, w_down, ln2_g, ln2_b):
    depth = w_in.shape[0]
    assert depth == 1, "the embedding LayerNorm is fused into the single layer's projection"
    batch, seq, d = x.shape
    assert batch == 1
    alpha = (2.0 * depth) ** 0.25
    layer = 0
    row = lambda a: a.reshape(1, -1).astype(F32)

    x2 = x.reshape(seq, d)
    log2f, one_minus_f, qig, four = _ln_proj(
        x2, row(ln_emb_g), row(ln_emb_b), lb_fwd_logits.astype(F32), lb_bwd_logits.astype(F32),
        w_in[layer], layer=layer)

    g1c, f2c, c3, s3 = _dft_constants(seq, FFT_N1)
    o_hgrn, p = _hgrn_fft2(row(g_norm[layer]), qig, log2f, one_minus_f, _fft_t1(four, g1c), f2c)

    h1 = _mix_ln(x2, row(ln_emb_g), row(ln_emb_b), o_hgrn, p, c3, s3, w_out[layer],
                 row(ln1_g[layer]), row(ln1_b[layer]), alpha=alpha)

    out = _ffn_ln(h1, w_gate[layer], w_up[layer], w_down[layer],
                  row(ln2_g[layer]), row(ln2_b[layer]), alpha=alpha)
    return out.reshape(batch, seq, d)
```

```python
import functools
import math

import numpy as np
import jax
import jax.numpy as jnp
from jax import lax
from jax.experimental import pallas as pl
from jax.experimental.pallas import tpu as pltpu

LN_EPS = 1e-5
RMS_EPS = 1e-6
HEAD_DIM = 128
FOURIER_GDIM = 256
FFT_N1 = 32
BF16_ROWS = 16
LN_ROWS = 256
FFN_OUT_SLAB = 512
HGRN_CHUNK = 128
LN_PROJ_STEPS = 4
V7X_VMEM_CAP = 63 * 1024 * 1024
COMPILER_SCRATCH_BYTES = 4 * 1024 * 1024
F32_ROWS = 8

BF16 = jnp.bfloat16
F32 = jnp.float32


def _vmem_limit(pipelined_bytes, resident_bytes):
    return int(min(V7X_VMEM_CAP, 2 * pipelined_bytes + resident_bytes + COMPILER_SCRATCH_BYTES))


def _nbytes(shape, dtype):
    return int(np.prod(shape)) * jnp.dtype(dtype).itemsize


def _dot(a, b):
    return jnp.dot(a, b, preferred_element_type=F32)


def _layernorm(x, g, b):
    mu = jnp.mean(x, axis=-1, keepdims=True)
    xc = x - mu
    var = jnp.mean(xc * xc, axis=-1, keepdims=True)
    return xc * lax.rsqrt(var + LN_EPS) * g + b


def _sigmoid(x):
    return 1.0 / (1.0 + jnp.exp(-x))


def _silu(x):
    return x * _sigmoid(x)


def _swap_leading(x, *, merge):
    a, b, m = x.shape
    y = jnp.swapaxes(x, 0, 1)
    return y.reshape(b * a, m) if merge else y


def _lower_bound(logits_ref, layer):
    logits = logits_ref[...]
    e = jnp.exp(logits - jnp.max(logits, axis=0, keepdims=True))
    return jnp.sum(e[:layer + 1], axis=0, keepdims=True) / jnp.sum(e, axis=0, keepdims=True)


def _ln_proj_kernel(x_ref, g_ref, b_ref, lbf_ref, lbb_ref, wzf_ref, wzb_ref, wq_ref, wi_ref, wg_ref, wf_ref,
                    lf_ref, kk_ref, q_ref, f_ref, h_even, h_odd, *, layer, ntiles):
    r = pl.program_id(0)
    j = pl.program_id(1)
    qrows = x_ref.shape[0]

    def normalise_quarter(h_dst):
        rows = pl.ds(pl.multiple_of(j * qrows, qrows), qrows)
        y = _layernorm(x_ref[...], g_ref[...], b_ref[...]).astype(BF16)
        h_dst[rows, :] = y
        bits = pltpu.bitcast(y, jnp.uint32)
        acc = bits[0:F32_ROWS, :]
        for s in range(F32_ROWS, bits.shape[0], F32_ROWS):
            acc = acc | bits[s:s + F32_ROWS, :]
        zero = ((acc >> 16) >> 16).astype(F32)
        return zero[0:1, 0:wf_ref.shape[1]]

    def project_slab(h_src, zero=None):
        h = h_src[...]
        sw = wf_ref.shape[1]
        proj = lambda w_ref: _dot(h, w_ref[...].astype(BF16))
        for k, (w_ref, lb_ref) in enumerate(((wzf_ref, lbf_ref), (wzb_ref, lbb_ref))):
            lb = _lower_bound(lb_ref, layer)
            if zero is not None and k == 0:
                lb = lb + zero
            f = lb + (1.0 - lb) * _sigmoid(proj(w_ref))
            lf_ref[:, k * sw:(k + 1) * sw] = jnp.log2(f)
            kk_ref[:, k * sw:(k + 1) * sw] = (1.0 - f).astype(BF16)
        q_ref[:, 0:sw] = _silu(proj(wq_ref)).astype(BF16)
        q_ref[:, sw:2 * sw] = proj(wi_ref).astype(BF16)
        q_ref[:, 2 * sw:3 * sw] = _silu(proj(wg_ref)).astype(BF16)
        f_ref[...] = proj(wf_ref).astype(BF16)

    @pl.when(r == 0)
    def _():
        normalise_quarter(h_even)

    middle = jnp.logical_and(r > 0, r < ntiles)

    @pl.when(jnp.logical_and(middle, r % 2 == 1))
    def _():
        project_slab(h_even, normalise_quarter(h_odd))

    @pl.when(jnp.logical_and(middle, r % 2 == 0))
    def _():
        project_slab(h_odd, normalise_quarter(h_even))

    pl.when(r == ntiles)(functools.partial(project_slab, h_odd if ntiles % 2 == 0 else h_even))


def _ln_proj(x, g, b, lbf_logits, lbb_logits, w, *, layer, tm=1024):
    t, d = x.shape
    nj = LN_PROJ_STEPS
    sw = w.shape[1] // (6 * nj)
    slots = lbf_logits.shape[0]
    ntiles = t // tm
    qrows = tm // nj
    col = lambda r, j: jnp.where(r == 0, 0, j)
    wspec = lambda section: pl.BlockSpec((d, sw), lambda r, j: (0, section * nj + col(r, j)))
    ospec = lambda width: pl.BlockSpec((tm, width), lambda r, j: (jnp.maximum(r - 1, 0), col(r, j)))
    pipelined = (_nbytes((qrows, d), F32) + _nbytes((d, 6 * sw), F32)
                 + _nbytes((tm, 2 * sw), F32) + _nbytes((tm, 6 * sw), BF16))
    resident = 2 * _nbytes((tm, d), BF16) + _nbytes((tm, 6 * sw), F32)
    return pl.pallas_call(
        functools.partial(_ln_proj_kernel, layer=layer, ntiles=ntiles),
        grid=(ntiles + 1, nj),
        in_specs=[
            pl.BlockSpec((qrows, d), lambda r, j: (jnp.minimum(r, ntiles - 1) * nj + j, 0)),
            pl.BlockSpec((1, d), lambda r, j: (0, 0)),
            pl.BlockSpec((1, d), lambda r, j: (0, 0)),
            pl.BlockSpec((slots, sw), lambda r, j: (0, j)),
            pl.BlockSpec((slots, sw), lambda r, j: (0, j)),
            wspec(2), wspec(3), wspec(0), wspec(1), wspec(4), wspec(5),
        ],
        out_specs=[ospec(2 * sw), ospec(2 * sw), ospec(3 * sw), ospec(sw)],
        out_shape=[
            jax.ShapeDtypeStruct((t, nj * 2 * sw), F32),
            jax.ShapeDtypeStruct((t, nj * 2 * sw), BF16),
            jax.ShapeDtypeStruct((t, nj * 3 * sw), BF16),
            jax.ShapeDtypeStruct((t, nj * sw), BF16),
        ],
        scratch_shapes=[pltpu.VMEM((tm, d), BF16), pltpu.VMEM((tm, d), BF16)],
        compiler_params=pltpu.CompilerParams(
            dimension_semantics=("arbitrary", "arbitrary"),
            vmem_limit_bytes=_vmem_limit(pipelined, resident)),
        name="ln_proj",
    )(x, g, b, lbf_logits, lbb_logits, w, w, w, w, w, w)


def _hgrn_kernel(gn_ref, q_ref, v_ref, g_ref, lf_ref, kk_ref, f2_ref, w_ref, wdf_ref,
                 o_ref, p_ref, wdb_ref, st_ref, of_ref, *, nblk, blk_rows):
    c = HGRN_CHUNK
    d = HEAD_DIM
    half = c // 2
    nchunk = blk_rows // c
    heads = st_ref.shape[0]
    phase = pl.program_id(1)
    t = pl.program_id(2)

    @pl.when(t == 0)
    def _():
        st_ref[...] = jnp.zeros_like(st_ref)

    def scan_block(fwd):
        row = lax.broadcasted_iota(jnp.int32, (c, c), 0)
        col = lax.broadcasted_iota(jnp.int32, (c, c), 1)
        tri = (col <= row) if fwd else (col >= row)
        tri2 = jnp.concatenate([tri.astype(BF16)] * 2, axis=1)
        blk = t if fwd else nblk - 1 - t
        chunk_rows = [pl.ds((ci if fwd else nchunk - 1 - ci) * c, c) for ci in range(nchunk)]
        head_cols = [slice(h * d, (h + 1) * d) for h in range(heads)]
        trans_a = (((0,), (0,)), ((), ()))
        n2 = w_ref.shape[2]
        wdb_ref[...] = wdf_ref[...].astype(BF16)

        def dft_slice(k):
            res = _dot(f2_ref[...], jnp.concatenate([w_ref[0, k], w_ref[1, k]], axis=0))
            p_ref[0, k] = res[:n2].astype(BF16)
            p_ref[1, k] = res[n2:].astype(BF16)

        bs = []
        for rows in chunk_rows:
            lf = lf_ref[rows, :]
            hi = lf.astype(BF16)
            lo = (lf - hi.astype(F32)).astype(BF16)
            bs.append(_dot(tri2, jnp.concatenate([hi, lo], axis=0)))
        for k in range(0, w_ref.shape[1], 2):
            dft_slice(k)

        qds, dsts, scs, decs, mids = [], [], [], [], []
        for rows, b in zip(chunk_rows, bs):
            b_mid = b[half - 1:half, :] if fwd else b[half:half + 1, :]
            b_end = b[c - 1:c, :] if fwd else b[0:1, :]
            kk = kk_ref[rows, :].astype(F32)
            qd = (q_ref[rows, :].astype(F32) * jnp.exp2(b - b_mid)).astype(BF16)
            ki = kk * jnp.exp2(b_mid - b)
            ke = (ki * jnp.exp2(b_end - b_mid)).astype(BF16)
            v = v_ref[rows, :]
            qds.append(qd)
            scs.append([_dot(qd[:, hs], ki[:, hs].T.astype(BF16)) for hs in head_cols])
            dsts.append([lax.dot_general(v[:, hs], ke[:, hs], trans_a, preferred_element_type=F32)
                         for hs in head_cols])
            decs.append(jnp.exp2(b_end))
            mids.append(jnp.exp2(b_mid))
        for k in range(1, w_ref.shape[1], 2):
            dft_slice(k)

        s_mids = [[] for _ in chunk_rows]
        for h, hs in enumerate(head_cols):
            st = st_ref[h]
            for ci in range(nchunk):
                s_mids[ci].append((st * mids[ci][:, hs]).T.astype(BF16))
                st = decs[ci][:, hs] * st + dsts[ci][h]
            st_ref[h] = st

        for ci, rows in enumerate(chunk_rows):
            v = v_ref[rows, :]
            outs = []
            for h, hs in enumerate(head_cols):
                scores = jnp.where(tri, scs[ci][h], 0.0).astype(BF16)
                o = _dot(scores, v[:, hs]) + _dot(qds[ci][:, hs], s_mids[ci][h])
                if not fwd:
                    o = o + of_ref[pl.ds(pl.multiple_of(blk * blk_rows, c) + rows.start, c), hs]
                    o = o * lax.rsqrt(jnp.mean(o * o, axis=-1, keepdims=True) + RMS_EPS) * gn_ref[...]
                outs.append(o)
            o = jnp.concatenate(outs, axis=1)
            if fwd:
                of_ref[pl.ds(pl.multiple_of(blk * blk_rows, c) + rows.start, c), :] = o
            else:
                o_ref[rows, :] = (o * g_ref[rows, :].astype(F32)).astype(BF16)

    pl.when(phase == 0)(functools.partial(scan_block, True))
    pl.when(phase == 1)(functools.partial(scan_block, False))


def _hgrn_fft2(g_norm, qig, log2f, one_minus_f, w4, f2, w_down, *, blk_rows=2048):
    t = log2f.shape[0]
    nj = LN_PROJ_STEPS
    sw = log2f.shape[1] // (2 * nj)
    heads = sw // HEAD_DIM
    nblk = t // blk_rows
    d = HEAD_DIM
    _, n1, n2, m = w4.shape
    kb = n1 // (nj * 2 * nblk)
    assert kb * nj * 2 * nblk == n1

    tblk = lambda p, i: i + p * (nblk - 1 - 2 * i)
    step = lambda j, p, i: (j * 2 + p) * nblk + i
    qspec = lambda sec: pl.BlockSpec((blk_rows, sw), lambda j, p, i: (tblk(p, i), 3 * j + sec))
    fspec = pl.BlockSpec((blk_rows, sw), lambda j, p, i: (tblk(p, i), 2 * j + p))
    dft_blk = (2, kb, n2, m)
    dft_spec = pl.BlockSpec(dft_blk, lambda j, p, i: (0, step(j, p, i), 0, 0))
    dff, dmodel = w_down.shape
    wd_rows = dff // (nj * 2 * nblk)
    assert wd_rows * nj * 2 * nblk == dff and wd_rows % BF16_ROWS == 0
    wd_spec = pl.BlockSpec((wd_rows, dmodel), lambda j, p, i: (step(j, p, i), 0))

    pipelined = (5 * _nbytes((blk_rows, sw), BF16) + _nbytes((blk_rows, sw), F32)
                 + 2 * _nbytes(dft_blk, BF16) + _nbytes((2 * n2, 2 * n2), BF16)
                 + _nbytes((wd_rows, dmodel), F32) + _nbytes((wd_rows, dmodel), BF16))
    resident = _nbytes((t, sw), F32) + 256 * _nbytes((d, d), F32) + 2 * _nbytes((2 * n2, m), F32)
    return pl.pallas_call(
        functools.partial(_hgrn_kernel, nblk=nblk, blk_rows=blk_rows),
        grid=(nj, 2, nblk),
        in_specs=[
            pl.BlockSpec((1, d), lambda j, p, i: (0, 0)),
            qspec(0), qspec(1),
            pl.BlockSpec((blk_rows, sw), lambda j, p, i: (nblk - 1 - p * i, 3 * j + 2)),
            fspec, fspec,
            pl.BlockSpec((2 * n2, 2 * n2), lambda j, p, i: (0, 0)),
            dft_spec,
            wd_spec,
        ],
        out_specs=[pl.BlockSpec((blk_rows, sw), lambda j, p, i: (nblk - 1 - p * i, j)), dft_spec, wd_spec],
        out_shape=[jax.ShapeDtypeStruct((t, nj * sw), BF16),
                   jax.ShapeDtypeStruct((2, n1, n2, m), BF16),
                   jax.ShapeDtypeStruct((dff, dmodel), BF16)],
        scratch_shapes=[pltpu.VMEM((heads, d, d), F32), pltpu.VMEM((t, sw), F32)],
        compiler_params=pltpu.CompilerParams(
            dimension_semantics=("arbitrary", "arbitrary", "arbitrary"),
            vmem_limit_bytes=_vmem_limit(pipelined, resident)),
        name="hgrn_fft2",
    )(g_norm, qig, qig, qig, log2f, one_minus_f, f2, w4, w_down)


def _dft_constants(t, n1):
    n2 = t // n1
    k1 = np.arange(n1)[None, :, None]
    t1 = np.arange(n1)[None, None, :]
    t2 = np.arange(n2)[:, None, None]
    ang = 2.0 * np.pi * ((k1 * (n2 * t1 + t2)) % t) / t
    g = np.concatenate([np.cos(ang), -np.sin(ang)], axis=1) / math.sqrt(n1)
    k2 = np.arange(n2)[:, None]
    s2 = np.arange(n2)[None, :]
    ang2 = 2.0 * np.pi * ((k2 * s2) % n2) / n2
    c2, sn2 = np.cos(ang2) / math.sqrt(n2), np.sin(ang2) / math.sqrt(n2)
    f2 = np.block([[c2, sn2], [-sn2, c2]])
    m = np.arange(FOURIER_GDIM)
    ang3 = 2.0 * np.pi * ((m[:, None] * m[None, :]) % FOURIER_GDIM) / FOURIER_GDIM
    c3 = np.cos(ang3) / math.sqrt(FOURIER_GDIM)
    s3 = np.sin(ang3) / math.sqrt(FOURIER_GDIM)
    as_bf16 = lambda a: jnp.asarray(a, dtype=F32).astype(BF16)
    return as_bf16(g), as_bf16(f2), as_bf16(c3), as_bf16(s3)


def _fft_t1_kernel(u_ref, g_ref, w_ref, *, tb, n1):
    x = _swap_leading(u_ref[...], merge=True)
    r = jnp.stack([_dot(g_ref[j], x[j * n1:(j + 1) * n1]).astype(BF16) for j in range(tb)])
    r = _swap_leading(r, merge=False)
    w_ref[0] = r[:n1]
    w_ref[1] = r[n1:]


def _fft_t1(four, g, *, tb=BF16_ROWS):
    t, m = four.shape
    n2, _, n1 = g.shape
    u = four.reshape(n1, n2, m)
    pipelined = (_nbytes((n1, tb, m), BF16) + _nbytes((tb, 2 * n1, n1), BF16)
                 + _nbytes((2, n1, tb, m), BF16))
    return pl.pallas_call(
        functools.partial(_fft_t1_kernel, tb=tb, n1=n1),
        grid=(n2 // tb,),
        in_specs=[
            pl.BlockSpec((n1, tb, m), lambda s: (0, s, 0)),
            pl.BlockSpec((tb, 2 * n1, n1), lambda s: (s, 0, 0)),
        ],
        out_specs=pl.BlockSpec((2, n1, tb, m), lambda s: (0, 0, s, 0)),
        out_shape=jax.ShapeDtypeStruct((2, n1, n2, m), BF16),
        compiler_params=pltpu.CompilerParams(
            dimension_semantics=("parallel",),
            vmem_limit_bytes=_vmem_limit(pipelined, 6 * _nbytes((tb * n1, m), F32))),
        name="fft_t1",
    )(u, g)


def _mix_ln_kernel(x_ref, ge_ref, be_ref, oh_ref, pr_ref, pi_ref, c3_ref, s3_ref, wo_ref,
                   g1_ref, b1_ref, h1_ref, pr_scr, pi_scr, wo_scr, *, alpha):
    gd = FOURIER_GDIM

    @pl.when(pl.program_id(0) == 0)
    def _():
        for r in range(0, wo_ref.shape[0], LN_ROWS):
            wo_scr[r:r + LN_ROWS, :] = wo_ref[r:r + LN_ROWS, :].astype(BF16)

    pr_scr[...] = _swap_leading(pr_ref[...], merge=True)
    pi_scr[...] = _swap_leading(pi_ref[...], merge=True)
    c3, s3 = c3_ref[...], s3_ref[...]
    parts = []
    for g in range(pr_scr.shape[1] // gd):
        sl = slice(g * gd, (g + 1) * gd)
        parts.append((_dot(pr_scr[:, sl], c3) + _dot(pi_scr[:, sl], s3)).astype(BF16))
    o_four = jnp.concatenate(parts, axis=1)
    wh = oh_ref.shape[1]
    mix = _dot(oh_ref[...], wo_scr[:wh, :]) + _dot(o_four, wo_scr[wh:, :])
    h0 = _layernorm(x_ref[...], ge_ref[...], be_ref[...])
    h1_ref[...] = _layernorm(alpha * h0 + mix, g1_ref[...], b1_ref[...])


def _mix_ln(x, ge, be, o_hgrn, p, c3, s3, wo, g1, b1, *, alpha, tm=512):
    t, d = x.shape
    wh = o_hgrn.shape[1]
    _, n1, n2, m = p.shape
    kr = tm // n1
    gd = FOURIER_GDIM
    const = lambda shape: pl.BlockSpec(shape, lambda i: tuple(0 for _ in shape),
                                       pipeline_mode=pl.Buffered(1))
    pipelined = (_nbytes((tm, d), F32) + _nbytes((tm, wh), BF16) + 2 * _nbytes((n1, kr, m), BF16)
                 + _nbytes((tm, d), F32))
    resident = (_nbytes((wh + m, d), BF16) + _nbytes((wh + m, d), F32)
                + 2 * _nbytes((tm, m), BF16) + 2 * _nbytes((tm, m), F32)
                + 3 * _nbytes((tm, d), F32))
    return pl.pallas_call(
        functools.partial(_mix_ln_kernel, alpha=alpha),
        grid=(t // tm,),
        in_specs=[
            pl.BlockSpec((tm, d), lambda i: (i, 0)),
            const((1, d)), const((1, d)),
            pl.BlockSpec((tm, wh), lambda i: (i, 0)),
            pl.BlockSpec((None, n1, kr, m), lambda i: (0, 0, i, 0)),
            pl.BlockSpec((None, n1, kr, m), lambda i: (1, 0, i, 0)),
            const((gd, gd)), const((gd, gd)),
            const((wh + m, d)),
            const((1, d)), const((1, d)),
        ],
        out_specs=pl.BlockSpec((tm, d), lambda i: (i, 0)),
        out_shape=jax.ShapeDtypeStruct((t, d), F32),
        scratch_shapes=[pltpu.VMEM((tm, m), BF16), pltpu.VMEM((tm, m), BF16), pltpu.VMEM((wh + m, d), BF16)],
        compiler_params=pltpu.CompilerParams(
            dimension_semantics=("arbitrary",),
            vmem_limit_bytes=_vmem_limit(pipelined, resident)),
        name="mix_ln",
    )(x, ge, be, o_hgrn, p, p, c3, s3, wo, g1, b1)


def _ffn_ln_kernel(h1_ref, wg_ref, wu_ref, wd_ref, g2_ref, b2_ref, o_ref, hb_scr, *, alpha):
    f = pl.program_id(1)

    def ffn_step(first):
        if first:
            hb = h1_ref[...].astype(BF16)
            hb_scr[...] = hb
        else:
            hb = hb_scr[...]
        gate = _dot(hb, wg_ref[...].astype(BF16))
        up = _dot(hb, wu_ref[...].astype(BF16))
        act = (gate * _sigmoid(gate) * up).astype(BF16)
        cw = FFN_OUT_SLAB
        for n in range(o_ref.shape[1] // cw):
            cols = slice(n * cw, (n + 1) * cw)
            part = _dot(act, wd_ref[:, cols].astype(BF16))
            if first:
                o_ref[:, cols] = alpha * h1_ref[:, cols] + part
            else:
                o_ref[:, cols] += part

    pl.when(f == 0)(functools.partial(ffn_step, True))
    pl.when(f > 0)(functools.partial(ffn_step, False))

    @pl.when(f == pl.num_programs(1) - 1)
    def _():
        for r in range(0, o_ref.shape[0], LN_ROWS):
            rows = slice(r, r + LN_ROWS)
            o_ref[rows, :] = _layernorm(o_ref[rows, :], g2_ref[...], b2_ref[...])


def _ffn_ln(h1, wg, wu, wd, g2, b2, *, alpha, tm=1024, tf=512):
    t, d = h1.shape
    dff = wg.shape[1]
    pipelined = _nbytes((tm, d), F32) + 2 * _nbytes((d, tf), F32) + _nbytes((tf, d), wd.dtype)
    resident = (_nbytes((tm, d), F32) + _nbytes((tm, d), BF16) + 2 * _nbytes((tm, tf), F32)
                + 2 * _nbytes((tm, FFN_OUT_SLAB), F32))
    return pl.pallas_call(
        functools.partial(_ffn_ln_kernel, alpha=alpha),
        grid=(t // tm, dff // tf),
        in_specs=[
            pl.BlockSpec((tm, d), lambda i, f: (i, 0)),
            pl.BlockSpec((d, tf), lambda i, f: (0, f)),
            pl.BlockSpec((d, tf), lambda i, f: (0, f)),
            pl.BlockSpec((tf, d), lambda i, f: (f, 0)),
            pl.BlockSpec((1, d), lambda i, f: (0, 0)),
            pl.BlockSpec((1, d), lambda i, f: (0, 0)),
        ],
        out_specs=pl.BlockSpec((tm, d), lambda i, f: (i, 0), pipeline_mode=pl.Buffered(1)),
        out_shape=jax.ShapeDtypeStruct((t, d), F32),
        scratch_shapes=[pltpu.VMEM((tm, d), BF16)],
        compiler_params=pltpu.CompilerParams(
            dimension_semantics=("parallel", "arbitrary"),
            vmem_limit_bytes=_vmem_limit(pipelined, resident)),
        name="ffn_ln",
    )(h1, wg, wu, wd, g2, b2)


def kernel(x, ln_emb_g, ln_emb_b, w_in, lb_fwd_logits, lb_bwd_logits, g_norm, w_out, ln1_g, ln1_b,
           w_gate, w_up, w_down, ln2_g, ln2_b):
    depth = w_in.shape[0]
    assert depth == 1, "the embedding LayerNorm is fused into the single layer's projection"
    batch, seq, d = x.shape
    assert batch == 1
    alpha = (2.0 * depth) ** 0.25
    layer = 0
    row = lambda a: a.reshape(1, -1).astype(F32)

    x2 = x.reshape(seq, d)
    log2f, one_minus_f, qig, four = _ln_proj(
        x2, row(ln_emb_g), row(ln_emb_b), lb_fwd_logits.astype(F32), lb_bwd_logits.astype(F32),
        w_in[layer], layer=layer)

    g1c, f2c, c3, s3 = _dft_constants(seq, FFT_N1)
    o_hgrn, p, w_down_bf16 = _hgrn_fft2(row(g_norm[layer]), qig, log2f, one_minus_f, _fft_t1(four, g1c), f2c,
                                        w_down[layer])

    h1 = _mix_ln(x2, row(ln_emb_g), row(ln_emb_b), o_hgrn, p, c3, s3, w_out[layer],
                 row(ln1_g[layer]), row(ln1_b[layer]), alpha=alpha)

    out = _ffn_ln(h1, w_gate[layer], w_up[layer], w_down_bf16,
                  row(ln2_g[layer]), row(ln2_b[layer]), alpha=alpha)
    return out.reshape(batch, seq, d)
```
